```python
import jax, jax.numpy as jnp
from jax import lax
import numpy as np

D_MODEL = 2048
BATCH = 1
SEQ = 16384
DEPTH = 1
DEC_BATCH = 32
DEC_SEQ = 32
PAST_LEN = 1024

CHUNK = 64
N_MEM = 256
D_CONV = D_MODEL // 2
D_RWKV = D_MODEL - D_CONV
D_MIX = D_CONV + D_RWKV
CONV_W = 3
RWKV_HEAD = 64
RWKV_HEADS = D_RWKV // RWKV_HEAD
DECAY_LORA = 64
A_LORA = 64
GATE_LORA = 160
D_SHIFT = 3 * D_RWKV + DECAY_LORA + A_LORA + GATE_LORA
D_IN = 3 * D_CONV + D_SHIFT
RW_SPLITS = (D_RWKV, 2 * D_RWKV, 3 * D_RWKV, 3 * D_RWKV + DECAY_LORA, 3 * D_RWKV + DECAY_LORA + A_LORA)
XA_HEADS = 4
XA_HEAD = D_MODEL // XA_HEADS
N_GROUPS = 4
EXP_PER_GROUP = 8
N_EXPERTS = N_GROUPS * EXP_PER_GROUP
TOP_K = 2
EXPERT_FF = 512
MOE_BLOCK = 128
LN_EPS = 1e-5
GN_EPS = 64e-5
ALPHA = (2 * DEPTH) ** 0.25
BETA = (8 * DEPTH) ** -0.25

kernel_name = 'hybrid_conv_rwkv7_hmoe_stream_step'


def _layer_norm(x, g, b, eps=LN_EPS):
    xf = x.astype(jnp.float32)
    mu = xf.mean(-1, keepdims=True)
    var = jnp.square(xf - mu).mean(-1, keepdims=True)
    return ((xf - mu) * lax.rsqrt(var + eps) * g + b).astype(x.dtype)


def _token_mixer(x, conv_buf, shift_buf, wkv, lp):
    f32 = jnp.float32
    bsz, t, _ = x.shape
    p = x @ lp['w_in']
    b_gate, c_gate, h, p_rw = jnp.split(p, [D_CONV, 2 * D_CONV, 3 * D_CONV], axis=-1)
    u = c_gate * h
    u_ext = jnp.concatenate([conv_buf.astype(u.dtype), u], axis=1)
    conv = lax.conv_general_dilated(u_ext, lp['conv_w'][:, None, :].astype(u.dtype), window_strides=(1,),
                                    padding='VALID', dimension_numbers=('NWC', 'WIO', 'NWC'),
                                    feature_group_count=D_CONV)
    y_conv = b_gate * conv
    new_conv = u_ext[:, -(CONV_W - 1):]
    prev = jnp.concatenate([shift_buf.astype(p_rw.dtype), p_rw[:, :-1]], axis=1)
    q = p_rw + lp['shift_mu'] * (prev - p_rw)
    new_shift = p_rw[:, -1:]
    r, k, v, qw, qa, qg = jnp.split(q, list(RW_SPLITS), axis=-1)
    w_log = -jax.nn.softplus(-(lp['decay_w0'] + jnp.tanh(qw) @ lp['decay_w2']).astype(f32)) - 0.5
    decay = jnp.exp(-jnp.exp(w_log))
    a = jax.nn.sigmoid((lp['iclr_a0'] + qa @ lp['iclr_a2']).astype(f32))
    g = jax.nn.sigmoid(qg) @ lp['gate_g2']
    hs = lambda z: z.reshape(bsz, t, RWKV_HEADS, RWKV_HEAD)
    hd = lambda z: z.astype(f32).reshape(RWKV_HEADS, RWKV_HEAD)
    r, k, v, a, decay = hs(r.astype(f32)), hs(k.astype(f32)), hs(v.astype(f32)), hs(a), hs(decay)
    kk = k * hd(lp['k_k'])
    kk = kk * lax.rsqrt(jnp.maximum(jnp.sum(kk * kk, -1, keepdims=True), 1e-24))
    k = k * (1.0 + (a - 1.0) * hd(lp['k_a']))
    tm = lambda z: jnp.moveaxis(z, 1, 0)

    def step(s, inp):
        r_t, d_t, k_t, v_t, kk_t, a_t = inp
        sa = jnp.einsum('bhvk,bhk->bhv', s, -kk_t)
        s = (s * d_t[:, :, None, :] + sa[..., None] * (kk_t * a_t)[:, :, None, :]
             + v_t[..., None] * k_t[:, :, None, :])
        return s, jnp.einsum('bhvk,bhk->bhv', s, r_t)

    s_fin, o = lax.scan(step, wkv.astype(f32), (tm(r), tm(decay), tm(k), tm(v), tm(kk), tm(a)))
    o = jnp.moveaxis(o, 0, 1)
    mu = o.mean(-1, keepdims=True)
    var = jnp.square(o - mu).mean(-1, keepdims=True)
    o = (o - mu) * lax.rsqrt(var + GN_EPS) * hd(lp['lnx_g']) + hd(lp['lnx_b'])
    o = o + jnp.sum(r * k * lp['r_k'].astype(f32), -1, keepdims=True) * v
    y_rwkv = o.reshape(bsz, t, D_RWKV).astype(x.dtype) * g
    y = jnp.concatenate([y_conv.astype(x.dtype), y_rwkv.astype(x.dtype)], axis=-1) @ lp['w_out']
    return y, new_conv.astype(x.dtype), new_shift.astype(x.dtype), s_fin.astype(x.dtype)


def _memory_kv(mem, xa_k, xa_v):
    b = mem.shape[0]
    mk = (mem @ xa_k).reshape(b, N_MEM, XA_HEADS, XA_HEAD)
    mv = (mem @ xa_v).reshape(b, N_MEM, XA_HEADS, XA_HEAD)
    return mk, mv


def _memory_attend(x, mk, mv, xa_q, xa_o):
    b, t, _ = x.shape
    q = (x @ xa_q).reshape(b, t, XA_HEADS, XA_HEAD)
    s = jnp.einsum('bthe,bmhe->bhtm', q, mk.astype(x.dtype)).astype(jnp.float32) * (XA_HEAD ** -0.5)
    pr = jax.nn.softmax(s, axis=-1).astype(x.dtype)
    o = jnp.einsum('bhtm,bmhe->bthe', pr, mv.astype(x.dtype)).reshape(b, t, D_MODEL)
    return o @ xa_o


def _hier_moe(x2, lp):
    f32 = jnp.float32
    n, d = x2.shape
    lg = (x2 @ lp['router_grp']).astype(f32) + lp['router_grp_b'].astype(f32)
    grp = jnp.argmax(lg, axis=-1).astype(jnp.int32)
    p_grp = jnp.take_along_axis(jax.nn.softmax(lg, axis=-1), grp[:, None], axis=-1)
    ls = jnp.einsum('nd,gde->nge', x2, lp['router_sub']).astype(f32) + lp['router_sub_b'].astype(f32)
    ls = jnp.take_along_axis(ls, grp[:, None, None], axis=1)[:, 0]
    top_v, top_j = lax.top_k(ls, TOP_K)
    gates = p_grp * jax.nn.softmax(top_v, axis=-1)
    eidx = (grp[:, None] * EXP_PER_GROUP + top_j).reshape(-1).astype(jnp.int32)
    a = n * TOP_K
    n_blk = -(-(a + N_EXPERTS * (MOE_BLOCK - 1)) // MOE_BLOCK)
    order = jnp.argsort(eidx)
    se = eidx[order]
    counts = jnp.zeros((N_EXPERTS,), jnp.int32).at[eidx].add(1)
    pcounts = (counts + MOE_BLOCK - 1) // MOE_BLOCK * MOE_BLOCK
    pend = jnp.cumsum(pcounts).astype(jnp.int32)
    cstart = (jnp.cumsum(counts) - counts).astype(jnp.int32)
    ar = jnp.arange(a, dtype=jnp.int32)
    dest_sorted = pend[se] - pcounts[se] + ar - cstart[se]
    dest = jnp.zeros((a,), jnp.int32).at[order].set(dest_sorted)
    slot_tok = jnp.full((n_blk * MOE_BLOCK,), n, jnp.int32).at[dest].set(ar // TOP_K)
    blk_e = jnp.minimum(jnp.searchsorted(pend, jnp.arange(n_blk, dtype=jnp.int32) * MOE_BLOCK, side='right'),
                        N_EXPERTS - 1).astype(jnp.int32)
    x_pad = jnp.concatenate([x2, jnp.zeros((1, d), x2.dtype)], axis=0)
    w1, w3, w2 = lp['moe_w1'], lp['moe_w3'], lp['moe_w2']

    def expert_block(args):
        toks, e = args
        xb = x_pad[toks]
        hdn = jax.nn.silu(xb @ w1[e]) * (xb @ w3[e])
        return hdn @ w2[e]

    yb = lax.map(expert_block, (slot_tok.reshape(n_blk, MOE_BLOCK), blk_e)).reshape(-1, d)
    y = yb[dest].reshape(n, TOP_K, d)
    return jnp.einsum('nkd,nk->nd', y, gates.astype(x2.dtype))


def _layer(x, mk, mv, conv_buf, shift_buf, wkv, lp):
    mix, new_conv, new_shift, new_wkv = _token_mixer(x, conv_buf, shift_buf, wkv, lp)
    x = _layer_norm(ALPHA * x + mix, lp['ln1_g'], lp['ln1_b'])
    x = _layer_norm(ALPHA * x + _memory_attend(x, mk, mv, lp['xa_q'], lp['xa_o']), lp['ln2_g'], lp['ln2_b'])
    ffn = _hier_moe(x.reshape(-1, D_MODEL), lp).reshape(x.shape)
    x = _layer_norm(ALPHA * x + ffn, lp['ln3_g'], lp['ln3_b'])
    return x, new_conv, new_shift, new_wkv


def setup_inputs(seed: int = 0) -> dict:
    key = jax.random.key(seed)
    ks = iter(jax.random.split(key, 48))
    f32 = jnp.float32
    L = DEPTH
    nrm = lambda shape, scale: jax.random.normal(next(ks), shape, f32) * scale
    uni = lambda shape, lo, hi: jax.random.uniform(next(ks), shape, f32, minval=lo, maxval=hi)
    return {
        'x_prompt': nrm((BATCH, SEQ, D_MODEL), 1.0),
        'x_sample': nrm((DEC_BATCH, DEC_SEQ, D_MODEL), 1.0),
        'mem_prompt': nrm((BATCH, N_MEM, D_MODEL), 1.0),
        'cache_mem_k': nrm((L, DEC_BATCH, N_MEM, XA_HEADS, XA_HEAD), 1.0),
        'cache_mem_v': nrm((L, DEC_BATCH, N_MEM, XA_HEADS, XA_HEAD), 1.0),
        'state_conv': nrm((L, DEC_BATCH, CONV_W - 1, D_CONV), 1.0),
        'state_shift': nrm((L, DEC_BATCH, 1, D_SHIFT), 1.0),
        'state_wkv': nrm((L, DEC_BATCH, RWKV_HEADS, RWKV_HEAD, RWKV_HEAD), 1.0),
        'w_in': nrm((L, D_MODEL, D_IN), D_MODEL ** -0.5),
        'conv_w': nrm((L, CONV_W, D_CONV), CONV_W ** -0.5),
        'shift_mu': uni((L, D_SHIFT), 0.0, 1.0),
        'decay_w0': uni((L, D_RWKV), -6.0, -0.5),
        'decay_w2': nrm((L, DECAY_LORA, D_RWKV), 0.1 * DECAY_LORA ** -0.5),
        'iclr_a0': nrm((L, D_RWKV), 0.1),
        'iclr_a2': nrm((L, A_LORA, D_RWKV), A_LORA ** -0.5),
        'gate_g2': nrm((L, GATE_LORA, D_RWKV), GATE_LORA ** -0.5),
        'k_k': 0.85 + nrm((L, D_RWKV), 0.02),
        'k_a': 1.0 + nrm((L, D_RWKV), 0.02),
        'r_k': -0.04 + nrm((L, RWKV_HEADS, RWKV_HEAD), 0.1),
        'lnx_g': 1.0 + nrm((L, D_RWKV), 0.02),
        'lnx_b': nrm((L, D_RWKV), 0.02),
        'w_out': nrm((L, D_MIX, D_MODEL), BETA * D_MIX ** -0.5),
        'ln1_g': 1.0 + nrm((L, D_MODEL), 0.02),
        'ln1_b': nrm((L, D_MODEL), 0.02),
        'xa_q': nrm((L, D_MODEL, D_MODEL), D_MODEL ** -0.5),
        'xa_k': nrm((L, D_MODEL, D_MODEL), D_MODEL ** -0.5),
        'xa_v': nrm((L, D_MODEL, D_MODEL), BETA * D_MODEL ** -0.5),
        'xa_o': nrm((L, D_MODEL, D_MODEL), BETA * D_MODEL ** -0.5),
        'ln2_g': 1.0 + nrm((L, D_MODEL), 0.02),
        'ln2_b': nrm((L, D_MODEL), 0.02),
        'router_grp': nrm((L, D_MODEL, N_GROUPS), D_MODEL ** -0.5),
        'router_grp_b': nrm((L, N_GROUPS), 0.01),
        'router_sub': nrm((L, N_GROUPS, D_MODEL, EXP_PER_GROUP), D_MODEL ** -0.5),
        'router_sub_b': nrm((L, N_GROUPS, EXP_PER_GROUP), 0.01),
        'moe_w1': nrm((L, N_EXPERTS, D_MODEL, EXPERT_FF), D_MODEL ** -0.5),
        'moe_w3': nrm((L, N_EXPERTS, D_MODEL, EXPERT_FF), D_MODEL ** -0.5),
        'moe_w2': nrm((L, N_EXPERTS, EXPERT_FF, D_MODEL), BETA * EXPERT_FF ** -0.5),
        'ln3_g': 1.0 + nrm((L, D_MODEL), 0.02),
        'ln3_b': nrm((L, D_MODEL), 0.02),
    }


def reference(x_prompt, x_sample, mem_prompt, cache_mem_k, cache_mem_v, state_conv, state_shift, state_wkv,
              w_in, conv_w, shift_mu, decay_w0, decay_w2, iclr_a0, iclr_a2, gate_g2, k_k, k_a, r_k,
              lnx_g, lnx_b, w_out, ln1_g, ln1_b, xa_q, xa_k, xa_v, xa_o, ln2_g, ln2_b,
              router_grp, router_grp_b, router_sub, router_sub_b, moe_w1, moe_w3, moe_w2, ln3_g, ln3_b):
    bp = x_prompt.shape[0]
    dt = x_prompt.dtype
    conv0 = jnp.zeros((bp, CONV_W - 1, D_CONV), dt)
    shift0 = jnp.zeros((bp, 1, D_SHIFT), dt)
    wkv0 = jnp.zeros((bp, RWKV_HEADS, RWKV_HEAD, RWKV_HEAD), dt)
    yp, ys = x_prompt, x_sample
    mk_p, mv_p, conv_p, shift_p, wkv_p, conv_s, shift_s, wkv_s = [], [], [], [], [], [], [], []
    for l in range(DEPTH):
        lp = dict(w_in=w_in[l], conv_w=conv_w[l], shift_mu=shift_mu[l], decay_w0=decay_w0[l],
                  decay_w2=decay_w2[l], iclr_a0=iclr_a0[l], iclr_a2=iclr_a2[l], gate_g2=gate_g2[l],
                  k_k=k_k[l], k_a=k_a[l], r_k=r_k[l], lnx_g=lnx_g[l], lnx_b=lnx_b[l], w_out=w_out[l],
                  ln1_g=ln1_g[l], ln1_b=ln1_b[l], xa_q=xa_q[l], xa_o=xa_o[l], ln2_g=ln2_g[l], ln2_b=ln2_b[l],
                  router_grp=router_grp[l], router_grp_b=router_grp_b[l], router_sub=router_sub[l],
                  router_sub_b=router_sub_b[l], moe_w1=moe_w1[l], moe_w3=moe_w3[l], moe_w2=moe_w2[l],
                  ln3_g=ln3_g[l], ln3_b=ln3_b[l])
        mk, mv = _memory_kv(mem_prompt, xa_k[l], xa_v[l])
        yp, c_p, s_p, w_p = _layer(yp, mk, mv, conv0, shift0, wkv0, lp)
        ys, c_s, s_s, w_s = _layer(ys, cache_mem_k[l], cache_mem_v[l], state_conv[l], state_shift[l],
                                   state_wkv[l], lp)
        mk_p.append(mk)
        mv_p.append(mv)
        conv_p.append(c_p)
        shift_p.append(s_p)
        wkv_p.append(w_p)
        conv_s.append(c_s)
        shift_s.append(s_s)
        wkv_s.append(w_s)
    return (yp, ys, jnp.stack(mk_p), jnp.stack(mv_p), jnp.stack(conv_p), jnp.stack(shift_p), jnp.stack(wkv_p),
            jnp.stack(conv_s), jnp.stack(shift_s), jnp.stack(wkv_s))
```

```python
import functools

import jax
import jax.numpy as jnp
from jax import lax
from jax.experimental import pallas as pl
from jax.experimental.pallas import tpu as pltpu

F32 = jnp.float32
BF16 = jnp.bfloat16

D_MODEL = 2048
D_CONV = 1024
D_RWKV = 1024
RWKV_HEAD = 64
RWKV_HEADS = 16
DECAY_LORA = 64
A_LORA = 64
GATE_LORA = 160
D_LORA = DECAY_LORA + A_LORA + GATE_LORA
D_SHIFT = 3 * D_RWKV + D_LORA
D_IN = 3 * D_CONV + D_SHIFT
LANE = 128
D_LORA_PAD = 384
D_SHIFT_PAD = 3 * D_RWKV + D_LORA_PAD
D_IN_PAD = 3 * D_CONV + D_SHIFT_PAD
N_MEM = 256
XA_HEADS = 4
XA_HEAD = 512
N_GROUPS = 4
EXP_PER_GROUP = 8
N_EXPERTS = 32
EXPERT_FF = 512
LN_EPS = 1e-5
GN_EPS = 64e-5
ALPHA = 2.0 ** 0.25
CHUNK = 64
ROUTE_OFF = N_GROUPS
VMEM_LIMIT = 56 * 1024 * 1024


def _cparams(*sem):
    return pltpu.CompilerParams(dimension_semantics=sem, vmem_limit_bytes=VMEM_LIMIT)


def _dot(a, b):
    return jnp.dot(a.astype(BF16), b.astype(BF16), preferred_element_type=F32)


def _dot_nt(a, b):
    return lax.dot_general(a.astype(BF16), b.astype(BF16), (((1,), (1,)), ((), ())), preferred_element_type=F32)


def _dot_tn(a, b):
    return lax.dot_general(a.astype(BF16), b.astype(BF16), (((0,), (0,)), ((), ())), preferred_element_type=F32)


def _split3(x):
    h1 = x.astype(BF16)
    r1 = x - h1.astype(F32)
    h2 = r1.astype(BF16)
    h3 = (r1 - h2.astype(F32)).astype(BF16)
    return h1, h2, h3


def _dot_exact_rhs(x, m):
    h1, h2, h3 = _split3(x)
    d = lambda h: jnp.dot(h, m, preferred_element_type=F32)
    return d(h1) + d(h2) + d(h3)


def _layer_norm(z, g, b):
    mu = jnp.mean(z, axis=-1, keepdims=True)
    d = z - mu
    var = jnp.mean(d * d, axis=-1, keepdims=True)
    return d * lax.rsqrt(var + LN_EPS) * g + b


def _sigmoid(x):
    return 1.0 / (1.0 + jnp.exp(-x))


def _tile(n, pref):
    t = min(n, pref)
    while t > 8 and (n % t or t % 8):
        t -= 8
    return t if n % t == 0 else n


def _proj_kernel(x_ref, w_ref, o_ref):
    o_ref[...] = _dot(x_ref[...], w_ref[...])


def _proj(x, w, tm, tn):
    n, k = x.shape
    m = w.shape[1]
    return pl.pallas_call(
        _proj_kernel,
        grid=(m // tn, n // tm),
        in_specs=[pl.BlockSpec((tm, k), lambda j, i: (i, 0)), pl.BlockSpec((k, tn), lambda j, i: (0, j))],
        out_specs=pl.BlockSpec((tm, tn), lambda j, i: (i, j)),
        out_shape=jax.ShapeDtypeStruct((n, m), F32),
        compiler_params=_cparams("arbitrary", "arbitrary"),
        name="proj",
    )(x, w)


def _shift_rows(x, heads):
    s = len(heads)
    out = pltpu.roll(x, s, axis=0)
    row = lax.broadcasted_iota(jnp.int32, x.shape, 0)
    for i, h in enumerate(heads):
        out = jnp.where(row == i, h, out)
    return out


def _segsum64(x, e):
    return jnp.concatenate([_dot_exact_rhs(x[:, j * LANE:(j + 1) * LANE], e) for j in range(x.shape[1] // LANE)], axis=1)


def _prep_kernel(p_ref, cst_ref, sst_ref, pv_ref, mu_ref, wl_ref, e_ref,
                 yc_ref, r_ref, k_ref, v_ref, w_ref, kk_ref, bb_ref, g_ref, nconv_ref, nshift_ref,
                 cu, cp):
    tt = p_ref.shape[1]

    @pl.when(pl.program_id(1) == 0)
    def _():
        cu[0:2, :] = cst_ref[0]
        cp[0:1, :] = sst_ref[0]

    bg = p_ref[0, :, 0:D_CONV]
    u = p_ref[0, :, D_CONV:2 * D_CONV] * p_ref[0, :, 2 * D_CONV:3 * D_CONV]
    u1 = _shift_rows(u, [cu[1:2, :]])
    u2 = _shift_rows(u, [cu[0:1, :], cu[1:2, :]])
    yc_ref[0] = bg * (pv_ref[4:5, :] * u2 + pv_ref[5:6, :] * u1 + pv_ref[6:7, :] * u)
    tail = u[tt - 2:tt, :]
    cu[0:2, :] = tail
    nconv_ref[0] = tail

    prw = p_ref[0, :, 3 * D_CONV:]
    prev = _shift_rows(prw, [cp[0:1, :]])
    q = prw + mu_ref[...] * (prev - prw)
    last = prw[tt - 1:tt, :]
    cp[0:1, :] = last
    nshift_ref[0] = last

    r = q[:, 0:D_RWKV]
    k = q[:, D_RWKV:2 * D_RWKV]
    v = q[:, 2 * D_RWKV:3 * D_RWKV]
    ql = q[:, 3 * D_RWKV:]
    lane = lax.broadcasted_iota(jnp.int32, ql.shape, 1)
    act = jnp.where(lane < DECAY_LORA, jnp.tanh(ql), jnp.where(lane < DECAY_LORA + A_LORA, ql, _sigmoid(ql)))
    lo = _dot(act, wl_ref[...])
    z = -(pv_ref[0:1, :] + lo[:, 0:D_RWKV])
    w_log = -(jnp.maximum(z, 0.0) + jnp.log(1.0 + jnp.exp(-jnp.abs(z)))) - 0.5
    a = _sigmoid(pv_ref[1:2, :] + lo[:, D_RWKV:2 * D_RWKV])
    kk = k * pv_ref[2:3, :]
    kk = kk * lax.rsqrt(jnp.maximum(_segsum64(kk * kk, e_ref[...]), 1e-24))
    r_ref[0] = r
    k_ref[0] = k * (1.0 + (a - 1.0) * pv_ref[3:4, :])
    v_ref[0] = v
    w_ref[0] = -jnp.exp(w_log)
    kk_ref[0] = kk
    bb_ref[0] = kk * a
    g_ref[0] = lo[:, 2 * D_RWKV:]


def _prep(p, conv_st, shift_st, pvec, mu, wl, e, tt):
    b, t, _ = p.shape
    row = lambda w: pl.BlockSpec((1, tt, w), lambda i, j: (i, j, 0))
    full = lambda a: pl.BlockSpec(a.shape, lambda i, j: (0,) * a.ndim)
    st = lambda a: pl.BlockSpec((1,) + a.shape[1:], lambda i, j: (i, 0, 0))
    seq = jax.ShapeDtypeStruct((b, t, D_RWKV), F32)
    return pl.pallas_call(
        _prep_kernel,
        grid=(b, t // tt),
        in_specs=[row(D_IN_PAD), st(conv_st), st(shift_st), full(pvec), full(mu), full(wl), full(e)],
        out_specs=[row(D_RWKV)] * 8 + [st(conv_st), st(shift_st)],
        out_shape=[seq] * 8 + [jax.ShapeDtypeStruct(conv_st.shape, F32), jax.ShapeDtypeStruct(shift_st.shape, F32)],
        scratch_shapes=[pltpu.VMEM((8, D_CONV), F32), pltpu.VMEM((8, D_SHIFT_PAD), F32)],
        compiler_params=_cparams("arbitrary", "arbitrary"),
        name="mixer_prep",
    )(p, conv_st, shift_st, pvec, mu, wl, e)


def _rwkv_kernel(r_ref, w_ref, k_ref, v_ref, kk_ref, bb_ref, g_ref, s0_ref, hv_ref, ms_ref, mi_ref, lt_ref, e_ref,
                 y_ref, sout_ref, S, *, C):
    t = pl.program_id(2)
    nt = pl.num_programs(2)
    H = RWKV_HEAD
    lane = lax.broadcasted_iota(jnp.int32, (1, LANE), 1)
    m0 = lane < H

    @pl.when(t == 0)
    def _():
        z = jnp.zeros((H, H), F32)
        S[0:H, :] = jnp.concatenate([s0_ref[0, 0], z], axis=1)
        S[H:2 * H, :] = jnp.concatenate([z, s0_ref[0, 1]], axis=1)

    rk = hv_ref[0:1, :]
    lng = hv_ref[1:2, :]
    lnb = hv_ref[2:3, :]
    pair = lambda x: jnp.concatenate([jnp.where(m0, x, 0.0), jnp.where(m0, 0.0, x)], axis=0)

    def chunk(c, carry):
        sl = pl.ds(pl.multiple_of(c * C, C), C)
        r = r_ref[0, sl, :]
        w = w_ref[0, sl, :]
        k = k_ref[0, sl, :]
        v = v_ref[0, sl, :]
        kk = kk_ref[0, sl, :]
        bb = bb_ref[0, sl, :]
        cw = jnp.dot(lt_ref[...], jnp.concatenate(_split3(w), axis=0), preferred_element_type=F32)
        p = jnp.exp(cw)
        ip = jnp.exp(-cw)
        pc = p[C - 1:C, :]
        la = pair(-(jnp.exp(cw - w) * kk))
        lr = pair(p * r)
        rb = pair(bb * ip)
        rkk = pair(k * ip)
        vst = pair(v)
        lhs = jnp.concatenate([la, lr], axis=0)
        rhs = jnp.concatenate([rb, rkk], axis=0)
        gram = _dot_nt(lhs, rhs)
        ms = ms_ref[...] > 0.5
        mi = mi_ref[...] > 0.5
        a_ab = jnp.where(ms, gram[:2 * C, :2 * C], 0.0)
        a_ak = jnp.where(ms, gram[:2 * C, 2 * C:], 0.0)
        a_rb = jnp.where(mi, gram[2 * C:, :2 * C], 0.0)
        a_rk = jnp.where(mi, gram[2 * C:, 2 * C:], 0.0)
        sm = S[...]
        ws = _dot_nt(lhs, sm)
        x = ws[:2 * C] + _dot(a_ak, vst)
        tm = (mi_ref[...] - ms_ref[...]) + a_ab
        pw = a_ab
        n = 1
        while 2 * n < C:
            pw = _dot(pw, pw)
            tm = tm + _dot(pw, tm)
            n *= 2
        u = _dot(tm, x)
        uv = jnp.concatenate([u, vst], axis=0)
        ost = ws[2 * C:] + _dot(jnp.concatenate([a_rb, a_rk], axis=1), uv)
        o = ost[:C] + ost[C:]
        S[...] = sm * pc + _dot_tn(uv, rhs * pc)
        e = e_ref[...]
        mu = _dot_exact_rhs(o, e) * (1.0 / H)
        d = o - mu
        var = _dot_exact_rhs(d * d, e) * (1.0 / H)
        on = d * lax.rsqrt(var + GN_EPS) * lng + lnb
        y_ref[0, sl, :] = (on + _dot_exact_rhs(r * k * rk, e) * v) * g_ref[0, sl, :]
        return carry

    lax.fori_loop(0, r_ref.shape[1] // C, chunk, 0)

    @pl.when(t == nt - 1)
    def _():
        sout_ref[0, 0] = S[0:H, 0:H]
        sout_ref[0, 1] = S[H:2 * H, H:2 * H]


def _rwkv(r, w, k, v, kk, bb, g, s0, hvec, consts, tc, C):
    b, t, _ = r.shape
    ms, mi, lt, e = consts
    seq = pl.BlockSpec((1, tc, LANE), lambda i, h, j: (i, j, h))
    st = pl.BlockSpec((1, 2, RWKV_HEAD, RWKV_HEAD), lambda i, h, j: (i, h, 0, 0))
    full = lambda a: pl.BlockSpec(a.shape, lambda i, h, j: (0,) * a.ndim)
    return pl.pallas_call(
        functools.partial(_rwkv_kernel, C=C),
        grid=(b, RWKV_HEADS // 2, t // tc),
        in_specs=[seq] * 7 + [st, pl.BlockSpec((8, LANE), lambda i, h, j: (0, h)), full(ms), full(mi), full(lt), full(e)],
        out_specs=[seq, st],
        out_shape=[jax.ShapeDtypeStruct((b, t, D_RWKV), F32), jax.ShapeDtypeStruct(s0.shape, F32)],
        scratch_shapes=[pltpu.VMEM((LANE, LANE), F32)],
        compiler_params=_cparams("arbitrary", "arbitrary", "arbitrary"),
        name="rwkv7",
    )(r, w, k, v, kk, bb, g, s0, hvec, ms, mi, lt, e)


def _rwkv_consts(C):
    i = jnp.arange(2 * C)
    same = (i[:, None] // C) == (i[None, :] // C)
    ms = (same & ((i[None, :] % C) < (i[:, None] % C))).astype(F32)
    mi = (same & ((i[None, :] % C) <= (i[:, None] % C))).astype(F32)
    j = jnp.arange(C)
    lt = jnp.tile((j[None, :] <= j[:, None]).astype(BF16), (1, 3))
    l = jnp.arange(LANE)
    e = ((l[:, None] // RWKV_HEAD) == (l[None, :] // RWKV_HEAD)).astype(BF16)
    return ms, mi, lt, e


def _out_ln_kernel(x_ref, yc_ref, yr_ref, w_ref, ln_ref, o_ref):
    z = ALPHA * x_ref[...] + _dot(yc_ref[...], w_ref[0:D_CONV, :]) + _dot(yr_ref[...], w_ref[D_CONV:, :])
    o_ref[...] = _layer_norm(z, ln_ref[0:1, :], ln_ref[1:2, :])


def _out_ln(x, yc, yr, w, ln, tm):
    n = x.shape[0]
    row = lambda wd: pl.BlockSpec((tm, wd), lambda i: (i, 0))
    full = lambda a: pl.BlockSpec(a.shape, lambda i: (0,) * a.ndim)
    return pl.pallas_call(
        _out_ln_kernel,
        grid=(n // tm,),
        in_specs=[row(D_MODEL), row(D_CONV), row(D_RWKV), full(w), full(ln)],
        out_specs=row(D_MODEL),
        out_shape=jax.ShapeDtypeStruct((n, D_MODEL), F32),
        compiler_params=_cparams("arbitrary"),
        name="out_proj_ln",
    )(x, yc, yr, w, ln)


def _attn_kernel(x_ref, mk_ref, mv_ref, wq_ref, wo_ref, ln_ref, o_ref):
    x = x_ref[0]
    q = _dot(x, wq_ref[...])
    heads = []
    for h in range(XA_HEADS):
        sl = slice(h * XA_HEAD, (h + 1) * XA_HEAD)
        s = _dot_nt(q[:, sl], mk_ref[0, :, sl]) * (XA_HEAD ** -0.5)
        s = s - jnp.max(s, axis=-1, keepdims=True)
        pe = jnp.exp(s)
        pr = pe / jnp.sum(pe, axis=-1, keepdims=True)
        heads.append(_dot(pr, mv_ref[0, :, sl]))
    o = jnp.concatenate(heads, axis=1)
    z = ALPHA * x + _dot(o, wo_ref[...])
    o_ref[0] = _layer_norm(z, ln_ref[0:1, :], ln_ref[1:2, :])


def _attn(x, mk, mv, wq, wo, ln, tq):
    b, t, _ = x.shape
    row = pl.BlockSpec((1, tq, D_MODEL), lambda i, j: (i, j, 0))
    mem = pl.BlockSpec((1, N_MEM, D_MODEL), lambda i, j: (i, 0, 0))
    full = lambda a: pl.BlockSpec(a.shape, lambda i, j: (0,) * a.ndim)
    return pl.pallas_call(
        _attn_kernel,
        grid=(b, t // tq),
        in_specs=[row, mem, mem, full(wq), full(wo), full(ln)],
        out_specs=row,
        out_shape=jax.ShapeDtypeStruct(x.shape, F32),
        compiler_params=_cparams("arbitrary", "arbitrary"),
        name="mem_attn_ln",
    )(x, mk, mv, wq, wo, ln)


def _router_kernel(x_ref, wr_ref, br_ref, lt_ref, route_ref, cnt_ref, cnt):
    i = pl.program_id(0)

    @pl.when(i == 0)
    def _():
        cnt[...] = jnp.zeros_like(cnt)

    neg = -1e30
    big = 1 << 20
    logits = _dot(x_ref[...], wr_ref[...]) + br_ref[...]
    lane = lax.broadcasted_iota(jnp.int32, logits.shape, 1)
    first_arg = lambda val, m: jnp.min(jnp.where(val == m, lane, big), axis=-1, keepdims=True)
    lg = jnp.where(lane < N_GROUPS, logits, neg)
    mg = jnp.max(lg, axis=-1, keepdims=True)
    grp = first_arg(lg, mg)
    p_grp = 1.0 / jnp.sum(jnp.exp(lg - mg), axis=-1, keepdims=True)
    lo = ROUTE_OFF + EXP_PER_GROUP * grp
    ls = jnp.where((lane >= lo) & (lane < lo + EXP_PER_GROUP), logits, neg)
    m1 = jnp.max(ls, axis=-1, keepdims=True)
    j1 = first_arg(ls, m1)
    ls2 = jnp.where(lane == j1, neg, ls)
    m2 = jnp.max(ls2, axis=-1, keepdims=True)
    j2 = first_arg(ls2, m2)
    t2 = jnp.exp(m2 - m1)
    g1 = p_grp * (1.0 / (1.0 + t2))
    g2 = p_grp * (t2 / (1.0 + t2))
    oh1 = lane == j1
    oh2 = lane == j2
    oh = jnp.where(oh1 | oh2, 1.0, 0.0)
    before = jnp.dot(lt_ref[...], oh.astype(BF16), preferred_element_type=F32) + cnt[0:1, :]
    rank1 = jnp.sum(jnp.where(oh1, before, 0.0), axis=-1, keepdims=True)
    rank2 = jnp.sum(jnp.where(oh2, before, 0.0), axis=-1, keepdims=True)
    total = cnt[0:1, :] + jnp.sum(oh, axis=0, keepdims=True)
    cnt[0:1, :] = total
    cnt_ref[...] = jnp.broadcast_to(total, cnt_ref.shape)
    e1 = (j1 - ROUTE_OFF).astype(F32)
    e2 = (j2 - ROUTE_OFF).astype(F32)
    out = jnp.zeros(logits.shape, F32)
    for col, val in enumerate((e1, e2, g1, g2, rank1, rank2)):
        out = jnp.where(lane == col, val, out)
    route_ref[...] = out


def _router(x, wr, br, tm):
    n = x.shape[0]
    j = jnp.arange(tm)
    lt = (j[None, :] < j[:, None]).astype(BF16)
    full = lambda a: pl.BlockSpec(a.shape, lambda i: (0,) * a.ndim)
    return pl.pallas_call(
        _router_kernel,
        grid=(n // tm,),
        in_specs=[pl.BlockSpec((tm, D_MODEL), lambda i: (i, 0)), full(wr), full(br), full(lt)],
        out_specs=[pl.BlockSpec((tm, LANE), lambda i: (i, 0)), pl.BlockSpec((8, LANE), lambda i: (0, 0))],
        out_shape=[jax.ShapeDtypeStruct((n, LANE), F32), jax.ShapeDtypeStruct((8, LANE), F32)],
        scratch_shapes=[pltpu.VMEM((8, LANE), F32)],
        compiler_params=_cparams("arbitrary"),
        name="router",
    )(x, wr, br, lt)


def _dispatch_kernel(dest_ref, x_hbm, xs_in, xs_hbm, sem, *, tm):
    del xs_in
    base = pl.program_id(0) * tm

    def copy(i, kslot):
        n = base + i
        return pltpu.make_async_copy(x_hbm.at[pl.ds(n, 1)], xs_hbm.at[pl.ds(dest_ref[2 * n + kslot], 1)], sem)

    def start(i, c):
        copy(i, 0).start()
        copy(i, 1).start()
        return c

    def wait(i, c):
        copy(i, 0).wait()
        copy(i, 1).wait()
        return c

    lax.fori_loop(0, tm, start, 0)
    lax.fori_loop(0, tm, wait, 0)


def _dispatch(dest, x, n_slots, tm):
    n = x.shape[0]
    xs0 = jnp.zeros((n_slots, D_MODEL), F32)
    return pl.pallas_call(
        functools.partial(_dispatch_kernel, tm=tm),
        grid_spec=pltpu.PrefetchScalarGridSpec(
            num_scalar_prefetch=1,
            grid=(n // tm,),
            in_specs=[pl.BlockSpec(memory_space=pl.ANY), pl.BlockSpec(memory_space=pl.ANY)],
            out_specs=pl.BlockSpec(memory_space=pl.ANY),
            scratch_shapes=[pltpu.SemaphoreType.DMA(())],
        ),
        out_shape=jax.ShapeDtypeStruct((n_slots, D_MODEL), F32),
        input_output_aliases={2: 0},
        compiler_params=_cparams("arbitrary"),
        name="moe_dispatch",
    )(dest, x, xs0)


def _expert_kernel(be_ref, na_ref, xs_ref, w1_ref, w3_ref, w2_ref, yb_ref):
    del be_ref

    active = pl.program_id(0) < na_ref[0]

    @pl.when(active)
    def _():
        x = xs_ref[...]
        h1 = _dot(x, w1_ref[0])
        h3 = _dot(x, w3_ref[0])
        yb_ref[...] = _dot(h1 * _sigmoid(h1) * h3, w2_ref[0])

    @pl.when(jnp.logical_not(active))
    def _():
        yb_ref[...] = jnp.zeros_like(yb_ref)


def _experts(blk_e, n_act, xs, w1, w3, w2, blk):
    n_slots = xs.shape[0]
    act = lambda i, be, na: jnp.minimum(i, na[0] - 1)
    slot = pl.BlockSpec((blk, D_MODEL), lambda i, be, na: (act(i, be, na), 0))
    return pl.pallas_call(
        _expert_kernel,
        grid_spec=pltpu.PrefetchScalarGridSpec(
            num_scalar_prefetch=2,
            grid=(n_slots // blk,),
            in_specs=[slot,
                      pl.BlockSpec((1, D_MODEL, EXPERT_FF), lambda i, be, na: (be[act(i, be, na)], 0, 0)),
                      pl.BlockSpec((1, D_MODEL, EXPERT_FF), lambda i, be, na: (be[act(i, be, na)], 0, 0)),
                      pl.BlockSpec((1, EXPERT_FF, D_MODEL), lambda i, be, na: (be[act(i, be, na)], 0, 0))],
            out_specs=pl.BlockSpec((blk, D_MODEL), lambda i, be, na: (i, 0)),
        ),
        out_shape=jax.ShapeDtypeStruct((n_slots, D_MODEL), F32),
        compiler_params=_cparams("arbitrary"),
        name="moe_experts",
    )(blk_e, n_act, xs, w1, w3, w2)


def _combine_kernel(dest_ref, x_ref, route_ref, yb_hbm, ln_ref, o_ref, buf, sem, *, tm):
    base = pl.program_id(0) * tm

    def copy(i, kslot):
        return pltpu.make_async_copy(yb_hbm.at[pl.ds(dest_ref[2 * (base + i) + kslot], 1)], buf.at[kslot, pl.ds(i, 1)], sem)

    def start(i, c):
        copy(i, 0).start()
        copy(i, 1).start()
        return c

    def wait(i, c):
        copy(i, 0).wait()
        copy(i, 1).wait()
        return c

    lax.fori_loop(0, tm, start, 0)
    lax.fori_loop(0, tm, wait, 0)
    y = route_ref[:, 2:3] * buf[0] + route_ref[:, 3:4] * buf[1]
    o_ref[...] = _layer_norm(ALPHA * x_ref[...] + y, ln_ref[0:1, :], ln_ref[1:2, :])


def _combine(dest, x, route, yb, ln, tm):
    n = x.shape[0]
    return pl.pallas_call(
        functools.partial(_combine_kernel, tm=tm),
        grid_spec=pltpu.PrefetchScalarGridSpec(
            num_scalar_prefetch=1,
            grid=(n // tm,),
            in_specs=[pl.BlockSpec((tm, D_MODEL), lambda i, d: (i, 0)),
                      pl.BlockSpec((tm, LANE), lambda i, d: (i, 0)),
                      pl.BlockSpec(memory_space=pl.ANY),
                      pl.BlockSpec(ln.shape, lambda i, d: (0, 0))],
            out_specs=pl.BlockSpec((tm, D_MODEL), lambda i, d: (i, 0)),
            scratch_shapes=[pltpu.VMEM((2, tm, D_MODEL), F32), pltpu.SemaphoreType.DMA(())],
        ),
        out_shape=jax.ShapeDtypeStruct((n, D_MODEL), F32),
        compiler_params=_cparams("arbitrary"),
        name="moe_combine_ln",
    )(dest, x, route, yb, ln)


def _hier_moe_ln(x, wts, blk):
    n = x.shape[0]
    route, cnt = _router(x, wts["wr"], wts["br"], _tile(n, 512))
    counts = cnt[0, ROUTE_OFF:ROUTE_OFF + N_EXPERTS].astype(jnp.int32)
    pcounts = (counts + blk - 1) // blk * blk
    pend = jnp.cumsum(pcounts)
    pstart = pend - pcounts
    eidx = route[:, 0:2].astype(jnp.int32)
    dest = (pstart[eidx] + route[:, 4:6].astype(jnp.int32)).reshape(-1)
    n_slots = -(-(2 * n + N_EXPERTS * (blk - 1)) // blk) * blk
    n_blk = n_slots // blk
    blk_e = jnp.minimum(jnp.searchsorted(pend, jnp.arange(n_blk, dtype=jnp.int32) * blk, side="right"), N_EXPERTS - 1).astype(jnp.int32)
    n_act = (pend[-1:] // blk).astype(jnp.int32)
    xs = _dispatch(dest, x, n_slots, _tile(n, 512))
    yb = _experts(blk_e, n_act, xs, wts["w1"], wts["w3"], wts["w2"], blk)
    return _combine(dest, x, route, yb, wts["ln3"], _tile(n, 256))


def _layer(x, mk, mv, conv_st, shift_st, wkv, wts):
    b, t, _ = x.shape
    n = b * t
    x2 = x.reshape(n, D_MODEL)
    p = _proj(x2, wts["w_in"], _tile(n, 512), D_IN_PAD // 3).reshape(b, t, D_IN_PAD)
    shift_pad = jnp.pad(shift_st, ((0, 0), (0, 0), (0, D_SHIFT_PAD - D_SHIFT)))
    C = min(CHUNK, t)
    consts = _rwkv_consts(C)
    yc, r, k, v, w, kk, bb, g, nconv, nshift = _prep(p, conv_st, shift_pad, wts["pvec"], wts["mu"], wts["wl"], consts[3], _tile(t, 256))
    yr, nwkv = _rwkv(r, w, k, v, kk, bb, g, wkv, wts["hvec"], consts, _tile(t, 512), C)
    tm = _tile(n, 256)
    x1 = _out_ln(x2, yc.reshape(n, D_CONV), yr.reshape(n, D_RWKV), wts["w_out"], wts["ln1"], tm)
    xa = _attn(x1.reshape(b, t, D_MODEL), mk, mv, wts["xa_q"], wts["xa_o"], wts["ln2"], _tile(t, 256))
    y = _hier_moe_ln(xa.reshape(n, D_MODEL), wts, 256 if n >= 8192 else 128)
    return y.reshape(b, t, D_MODEL), nconv, nshift[:, :, :D_SHIFT], nwkv


def _ln_rows(g, b):
    return jnp.zeros((8, D_MODEL), F32).at[0].set(g).at[1].set(b)


def kernel(x_prompt, x_sample, mem_prompt, cache_mem_k, cache_mem_v, state_conv, state_shift, state_wkv, w_in, conv_w, shift_mu, decay_w0, decay_w2, iclr_a0, iclr_a2, gate_g2, k_k, k_a, r_k, lnx_g, lnx_b, w_out, ln1_g, ln1_b, xa_q, xa_k, xa_v, xa_o, ln2_g, ln2_b, router_grp, router_grp_b, router_sub, router_sub_b, moe_w1, moe_w3, moe_w2, ln3_g, ln3_b):
    depth = w_in.shape[0]
    bp = x_prompt.shape[0]
    bs = x_sample.shape[0]
    yp, ys = x_prompt, x_sample
    outs = [[] for _ in range(8)]
    for l in range(depth):
        wl = jnp.zeros((D_LORA_PAD, 3 * D_RWKV), F32)
        wl = wl.at[0:DECAY_LORA, 0:D_RWKV].set(decay_w2[l])
        wl = wl.at[DECAY_LORA:DECAY_LORA + A_LORA, D_RWKV:2 * D_RWKV].set(iclr_a2[l])
        wl = wl.at[DECAY_LORA + A_LORA:D_LORA, 2 * D_RWKV:].set(gate_g2[l])
        wr = jnp.zeros((D_MODEL, LANE), F32)
        wr = wr.at[:, 0:N_GROUPS].set(router_grp[l])
        wr = wr.at[:, ROUTE_OFF:ROUTE_OFF + N_EXPERTS].set(jnp.transpose(router_sub[l], (1, 0, 2)).reshape(D_MODEL, N_EXPERTS))
        br = jnp.zeros((1, LANE), F32)
        br = br.at[0, 0:N_GROUPS].set(router_grp_b[l]).at[0, ROUTE_OFF:ROUTE_OFF + N_EXPERTS].set(router_sub_b[l].reshape(-1))
        wts = dict(
            w_in=jnp.pad(w_in[l], ((0, 0), (0, D_IN_PAD - D_IN))).astype(BF16),
            pvec=jnp.zeros((8, D_RWKV), F32).at[0].set(decay_w0[l]).at[1].set(iclr_a0[l]).at[2].set(k_k[l]).at[3].set(k_a[l]).at[4:7].set(conv_w[l]),
            mu=jnp.pad(shift_mu[l], (0, D_SHIFT_PAD - D_SHIFT)).reshape(1, D_SHIFT_PAD),
            wl=wl.astype(BF16),
            hvec=jnp.zeros((8, D_RWKV), F32).at[0].set(r_k[l].reshape(-1)).at[1].set(lnx_g[l]).at[2].set(lnx_b[l]),
            w_out=w_out[l].astype(BF16), ln1=_ln_rows(ln1_g[l], ln1_b[l]),
            xa_q=xa_q[l].astype(BF16), xa_o=xa_o[l].astype(BF16), ln2=_ln_rows(ln2_g[l], ln2_b[l]),
            wr=wr.astype(BF16), br=br,
            w1=moe_w1[l].astype(BF16), w3=moe_w3[l].astype(BF16), w2=moe_w2[l].astype(BF16), ln3=_ln_rows(ln3_g[l], ln3_b[l]),
        )
        mem2 = mem_prompt.reshape(bp * N_MEM, D_MODEL)
        mk = _proj(mem2, xa_k[l].astype(BF16), _tile(bp * N_MEM, 256), 1024).reshape(bp, N_MEM, D_MODEL)
        mv = _proj(mem2, xa_v[l].astype(BF16), _tile(bp * N_MEM, 256), 1024).reshape(bp, N_MEM, D_MODEL)
        yp, c_p, s_p, w_p = _layer(yp, mk, mv, jnp.zeros((bp, 2, D_CONV), F32), jnp.zeros((bp, 1, D_SHIFT), F32),
                                   jnp.zeros((bp, RWKV_HEADS, RWKV_HEAD, RWKV_HEAD), F32), wts)
        ys, c_s, s_s, w_s = _layer(ys, cache_mem_k[l].reshape(bs, N_MEM, D_MODEL), cache_mem_v[l].reshape(bs, N_MEM, D_MODEL),
                                   state_conv[l], state_shift[l], state_wkv[l], wts)
        for o, val in zip(outs, (mk.reshape(bp, N_MEM, XA_HEADS, XA_HEAD), mv.reshape(bp, N_MEM, XA_HEADS, XA_HEAD),
                                 c_p, s_p, w_p, c_s, s_s, w_s)):
            o.append(val)
    return (yp, ys) + tuple(jnp.stack(o) for o in outs)
```

```python
import functools

import jax
import jax.numpy as jnp
from jax import lax
from jax.experimental import pallas as pl
from jax.experimental.pallas import tpu as pltpu

F32 = jnp.float32
BF16 = jnp.bfloat16

D_MODEL = 2048
D_CONV = 1024
D_RWKV = 1024
RWKV_HEAD = 64
RWKV_HEADS = 16
DECAY_LORA = 64
A_LORA = 64
GATE_LORA = 160
D_LORA = DECAY_LORA + A_LORA + GATE_LORA
D_SHIFT = 3 * D_RWKV + D_LORA
D_IN = 3 * D_CONV + D_SHIFT
LANE = 128
D_LORA_PAD = 384
D_SHIFT_PAD = 3 * D_RWKV + D_LORA_PAD
D_IN_PAD = 3 * D_CONV + D_SHIFT_PAD
N_MEM = 256
XA_HEADS = 4
XA_HEAD = 512
N_GROUPS = 4
EXP_PER_GROUP = 8
N_EXPERTS = 32
EXPERT_FF = 512
LN_EPS = 1e-5
GN_EPS = 64e-5
ALPHA = 2.0 ** 0.25
CHUNK = 64
ROUTE_OFF = N_GROUPS
VMEM_LIMIT = 56 * 1024 * 1024


def _cparams(*sem):
    return pltpu.CompilerParams(dimension_semantics=sem, vmem_limit_bytes=VMEM_LIMIT)


def _dot(a, b):
    return jnp.dot(a.astype(BF16), b.astype(BF16), preferred_element_type=F32)


def _dot_nt(a, b):
    return lax.dot_general(a.astype(BF16), b.astype(BF16), (((1,), (1,)), ((), ())), preferred_element_type=F32)


def _dot_tn(a, b):
    return lax.dot_general(a.astype(BF16), b.astype(BF16), (((0,), (0,)), ((), ())), preferred_element_type=F32)


def _split3(x):
    h1 = x.astype(BF16)
    r1 = x - h1.astype(F32)
    h2 = r1.astype(BF16)
    h3 = (r1 - h2.astype(F32)).astype(BF16)
    return h1, h2, h3


def _dot_exact_rhs(x, m3):
    return jnp.dot(jnp.concatenate(_split3(x), axis=1), m3, preferred_element_type=F32)


def _layer_norm(z, g, b):
    mu = jnp.mean(z, axis=-1, keepdims=True)
    d = z - mu
    var = jnp.mean(d * d, axis=-1, keepdims=True)
    return d * lax.rsqrt(var + LN_EPS) * g + b


def _sigmoid(x):
    return 1.0 / (1.0 + jnp.exp(-x))


def _tile(n, pref):
    t = min(n, pref)
    while t > 8 and (n % t or t % 8):
        t -= 8
    return t if n % t == 0 else n


def _proj_kernel(x_ref, w_ref, o_ref):
    o_ref[...] = _dot(x_ref[...], w_ref[...])


def _proj(x, w, tm, tn):
    n, k = x.shape
    m = w.shape[1]
    return pl.pallas_call(
        _proj_kernel,
        grid=(m // tn, n // tm),
        in_specs=[pl.BlockSpec((tm, k), lambda j, i: (i, 0)), pl.BlockSpec((k, tn), lambda j, i: (0, j))],
        out_specs=pl.BlockSpec((tm, tn), lambda j, i: (i, j)),
        out_shape=jax.ShapeDtypeStruct((n, m), F32),
        compiler_params=_cparams("arbitrary", "arbitrary"),
        name="proj",
    )(x, w)


def _shift_rows(x, heads):
    s = len(heads)
    out = pltpu.roll(x, s, axis=0)
    row = lax.broadcasted_iota(jnp.int32, x.shape, 0)
    for i, h in enumerate(heads):
        out = jnp.where(row == i, h, out)
    return out


def _segsum64(x, e):
    return jnp.concatenate([_dot_exact_rhs(x[:, j * LANE:(j + 1) * LANE], e) for j in range(x.shape[1] // LANE)], axis=1)


def _prep_kernel(p_ref, cst_ref, sst_ref, pv_ref, mu_ref, wl_ref, e_ref,
                 yc_ref, r_ref, k_ref, v_ref, w_ref, kk_ref, bb_ref, g_ref, nconv_ref, nshift_ref,
                 cu, cp):
    tt = p_ref.shape[1]

    @pl.when(pl.program_id(1) == 0)
    def _():
        cu[0:2, :] = cst_ref[0]
        cp[0:1, :] = sst_ref[0]

    bg = p_ref[0, :, 0:D_CONV]
    u = p_ref[0, :, D_CONV:2 * D_CONV] * p_ref[0, :, 2 * D_CONV:3 * D_CONV]
    u1 = _shift_rows(u, [cu[1:2, :]])
    u2 = _shift_rows(u, [cu[0:1, :], cu[1:2, :]])
    yc_ref[0] = bg * (pv_ref[4:5, :] * u2 + pv_ref[5:6, :] * u1 + pv_ref[6:7, :] * u)
    tail = u[tt - 2:tt, :]
    cu[0:2, :] = tail
    nconv_ref[0] = tail

    prw = p_ref[0, :, 3 * D_CONV:]
    prev = _shift_rows(prw, [cp[0:1, :]])
    q = prw + mu_ref[...] * (prev - prw)
    last = prw[tt - 1:tt, :]
    cp[0:1, :] = last
    nshift_ref[0] = last

    r = q[:, 0:D_RWKV]
    k = q[:, D_RWKV:2 * D_RWKV]
    v = q[:, 2 * D_RWKV:3 * D_RWKV]
    ql = q[:, 3 * D_RWKV:]
    lane = lax.broadcasted_iota(jnp.int32, ql.shape, 1)
    act = jnp.where(lane < DECAY_LORA, jnp.tanh(ql), jnp.where(lane < DECAY_LORA + A_LORA, ql, _sigmoid(ql)))
    lo = _dot(act, wl_ref[...])
    z = -(pv_ref[0:1, :] + lo[:, 0:D_RWKV])
    w_log = -(jnp.maximum(z, 0.0) + jnp.log(1.0 + jnp.exp(-jnp.abs(z)))) - 0.5
    a = _sigmoid(pv_ref[1:2, :] + lo[:, D_RWKV:2 * D_RWKV])
    kk = k * pv_ref[2:3, :]
    kk = kk * lax.rsqrt(jnp.maximum(_segsum64(kk * kk, e_ref[...]), 1e-24))
    r_ref[0] = r
    k_ref[0] = k * (1.0 + (a - 1.0) * pv_ref[3:4, :])
    v_ref[0] = v
    w_ref[0] = -jnp.exp(w_log)
    kk_ref[0] = kk
    bb_ref[0] = kk * a
    g_ref[0] = lo[:, 2 * D_RWKV:]


def _prep(p, conv_st, shift_st, pvec, mu, wl, e, tt):
    b, t, _ = p.shape
    row = lambda w: pl.BlockSpec((1, tt, w), lambda i, j: (i, j, 0))
    full = lambda a: pl.BlockSpec(a.shape, lambda i, j: (0,) * a.ndim)
    st = lambda a: pl.BlockSpec((1,) + a.shape[1:], lambda i, j: (i, 0, 0))
    seq = jax.ShapeDtypeStruct((b, t, D_RWKV), F32)
    return pl.pallas_call(
        _prep_kernel,
        grid=(b, t // tt),
        in_specs=[row(D_IN_PAD), st(conv_st), st(shift_st), full(pvec), full(mu), full(wl), full(e)],
        out_specs=[row(D_RWKV)] * 8 + [st(conv_st), st(shift_st)],
        out_shape=[seq] * 8 + [jax.ShapeDtypeStruct(conv_st.shape, F32), jax.ShapeDtypeStruct(shift_st.shape, F32)],
        scratch_shapes=[pltpu.VMEM((8, D_CONV), F32), pltpu.VMEM((8, D_SHIFT_PAD), F32)],
        compiler_params=_cparams("arbitrary", "arbitrary"),
        name="mixer_prep",
    )(p, conv_st, shift_st, pvec, mu, wl, e)


N_PAIRS = RWKV_HEADS // 2


def _rwkv_kernel(r_ref, w_ref, k_ref, v_ref, kk_ref, bb_ref, g_ref, s0_ref, hv_ref, ms_ref, mi_ref, lt_ref, e_ref,
                 y_ref, sout_ref, S, *, C):
    t = pl.program_id(1)
    nt = pl.num_programs(1)
    H = RWKV_HEAD
    lane = lax.broadcasted_iota(jnp.int32, (1, LANE), 1)
    m0 = lane < H

    @pl.when(t == 0)
    def _():
        z = jnp.zeros((H, H), F32)
        for hp in range(N_PAIRS):
            S[hp, 0:H, :] = jnp.concatenate([s0_ref[0, 2 * hp], z], axis=1)
            S[hp, H:2 * H, :] = jnp.concatenate([z, s0_ref[0, 2 * hp + 1]], axis=1)

    pair = lambda x: jnp.concatenate([jnp.where(m0, x, 0.0), jnp.where(m0, 0.0, x)], axis=0)

    def pair_chunk(hp, sl):
        ln = slice(hp * LANE, (hp + 1) * LANE)
        rk = hv_ref[0:1, ln]
        lng = hv_ref[1:2, ln]
        lnb = hv_ref[2:3, ln]
        r = r_ref[0, sl, ln]
        w = w_ref[0, sl, ln]
        k = k_ref[0, sl, ln]
        v = v_ref[0, sl, ln]
        kk = kk_ref[0, sl, ln]
        bb = bb_ref[0, sl, ln]
        cw = jnp.dot(lt_ref[...], jnp.concatenate(_split3(w), axis=0), preferred_element_type=F32)
        yield
        p = jnp.exp(cw)
        ip = jnp.exp(-cw)
        pc = p[C - 1:C, :]
        la = pair(-(jnp.exp(cw - w) * kk))
        lr = pair(p * r)
        rb = pair(bb * ip)
        rkk = pair(k * ip)
        vst = pair(v)
        lhs = jnp.concatenate([la, lr], axis=0)
        rhs = jnp.concatenate([rb, rkk], axis=0)
        gram = _dot_nt(lhs, rhs)
        sm = S[hp]
        ws = _dot_nt(lhs, sm)
        yield
        ms = ms_ref[...] > 0.5
        mi = mi_ref[...] > 0.5
        a_ab = jnp.where(ms, gram[:2 * C, :2 * C], 0.0)
        a_ak = jnp.where(ms, gram[:2 * C, 2 * C:], 0.0)
        a_rb = jnp.where(mi, gram[2 * C:, :2 * C], 0.0)
        a_rk = jnp.where(mi, gram[2 * C:, 2 * C:], 0.0)
        x = ws[:2 * C] + _dot(a_ak, vst)
        tm = (mi_ref[...] - ms_ref[...]) + a_ab
        pw = a_ab
        n = 1
        while 2 * n < C:
            pw = _dot(pw, pw)
            yield
            tm = tm + _dot(pw, tm)
            yield
            n *= 2
        u = _dot(tm, x)
        yield
        uv = jnp.concatenate([u, vst], axis=0)
        ost = ws[2 * C:] + _dot(jnp.concatenate([a_rb, a_rk], axis=1), uv)
        o = ost[:C] + ost[C:]
        S[hp] = sm * pc + _dot_tn(uv, rhs * pc)
        yield
        e3 = e_ref[...]
        sums = _dot_exact_rhs(jnp.concatenate([o, r * k * rk], axis=0), e3)
        yield
        d = o - sums[:C] * (1.0 / H)
        var = _dot_exact_rhs(d * d, e3) * (1.0 / H)
        yield
        on = d * lax.rsqrt(var + GN_EPS) * lng + lnb
        y_ref[0, sl, ln] = (on + sums[C:] * v) * g_ref[0, sl, ln]

    def chunk(c, carry):
        sl = pl.ds(pl.multiple_of(c * C, C), C)
        live = [pair_chunk(hp, sl) for hp in range(N_PAIRS)]
        while live:
            live = [g for g in live if next(g, live) is not live]
        return carry

    lax.fori_loop(0, r_ref.shape[1] // C, chunk, 0)

    @pl.when(t == nt - 1)
    def _():
        for hp in range(N_PAIRS):
            sout_ref[0, 2 * hp] = S[hp, 0:H, 0:H]
            sout_ref[0, 2 * hp + 1] = S[hp, H:2 * H, H:2 * H]


def _rwkv(r, w, k, v, kk, bb, g, s0, hvec, consts, tc, C):
    b, t, _ = r.shape
    ms, mi, lt, e3 = consts
    seq = pl.BlockSpec((1, tc, D_RWKV), lambda i, j: (i, j, 0))
    st = pl.BlockSpec((1, RWKV_HEADS, RWKV_HEAD, RWKV_HEAD), lambda i, j: (i, 0, 0, 0))
    full = lambda a: pl.BlockSpec(a.shape, lambda i, j: (0,) * a.ndim)
    return pl.pallas_call(
        functools.partial(_rwkv_kernel, C=C),
        grid=(b, t // tc),
        in_specs=[seq] * 7 + [st, full(hvec), full(ms), full(mi), full(lt), full(e3)],
        out_specs=[seq, st],
        out_shape=[jax.ShapeDtypeStruct((b, t, D_RWKV), F32), jax.ShapeDtypeStruct(s0.shape, F32)],
        scratch_shapes=[pltpu.VMEM((N_PAIRS, LANE, LANE), F32)],
        compiler_params=_cparams("arbitrary", "arbitrary"),
        name="rwkv7",
    )(r, w, k, v, kk, bb, g, s0, hvec, ms, mi, lt, e3)


def _rwkv_consts(C):
    i = jnp.arange(2 * C)
    same = (i[:, None] // C) == (i[None, :] // C)
    ms = (same & ((i[None, :] % C) < (i[:, None] % C))).astype(F32)
    mi = (same & ((i[None, :] % C) <= (i[:, None] % C))).astype(F32)
    j = jnp.arange(C)
    lt = jnp.tile((j[None, :] <= j[:, None]).astype(BF16), (1, 3))
    l = jnp.arange(LANE)
    e3 = jnp.tile(((l[:, None] // RWKV_HEAD) == (l[None, :] // RWKV_HEAD)).astype(BF16), (3, 1))
    return ms, mi, lt, e3


def _out_ln_kernel(x_ref, yc_ref, yr_ref, w_ref, ln_ref, o_ref):
    z = ALPHA * x_ref[...] + _dot(yc_ref[...], w_ref[0:D_CONV, :]) + _dot(yr_ref[...], w_ref[D_CONV:, :])
    o_ref[...] = _layer_norm(z, ln_ref[0:1, :], ln_ref[1:2, :])


def _out_ln(x, yc, yr, w, ln, tm):
    n = x.shape[0]
    row = lambda wd: pl.BlockSpec((tm, wd), lambda i: (i, 0))
    full = lambda a: pl.BlockSpec(a.shape, lambda i: (0,) * a.ndim)
    return pl.pallas_call(
        _out_ln_kernel,
        grid=(n // tm,),
        in_specs=[row(D_MODEL), row(D_CONV), row(D_RWKV), full(w), full(ln)],
        out_specs=row(D_MODEL),
        out_shape=jax.ShapeDtypeStruct((n, D_MODEL), F32),
        compiler_params=_cparams("arbitrary"),
        name="out_proj_ln",
    )(x, yc, yr, w, ln)


def _attn_kernel(x_ref, mk_ref, mv_ref, wq_ref, wo_ref, ln_ref, o_ref):
    x = x_ref[0]
    q = _dot(x, wq_ref[...])
    heads = []
    for h in range(XA_HEADS):
        sl = slice(h * XA_HEAD, (h + 1) * XA_HEAD)
        s = _dot_nt(q[:, sl], mk_ref[0, :, sl]) * (XA_HEAD ** -0.5)
        s = s - jnp.max(s, axis=-1, keepdims=True)
        pe = jnp.exp(s)
        pr = pe / jnp.sum(pe, axis=-1, keepdims=True)
        heads.append(_dot(pr, mv_ref[0, :, sl]))
    o = jnp.concatenate(heads, axis=1)
    z = ALPHA * x + _dot(o, wo_ref[...])
    o_ref[0] = _layer_norm(z, ln_ref[0:1, :], ln_ref[1:2, :])


def _attn(x, mk, mv, wq, wo, ln, tq):
    b, t, _ = x.shape
    row = pl.BlockSpec((1, tq, D_MODEL), lambda i, j: (i, j, 0))
    mem = pl.BlockSpec((1, N_MEM, D_MODEL), lambda i, j: (i, 0, 0))
    full = lambda a: pl.BlockSpec(a.shape, lambda i, j: (0,) * a.ndim)
    return pl.pallas_call(
        _attn_kernel,
        grid=(b, t // tq),
        in_specs=[row, mem, mem, full(wq), full(wo), full(ln)],
        out_specs=row,
        out_shape=jax.ShapeDtypeStruct(x.shape, F32),
        compiler_params=_cparams("arbitrary", "arbitrary"),
        name="mem_attn_ln",
    )(x, mk, mv, wq, wo, ln)


def _router_kernel(x_ref, wr_ref, br_ref, lt_ref, route_ref, cnt_ref, cnt):
    i = pl.program_id(0)

    @pl.when(i == 0)
    def _():
        cnt[...] = jnp.zeros_like(cnt)

    neg = -1e30
    big = 1 << 20
    logits = _dot(x_ref[...], wr_ref[...]) + br_ref[...]
    lane = lax.broadcasted_iota(jnp.int32, logits.shape, 1)
    first_arg = lambda val, m: jnp.min(jnp.where(val == m, lane, big), axis=-1, keepdims=True)
    lg = jnp.where(lane < N_GROUPS, logits, neg)
    mg = jnp.max(lg, axis=-1, keepdims=True)
    grp = first_arg(lg, mg)
    p_grp = 1.0 / jnp.sum(jnp.exp(lg - mg), axis=-1, keepdims=True)
    lo = ROUTE_OFF + EXP_PER_GROUP * grp
    ls = jnp.where((lane >= lo) & (lane < lo + EXP_PER_GROUP), logits, neg)
    m1 = jnp.max(ls, axis=-1, keepdims=True)
    j1 = first_arg(ls, m1)
    ls2 = jnp.where(lane == j1, neg, ls)
    m2 = jnp.max(ls2, axis=-1, keepdims=True)
    j2 = first_arg(ls2, m2)
    t2 = jnp.exp(m2 - m1)
    g1 = p_grp * (1.0 / (1.0 + t2))
    g2 = p_grp * (t2 / (1.0 + t2))
    oh1 = lane == j1
    oh2 = lane == j2
    oh = jnp.where(oh1 | oh2, 1.0, 0.0)
    before = jnp.dot(lt_ref[...], oh.astype(BF16), preferred_element_type=F32) + cnt[0:1, :]
    rank1 = jnp.sum(jnp.where(oh1, before, 0.0), axis=-1, keepdims=True)
    rank2 = jnp.sum(jnp.where(oh2, before, 0.0), axis=-1, keepdims=True)
    total = cnt[0:1, :] + jnp.sum(oh, axis=0, keepdims=True)
    cnt[0:1, :] = total
    cnt_ref[...] = jnp.broadcast_to(total, cnt_ref.shape)
    e1 = (j1 - ROUTE_OFF).astype(F32)
    e2 = (j2 - ROUTE_OFF).astype(F32)
    out = jnp.zeros(logits.shape, F32)
    for col, val in enumerate((e1, e2, g1, g2, rank1, rank2)):
        out = jnp.where(lane == col, val, out)
    route_ref[...] = out


def _router(x, wr, br, tm):
    n = x.shape[0]
    j = jnp.arange(tm)
    lt = (j[None, :] < j[:, None]).astype(BF16)
    full = lambda a: pl.BlockSpec(a.shape, lambda i: (0,) * a.ndim)
    return pl.pallas_call(
        _router_kernel,
        grid=(n // tm,),
        in_specs=[pl.BlockSpec((tm, D_MODEL), lambda i: (i, 0)), full(wr), full(br), full(lt)],
        out_specs=[pl.BlockSpec((tm, LANE), lambda i: (i, 0)), pl.BlockSpec((8, LANE), lambda i: (0, 0))],
        out_shape=[jax.ShapeDtypeStruct((n, LANE), F32), jax.ShapeDtypeStruct((8, LANE), F32)],
        scratch_shapes=[pltpu.VMEM((8, LANE), F32)],
        compiler_params=_cparams("arbitrary"),
        name="router",
    )(x, wr, br, lt)


def _slotmap_kernel(dest_ref, slot_ref):
    def clear(s, c):
        slot_ref[s] = 0
        return c

    def put(a, c):
        slot_ref[dest_ref[a]] = lax.shift_right_logical(a, jnp.int32(1))
        return c

    lax.fori_loop(0, slot_ref.shape[0], clear, 0, unroll=8)
    lax.fori_loop(0, dest_ref.shape[0], put, 0, unroll=8)


def _slotmap(dest, n_slots):
    return pl.pallas_call(
        _slotmap_kernel,
        in_specs=[pl.BlockSpec(memory_space=pltpu.SMEM)],
        out_specs=pl.BlockSpec(memory_space=pltpu.SMEM),
        out_shape=jax.ShapeDtypeStruct((n_slots,), jnp.int32),
        name="moe_slot_map",
    )(dest)


def _expert_kernel(slot_ref, be_ref, na_ref, x_hbm, w1_ref, w3_ref, w2_ref, yb_ref, buf, sem, w1s, w3s, w2s, *, blk):
    i = pl.program_id(0)
    na = na_ref[0]

    def row_copy(par, src_row, r):
        return pltpu.make_async_copy(x_hbm.at[pl.ds(src_row, 1)], buf.at[par, pl.ds(r, 1)], sem.at[par])

    def gather(b):
        def body(r, c):
            row_copy(b % 2, slot_ref[b * blk + r], r).start()
            return c

        lax.fori_loop(0, blk, body, 0, unroll=8)

    @pl.when(i == 0)
    def _():
        gather(i)

    @pl.when(i + 1 < na)
    def _():
        gather(i + 1)

    @pl.when(i < na)
    def _():
        def body(r, c):
            row_copy(i % 2, 0, r).wait()
            return c

        lax.fori_loop(0, blk, body, 0, unroll=8)

        @pl.when((i == 0) | (be_ref[i] != be_ref[jnp.maximum(i - 1, 0)]))
        def _():
            w1s[...] = w1_ref[0, 0].astype(BF16)
            w3s[...] = w3_ref[0, 0].astype(BF16)
            w2s[...] = w2_ref[0, 0].astype(BF16)

        x = buf[i % 2].astype(BF16)
        h1 = jnp.dot(x, w1s[...], preferred_element_type=F32)
        h3 = jnp.dot(x, w3s[...], preferred_element_type=F32)
        yb_ref[...] = jnp.dot((h1 * _sigmoid(h1) * h3).astype(BF16), w2s[...], preferred_element_type=F32)

    @pl.when(i >= na)
    def _():
        yb_ref[...] = jnp.zeros_like(yb_ref)


def _experts(slot_tok, blk_e, n_act, x, w1, w3, w2, l, blk):
    n_slots = slot_tok.shape[0]
    wspec = lambda a: pl.BlockSpec((1, 1) + a.shape[2:], lambda i, st, be, na: (l, be[jnp.minimum(i, na[0] - 1)], 0, 0))
    return pl.pallas_call(
        functools.partial(_expert_kernel, blk=blk),
        grid_spec=pltpu.PrefetchScalarGridSpec(
            num_scalar_prefetch=3,
            grid=(n_slots // blk,),
            in_specs=[pl.BlockSpec(memory_space=pl.ANY), wspec(w1), wspec(w3), wspec(w2)],
            out_specs=pl.BlockSpec((blk, D_MODEL), lambda i, st, be, na: (i, 0)),
            scratch_shapes=[pltpu.VMEM((2, blk, D_MODEL), F32), pltpu.SemaphoreType.DMA((2,)),
                            pltpu.VMEM((D_MODEL, EXPERT_FF), BF16), pltpu.VMEM((D_MODEL, EXPERT_FF), BF16),
                            pltpu.VMEM((EXPERT_FF, D_MODEL), BF16)],
        ),
        out_shape=jax.ShapeDtypeStruct((n_slots, D_MODEL), F32),
        compiler_params=_cparams("arbitrary"),
        name="moe_experts",
    )(slot_tok, blk_e, n_act, x, w1, w3, w2)


def _combine_kernel(dest_ref, x_ref, route_ref, yb_hbm, ln_ref, o_ref, buf, sem, *, tm):
    base = pl.program_id(0) * tm

    def copy(i, kslot):
        return pltpu.make_async_copy(yb_hbm.at[pl.ds(dest_ref[2 * (base + i) + kslot], 1)], buf.at[kslot, pl.ds(i, 1)], sem)

    def start(i, c):
        copy(i, 0).start()
        copy(i, 1).start()
        return c

    def wait(i, c):
        copy(i, 0).wait()
        copy(i, 1).wait()
        return c

    lax.fori_loop(0, tm, start, 0)
    lax.fori_loop(0, tm, wait, 0)
    y = route_ref[:, 2:3] * buf[0] + route_ref[:, 3:4] * buf[1]
    o_ref[...] = _layer_norm(ALPHA * x_ref[...] + y, ln_ref[0:1, :], ln_ref[1:2, :])


def _combine(dest, x, route, yb, ln, tm):
    n = x.shape[0]
    return pl.pallas_call(
        functools.partial(_combine_kernel, tm=tm),
        grid_spec=pltpu.PrefetchScalarGridSpec(
            num_scalar_prefetch=1,
            grid=(n // tm,),
            in_specs=[pl.BlockSpec((tm, D_MODEL), lambda i, d: (i, 0)),
                      pl.BlockSpec((tm, LANE), lambda i, d: (i, 0)),
                      pl.BlockSpec(memory_space=pl.ANY),
                      pl.BlockSpec(ln.shape, lambda i, d: (0, 0))],
            out_specs=pl.BlockSpec((tm, D_MODEL), lambda i, d: (i, 0)),
            scratch_shapes=[pltpu.VMEM((2, tm, D_MODEL), F32), pltpu.SemaphoreType.DMA(())],
        ),
        out_shape=jax.ShapeDtypeStruct((n, D_MODEL), F32),
        compiler_params=_cparams("arbitrary"),
        name="moe_combine_ln",
    )(dest, x, route, yb, ln)


def _hier_moe_ln(x, wts, blk):
    n = x.shape[0]
    route, cnt = _router(x, wts["wr"], wts["br"], _tile(n, 512))
    counts = cnt[0, ROUTE_OFF:ROUTE_OFF + N_EXPERTS].astype(jnp.int32)
    pcounts = (counts + blk - 1) // blk * blk
    pend = jnp.cumsum(pcounts)
    pstart = pend - pcounts
    eidx = route[:, 0:2].astype(jnp.int32)
    dest = (pstart[eidx] + route[:, 4:6].astype(jnp.int32)).reshape(-1)
    n_slots = -(-(2 * n + N_EXPERTS * (blk - 1)) // blk) * blk
    n_blk = n_slots // blk
    blk_start = jnp.arange(n_blk, dtype=jnp.int32) * blk
    blk_e = jnp.minimum(jnp.sum(pend[None, :] <= blk_start[:, None], axis=1), N_EXPERTS - 1).astype(jnp.int32)
    n_act = (pend[-1:] // blk).astype(jnp.int32)
    slot_tok = _slotmap(dest, n_slots)
    yb = _experts(slot_tok, blk_e, n_act, x, wts["w1"], wts["w3"], wts["w2"], wts["l"], blk)
    return _combine(dest, x, route, yb, wts["ln3"], _tile(n, 256))


def _layer(x, mk, mv, conv_st, shift_st, wkv, wts):
    b, t, _ = x.shape
    n = b * t
    x2 = x.reshape(n, D_MODEL)
    p = _proj(x2, wts["w_in"], _tile(n, 512), D_IN_PAD // 3).reshape(b, t, D_IN_PAD)
    shift_pad = jnp.pad(shift_st, ((0, 0), (0, 0), (0, D_SHIFT_PAD - D_SHIFT)))
    C = min(CHUNK, t)
    consts = _rwkv_consts(C)
    yc, r, k, v, w, kk, bb, g, nconv, nshift = _prep(p, conv_st, shift_pad, wts["pvec"], wts["mu"], wts["wl"], consts[3], _tile(t, 256))
    yr, nwkv = _rwkv(r, w, k, v, kk, bb, g, wkv, wts["hvec"], consts, _tile(t, 256), C)
    tm = _tile(n, 256)
    x1 = _out_ln(x2, yc.reshape(n, D_CONV), yr.reshape(n, D_RWKV), wts["w_out"], wts["ln1"], tm)
    xa = _attn(x1.reshape(b, t, D_MODEL), mk, mv, wts["xa_q"], wts["xa_o"], wts["ln2"], _tile(t, 256))
    y = _hier_moe_ln(xa.reshape(n, D_MODEL), wts, 256 if n >= 8192 else 128)
    return y.reshape(b, t, D_MODEL), nconv, nshift[:, :, :D_SHIFT], nwkv


def _ln_rows(g, b):
    return jnp.zeros((8, D_MODEL), F32).at[0].set(g).at[1].set(b)


def kernel(x_prompt, x_sample, mem_prompt, cache_mem_k, cache_mem_v, state_conv, state_shift, state_wkv, w_in, conv_w, shift_mu, decay_w0, decay_w2, iclr_a0, iclr_a2, gate_g2, k_k, k_a, r_k, lnx_g, lnx_b, w_out, ln1_g, ln1_b, xa_q, xa_k, xa_v, xa_o, ln2_g, ln2_b, router_grp, router_grp_b, router_sub, router_sub_b, moe_w1, moe_w3, moe_w2, ln3_g, ln3_b):
    depth = w_in.shape[0]
    bp = x_prompt.shape[0]
    bs = x_sample.shape[0]
    yp, ys = x_prompt, x_sample
    outs = [[] for _ in range(8)]
    for l in range(depth):
        wl = jnp.zeros((D_LORA_PAD, 3 * D_RWKV), F32)
        wl = wl.at[0:DECAY_LORA, 0:D_RWKV].set(decay_w2[l])
        wl = wl.at[DECAY_LORA:DECAY_LORA + A_LORA, D_RWKV:2 * D_RWKV].set(iclr_a2[l])
        wl = wl.at[DECAY_LORA + A_LORA:D_LORA, 2 * D_RWKV:].set(gate_g2[l])
        wr = jnp.zeros((D_MODEL, LANE), F32)
        wr = wr.at[:, 0:N_GROUPS].set(router_grp[l])
        wr = wr.at[:, ROUTE_OFF:ROUTE_OFF + N_EXPERTS].set(jnp.transpose(router_sub[l], (1, 0, 2)).reshape(D_MODEL, N_EXPERTS))
        br = jnp.zeros((1, LANE), F32)
        br = br.at[0, 0:N_GROUPS].set(router_grp_b[l]).at[0, ROUTE_OFF:ROUTE_OFF + N_EXPERTS].set(router_sub_b[l].reshape(-1))
        wts = dict(
            w_in=jnp.pad(w_in[l], ((0, 0), (0, D_IN_PAD - D_IN))).astype(BF16),
            pvec=jnp.zeros((8, D_RWKV), F32).at[0].set(decay_w0[l]).at[1].set(iclr_a0[l]).at[2].set(k_k[l]).at[3].set(k_a[l]).at[4:7].set(conv_w[l]),
            mu=jnp.pad(shift_mu[l], (0, D_SHIFT_PAD - D_SHIFT)).reshape(1, D_SHIFT_PAD),
            wl=wl.astype(BF16),
            hvec=jnp.zeros((8, D_RWKV), F32).at[0].set(r_k[l].reshape(-1)).at[1].set(lnx_g[l]).at[2].set(lnx_b[l]),
            w_out=w_out[l].astype(BF16), ln1=_ln_rows(ln1_g[l], ln1_b[l]),
            xa_q=xa_q[l].astype(BF16), xa_o=xa_o[l].astype(BF16), ln2=_ln_rows(ln2_g[l], ln2_b[l]),
            wr=wr.astype(BF16), br=br,
            w1=moe_w1, w3=moe_w3, w2=moe_w2, l=l, ln3=_ln_rows(ln3_g[l], ln3_b[l]),
        )
        mem2 = mem_prompt.reshape(bp * N_MEM, D_MODEL)
        mk = _proj(mem2, xa_k[l].astype(BF16), _tile(bp * N_MEM, 256), 1024).reshape(bp, N_MEM, D_MODEL)
        mv = _proj(mem2, xa_v[l].astype(BF16), _tile(bp * N_MEM, 256), 1024).reshape(bp, N_MEM, D_MODEL)
        yp, c_p, s_p, w_p = _layer(yp, mk, mv, jnp.zeros((bp, 2, D_CONV), F32), jnp.zeros((bp, 1, D_SHIFT), F32),
                                   jnp.zeros((bp, RWKV_HEADS, RWKV_HEAD, RWKV_HEAD), F32), wts)
        ys, c_s, s_s, w_s = _layer(ys, cache_mem_k[l].reshape(bs, N_MEM, D_MODEL), cache_mem_v[l].reshape(bs, N_MEM, D_MODEL),
                                   state_conv[l], state_shift[l], state_wkv[l], wts)
        for o, val in zip(outs, (mk.reshape(bp, N_MEM, XA_HEADS, XA_HEAD), mv.reshape(bp, N_MEM, XA_HEADS, XA_HEAD),
                                 c_p, s_p, w_p, c_s, s_s, w_s)):
            o.append(val)
    return (yp, ys) + tuple(jnp.stack(o) for o in outs)
```

```python
import functools

import jax
import jax.numpy as jnp
from jax import lax
from jax.experimental import pallas as pl
from jax.experimental.pallas import tpu as pltpu

F32 = jnp.float32
BF16 = jnp.bfloat16

D_MODEL = 2048
D_CONV = 1024
D_RWKV = 1024
RWKV_HEAD = 64
RWKV_HEADS = 16
DECAY_LORA = 64
A_LORA = 64
GATE_LORA = 160
D_LORA = DECAY_LORA + A_LORA + GATE_LORA
D_SHIFT = 3 * D_RWKV + D_LORA
D_IN = 3 * D_CONV + D_SHIFT
LANE = 128
D_LORA_PAD = 384
D_SHIFT_PAD = 3 * D_RWKV + D_LORA_PAD
D_IN_PAD = 3 * D_CONV + D_SHIFT_PAD
N_MEM = 256
XA_HEADS = 4
XA_HEAD = 512
N_GROUPS = 4
EXP_PER_GROUP = 8
N_EXPERTS = 32
EXPERT_FF = 512
LN_EPS = 1e-5
GN_EPS = 64e-5
ALPHA = 2.0 ** 0.25
CHUNK = 64
ROUTE_OFF = N_GROUPS
VMEM_LIMIT = 56 * 1024 * 1024


def _cparams(*sem):
    return pltpu.CompilerParams(dimension_semantics=sem, vmem_limit_bytes=VMEM_LIMIT)


def _dot(a, b):
    return jnp.dot(a.astype(BF16), b.astype(BF16), preferred_element_type=F32)


def _dot_nt(a, b):
    return lax.dot_general(a.astype(BF16), b.astype(BF16), (((1,), (1,)), ((), ())), preferred_element_type=F32)


def _dot_tn(a, b):
    return lax.dot_general(a.astype(BF16), b.astype(BF16), (((0,), (0,)), ((), ())), preferred_element_type=F32)


def _split3(x):
    h1 = x.astype(BF16)
    r1 = x - h1.astype(F32)
    h2 = r1.astype(BF16)
    h3 = (r1 - h2.astype(F32)).astype(BF16)
    return h1, h2, h3


def _dot_exact_rhs(x, m3):
    return jnp.dot(jnp.concatenate(_split3(x), axis=1), m3, preferred_element_type=F32)


def _layer_norm(z, g, b):
    mu = jnp.mean(z, axis=-1, keepdims=True)
    d = z - mu
    var = jnp.mean(d * d, axis=-1, keepdims=True)
    return d * lax.rsqrt(var + LN_EPS) * g + b


def _sigmoid(x):
    return 1.0 / (1.0 + jnp.exp(-x))


def _tile(n, pref):
    t = min(n, pref)
    while t > 8 and (n % t or t % 8):
        t -= 8
    return t if n % t == 0 else n


def _proj_kernel(x_ref, w_ref, o_ref):
    o_ref[...] = _dot(x_ref[...], w_ref[...])


def _proj(x, w, tm, tn):
    n, k = x.shape
    m = w.shape[1]
    return pl.pallas_call(
        _proj_kernel,
        grid=(m // tn, n // tm),
        in_specs=[pl.BlockSpec((tm, k), lambda j, i: (i, 0)), pl.BlockSpec((k, tn), lambda j, i: (0, j))],
        out_specs=pl.BlockSpec((tm, tn), lambda j, i: (i, j)),
        out_shape=jax.ShapeDtypeStruct((n, m), F32),
        compiler_params=_cparams("arbitrary", "arbitrary"),
        name="proj",
    )(x, w)


def _shift_rows(x, heads):
    s = len(heads)
    out = pltpu.roll(x, s, axis=0)
    row = lax.broadcasted_iota(jnp.int32, x.shape, 0)
    for i, h in enumerate(heads):
        out = jnp.where(row == i, h, out)
    return out


def _segsum64(x, e):
    return jnp.concatenate([_dot_exact_rhs(x[:, j * LANE:(j + 1) * LANE], e) for j in range(x.shape[1] // LANE)], axis=1)


def _prep_kernel(p_ref, cst_ref, sst_ref, pv_ref, mu_ref, wl_ref, e_ref,
                 yc_ref, r_ref, k_ref, v_ref, w_ref, kk_ref, bb_ref, g_ref, nconv_ref, nshift_ref,
                 cu, cp):
    tt = p_ref.shape[1]

    @pl.when(pl.program_id(1) == 0)
    def _():
        cu[0:2, :] = cst_ref[0]
        cp[0:1, :] = sst_ref[0]

    bg = p_ref[0, :, 0:D_CONV]
    u = p_ref[0, :, D_CONV:2 * D_CONV] * p_ref[0, :, 2 * D_CONV:3 * D_CONV]
    u1 = _shift_rows(u, [cu[1:2, :]])
    u2 = _shift_rows(u, [cu[0:1, :], cu[1:2, :]])
    yc_ref[0] = bg * (pv_ref[4:5, :] * u2 + pv_ref[5:6, :] * u1 + pv_ref[6:7, :] * u)
    tail = u[tt - 2:tt, :]
    cu[0:2, :] = tail
    nconv_ref[0] = tail

    prw = p_ref[0, :, 3 * D_CONV:]
    prev = _shift_rows(prw, [cp[0:1, :]])
    q = prw + mu_ref[...] * (prev - prw)
    last = prw[tt - 1:tt, :]
    cp[0:1, :] = last
    nshift_ref[0] = last

    r = q[:, 0:D_RWKV]
    k = q[:, D_RWKV:2 * D_RWKV]
    v = q[:, 2 * D_RWKV:3 * D_RWKV]
    ql = q[:, 3 * D_RWKV:]
    lane = lax.broadcasted_iota(jnp.int32, ql.shape, 1)
    act = jnp.where(lane < DECAY_LORA, jnp.tanh(ql), jnp.where(lane < DECAY_LORA + A_LORA, ql, _sigmoid(ql)))
    lo = _dot(act, wl_ref[...])
    z = -(pv_ref[0:1, :] + lo[:, 0:D_RWKV])
    w_log = -(jnp.maximum(z, 0.0) + jnp.log(1.0 + jnp.exp(-jnp.abs(z)))) - 0.5
    a = _sigmoid(pv_ref[1:2, :] + lo[:, D_RWKV:2 * D_RWKV])
    kk = k * pv_ref[2:3, :]
    kk = kk * lax.rsqrt(jnp.maximum(_segsum64(kk * kk, e_ref[...]), 1e-24))
    r_ref[0] = r
    k_ref[0] = k * (1.0 + (a - 1.0) * pv_ref[3:4, :])
    v_ref[0] = v
    w_ref[0] = -jnp.exp(w_log)
    kk_ref[0] = kk
    bb_ref[0] = kk * a
    g_ref[0] = lo[:, 2 * D_RWKV:]


def _prep(p, conv_st, shift_st, pvec, mu, wl, e, tt):
    b, t, _ = p.shape
    row = lambda w: pl.BlockSpec((1, tt, w), lambda i, j: (i, j, 0))
    full = lambda a: pl.BlockSpec(a.shape, lambda i, j: (0,) * a.ndim)
    st = lambda a: pl.BlockSpec((1,) + a.shape[1:], lambda i, j: (i, 0, 0))
    seq = jax.ShapeDtypeStruct((b, t, D_RWKV), F32)
    return pl.pallas_call(
        _prep_kernel,
        grid=(b, t // tt),
        in_specs=[row(D_IN_PAD), st(conv_st), st(shift_st), full(pvec), full(mu), full(wl), full(e)],
        out_specs=[row(D_RWKV)] * 8 + [st(conv_st), st(shift_st)],
        out_shape=[seq] * 8 + [jax.ShapeDtypeStruct(conv_st.shape, F32), jax.ShapeDtypeStruct(shift_st.shape, F32)],
        scratch_shapes=[pltpu.VMEM((8, D_CONV), F32), pltpu.VMEM((8, D_SHIFT_PAD), F32)],
        compiler_params=_cparams("arbitrary", "arbitrary"),
        name="mixer_prep",
    )(p, conv_st, shift_st, pvec, mu, wl, e)


N_PAIRS = RWKV_HEADS // 2


def _rwkv_kernel(r_ref, w_ref, k_ref, v_ref, kk_ref, bb_ref, g_ref, s0_ref, hv_ref, ms_ref, mi_ref, lt_ref, e_ref,
                 y_ref, sout_ref, S, *, C):
    t = pl.program_id(1)
    nt = pl.num_programs(1)
    H = RWKV_HEAD
    lane = lax.broadcasted_iota(jnp.int32, (1, LANE), 1)
    m0 = lane < H

    @pl.when(t == 0)
    def _():
        z = jnp.zeros((H, H), F32)
        for hp in range(N_PAIRS):
            S[hp, 0:H, :] = jnp.concatenate([s0_ref[0, 2 * hp], z], axis=1)
            S[hp, H:2 * H, :] = jnp.concatenate([z, s0_ref[0, 2 * hp + 1]], axis=1)

    pair = lambda x: jnp.concatenate([jnp.where(m0, x, 0.0), jnp.where(m0, 0.0, x)], axis=0)

    def pair_chunk(hp, sl):
        ln = slice(hp * LANE, (hp + 1) * LANE)
        rk = hv_ref[0:1, ln]
        lng = hv_ref[1:2, ln]
        lnb = hv_ref[2:3, ln]
        r = r_ref[0, sl, ln]
        w = w_ref[0, sl, ln]
        k = k_ref[0, sl, ln]
        v = v_ref[0, sl, ln]
        kk = kk_ref[0, sl, ln]
        bb = bb_ref[0, sl, ln]
        cw = jnp.dot(lt_ref[...], jnp.concatenate(_split3(w), axis=0), preferred_element_type=F32)
        yield
        p = jnp.exp(cw)
        ip = jnp.exp(-cw)
        pc = p[C - 1:C, :]
        la = pair(-(jnp.exp(cw - w) * kk))
        lr = pair(p * r)
        rb = pair(bb * ip)
        rkk = pair(k * ip)
        vst = pair(v)
        lhs = jnp.concatenate([la, lr], axis=0)
        rhs = jnp.concatenate([rb, rkk], axis=0)
        gram = _dot_nt(lhs, rhs)
        sm = S[hp]
        ws = _dot_nt(lhs, sm)
        yield
        ms = ms_ref[...] > 0.5
        mi = mi_ref[...] > 0.5
        a_ab = jnp.where(ms, gram[:2 * C, :2 * C], 0.0)
        a_ak = jnp.where(ms, gram[:2 * C, 2 * C:], 0.0)
        a_rb = jnp.where(mi, gram[2 * C:, :2 * C], 0.0)
        a_rk = jnp.where(mi, gram[2 * C:, 2 * C:], 0.0)
        x = ws[:2 * C] + _dot(a_ak, vst)
        tm = (mi_ref[...] - ms_ref[...]) + a_ab
        pw = a_ab
        n = 1
        while 2 * n < C:
            pw = _dot(pw, pw)
            yield
            tm = tm + _dot(pw, tm)
            yield
            n *= 2
        u = _dot(tm, x)
        yield
        uv = jnp.concatenate([u, vst], axis=0)
        ost = ws[2 * C:] + _dot(jnp.concatenate([a_rb, a_rk], axis=1), uv)
        o = ost[:C] + ost[C:]
        S[hp] = sm * pc + _dot_tn(uv, rhs * pc)
        yield
        e3 = e_ref[...]
        sums = _dot_exact_rhs(jnp.concatenate([o, r * k * rk], axis=0), e3)
        yield
        d = o - sums[:C] * (1.0 / H)
        var = _dot_exact_rhs(d * d, e3) * (1.0 / H)
        yield
        on = d * lax.rsqrt(var + GN_EPS) * lng + lnb
        y_ref[0, sl, ln] = (on + sums[C:] * v) * g_ref[0, sl, ln]

    def chunk(c, carry):
        sl = pl.ds(pl.multiple_of(c * C, C), C)
        live = [pair_chunk(hp, sl) for hp in range(N_PAIRS)]
        while live:
            live = [g for g in live if next(g, live) is not live]
        return carry

    lax.fori_loop(0, r_ref.shape[1] // C, chunk, 0)

    @pl.when(t == nt - 1)
    def _():
        for hp in range(N_PAIRS):
            sout_ref[0, 2 * hp] = S[hp, 0:H, 0:H]
            sout_ref[0, 2 * hp + 1] = S[hp, H:2 * H, H:2 * H]


def _rwkv(r, w, k, v, kk, bb, g, s0, hvec, consts, tc, C):
    b, t, _ = r.shape
    ms, mi, lt, e3 = consts
    seq = pl.BlockSpec((1, tc, D_RWKV), lambda i, j: (i, j, 0))
    st = pl.BlockSpec((1, RWKV_HEADS, RWKV_HEAD, RWKV_HEAD), lambda i, j: (i, 0, 0, 0))
    full = lambda a: pl.BlockSpec(a.shape, lambda i, j: (0,) * a.ndim)
    return pl.pallas_call(
        functools.partial(_rwkv_kernel, C=C),
        grid=(b, t // tc),
        in_specs=[seq] * 7 + [st, full(hvec), full(ms), full(mi), full(lt), full(e3)],
        out_specs=[seq, st],
        out_shape=[jax.ShapeDtypeStruct((b, t, D_RWKV), F32), jax.ShapeDtypeStruct(s0.shape, F32)],
        scratch_shapes=[pltpu.VMEM((N_PAIRS, LANE, LANE), F32)],
        compiler_params=_cparams("arbitrary", "arbitrary"),
        name="rwkv7",
    )(r, w, k, v, kk, bb, g, s0, hvec, ms, mi, lt, e3)


def _rwkv_consts(C):
    i = jnp.arange(2 * C)
    same = (i[:, None] // C) == (i[None, :] // C)
    ms = (same & ((i[None, :] % C) < (i[:, None] % C))).astype(F32)
    mi = (same & ((i[None, :] % C) <= (i[:, None] % C))).astype(F32)
    j = jnp.arange(C)
    lt = jnp.tile((j[None, :] <= j[:, None]).astype(BF16), (1, 3))
    l = jnp.arange(LANE)
    e3 = jnp.tile(((l[:, None] // RWKV_HEAD) == (l[None, :] // RWKV_HEAD)).astype(BF16), (3, 1))
    return ms, mi, lt, e3


def _out_ln_kernel(x_ref, yc_ref, yr_ref, w_ref, ln_ref, o_ref):
    z = ALPHA * x_ref[...] + _dot(yc_ref[...], w_ref[0:D_CONV, :]) + _dot(yr_ref[...], w_ref[D_CONV:, :])
    o_ref[...] = _layer_norm(z, ln_ref[0:1, :], ln_ref[1:2, :])


def _out_ln(x, yc, yr, w, ln, tm):
    n = x.shape[0]
    row = lambda wd: pl.BlockSpec((tm, wd), lambda i: (i, 0))
    full = lambda a: pl.BlockSpec(a.shape, lambda i: (0,) * a.ndim)
    return pl.pallas_call(
        _out_ln_kernel,
        grid=(n // tm,),
        in_specs=[row(D_MODEL), row(D_CONV), row(D_RWKV), full(w), full(ln)],
        out_specs=row(D_MODEL),
        out_shape=jax.ShapeDtypeStruct((n, D_MODEL), F32),
        compiler_params=_cparams("arbitrary"),
        name="out_proj_ln",
    )(x, yc, yr, w, ln)


def _attn_kernel(x_ref, mk_ref, mv_ref, wq_ref, wo_ref, ln_ref, o_ref):
    x = x_ref[0]
    q = _dot(x, wq_ref[...])
    heads = []
    for h in range(XA_HEADS):
        sl = slice(h * XA_HEAD, (h + 1) * XA_HEAD)
        s = _dot_nt(q[:, sl], mk_ref[0, :, sl]) * (XA_HEAD ** -0.5)
        s = s - jnp.max(s, axis=-1, keepdims=True)
        pe = jnp.exp(s)
        pr = pe / jnp.sum(pe, axis=-1, keepdims=True)
        heads.append(_dot(pr, mv_ref[0, :, sl]))
    o = jnp.concatenate(heads, axis=1)
    z = ALPHA * x + _dot(o, wo_ref[...])
    o_ref[0] = _layer_norm(z, ln_ref[0:1, :], ln_ref[1:2, :])


def _attn(x, mk, mv, wq, wo, ln, tq):
    b, t, _ = x.shape
    row = pl.BlockSpec((1, tq, D_MODEL), lambda i, j: (i, j, 0))
    mem = pl.BlockSpec((1, N_MEM, D_MODEL), lambda i, j: (i, 0, 0))
    full = lambda a: pl.BlockSpec(a.shape, lambda i, j: (0,) * a.ndim)
    return pl.pallas_call(
        _attn_kernel,
        grid=(b, t // tq),
        in_specs=[row, mem, mem, full(wq), full(wo), full(ln)],
        out_specs=row,
        out_shape=jax.ShapeDtypeStruct(x.shape, F32),
        compiler_params=_cparams("arbitrary", "arbitrary"),
        name="mem_attn_ln",
    )(x, mk, mv, wq, wo, ln)


def _router_kernel(x_ref, wr_ref, br_ref, lt_ref, route_ref, cnt_ref, cnt):
    i = pl.program_id(0)

    @pl.when(i == 0)
    def _():
        cnt[...] = jnp.zeros_like(cnt)

    neg = -1e30
    big = 1 << 20
    logits = _dot(x_ref[...], wr_ref[...]) + br_ref[...]
    lane = lax.broadcasted_iota(jnp.int32, logits.shape, 1)
    first_arg = lambda val, m: jnp.min(jnp.where(val == m, lane, big), axis=-1, keepdims=True)
    lg = jnp.where(lane < N_GROUPS, logits, neg)
    mg = jnp.max(lg, axis=-1, keepdims=True)
    grp = first_arg(lg, mg)
    p_grp = 1.0 / jnp.sum(jnp.exp(lg - mg), axis=-1, keepdims=True)
    lo = ROUTE_OFF + EXP_PER_GROUP * grp
    ls = jnp.where((lane >= lo) & (lane < lo + EXP_PER_GROUP), logits, neg)
    m1 = jnp.max(ls, axis=-1, keepdims=True)
    j1 = first_arg(ls, m1)
    ls2 = jnp.where(lane == j1, neg, ls)
    m2 = jnp.max(ls2, axis=-1, keepdims=True)
    j2 = first_arg(ls2, m2)
    t2 = jnp.exp(m2 - m1)
    g1 = p_grp * (1.0 / (1.0 + t2))
    g2 = p_grp * (t2 / (1.0 + t2))
    oh1 = lane == j1
    oh2 = lane == j2
    oh = jnp.where(oh1 | oh2, 1.0, 0.0)
    before = jnp.dot(lt_ref[...], oh.astype(BF16), preferred_element_type=F32) + cnt[0:1, :]
    rank1 = jnp.sum(jnp.where(oh1, before, 0.0), axis=-1, keepdims=True)
    rank2 = jnp.sum(jnp.where(oh2, before, 0.0), axis=-1, keepdims=True)
    total = cnt[0:1, :] + jnp.sum(oh, axis=0, keepdims=True)
    cnt[0:1, :] = total
    cnt_ref[...] = jnp.broadcast_to(total, cnt_ref.shape)
    e1 = (j1 - ROUTE_OFF).astype(F32)
    e2 = (j2 - ROUTE_OFF).astype(F32)
    out = jnp.zeros(logits.shape, F32)
    for col, val in enumerate((e1, e2, g1, g2, rank1, rank2)):
        out = jnp.where(lane == col, val, out)
    route_ref[...] = out


def _router(x, wr, br, tm):
    n = x.shape[0]
    j = jnp.arange(tm)
    lt = (j[None, :] < j[:, None]).astype(BF16)
    full = lambda a: pl.BlockSpec(a.shape, lambda i: (0,) * a.ndim)
    return pl.pallas_call(
        _router_kernel,
        grid=(n // tm,),
        in_specs=[pl.BlockSpec((tm, D_MODEL), lambda i: (i, 0)), full(wr), full(br), full(lt)],
        out_specs=[pl.BlockSpec((tm, LANE), lambda i: (i, 0)), pl.BlockSpec((8, LANE), lambda i: (0, 0))],
        out_shape=[jax.ShapeDtypeStruct((n, LANE), F32), jax.ShapeDtypeStruct((8, LANE), F32)],
        scratch_shapes=[pltpu.VMEM((8, LANE), F32)],
        compiler_params=_cparams("arbitrary"),
        name="router",
    )(x, wr, br, lt)


def _slotmap_kernel(dest_ref, lo_ref, hi_ref, src_ref, dst_ref, *, n_tok, blk):
    n_slots = src_ref.shape[0]

    def spare(lo, hi):
        def body(s, c):
            src_ref[s] = 0
            dst_ref[blk + s] = 2 * n_tok + s
            return c

        lax.fori_loop(lo, hi, body, 0)

    def lead(s, c):
        dst_ref[s] = 2 * n_tok + n_slots + s
        return c

    lax.fori_loop(0, blk, lead, 0)
    for e in range(N_EXPERTS):
        spare(lo_ref[e], hi_ref[e])
    spare(hi_ref[N_EXPERTS - 1], n_slots)

    def put(a, c):
        s = dest_ref[a]
        tok = lax.shift_right_logical(a, jnp.int32(1))
        src_ref[s] = tok
        dst_ref[blk + s] = (a & 1) * n_tok + tok
        return c

    lax.fori_loop(0, 2 * n_tok, put, 0, unroll=8)


def _slotmap(dest, pad_lo, pad_hi, n_tok, n_slots, blk):
    smem = pl.BlockSpec(memory_space=pltpu.SMEM)
    return pl.pallas_call(
        functools.partial(_slotmap_kernel, n_tok=n_tok, blk=blk),
        in_specs=[smem, smem, smem],
        out_specs=[smem, smem],
        out_shape=[jax.ShapeDtypeStruct((n_slots,), jnp.int32), jax.ShapeDtypeStruct((n_slots + blk,), jnp.int32)],
        name="moe_slot_map",
    )(dest, pad_lo, pad_hi)


def _expert_kernel(src_ref, dst_ref, be_ref, na_ref, x_hbm, w1_ref, w3_ref, w2_ref, out_hbm,
                   xbuf, ybuf, gsem, ssem, w1s, w3s, w2s, *, blk):
    i = pl.program_id(0)
    na = na_ref[0]
    par = i % 2
    oth = 1 - par
    rows = range(blk)

    def g_copy(p, src_row, r):
        return pltpu.make_async_copy(x_hbm.at[pl.ds(src_row, 1)], xbuf.at[p, pl.ds(r, 1)], gsem.at[p])

    def s_copy(p, r, dst_row):
        return pltpu.make_async_copy(ybuf.at[p, pl.ds(r, 1)], out_hbm.at[pl.ds(dst_row, 1)], ssem.at[p])

    @pl.when(i == 0)
    def _():
        for r in rows:
            g_copy(0, src_ref[r], r).start(priority=1)
        ybuf[1] = jnp.zeros((blk, D_MODEL), F32)

    @pl.when(i < na)
    def _():
        for r in rows:
            g_copy(par, 0, r).wait()

        @pl.when(i >= 1)
        def _():
            for r in rows:
                s_copy(par, r, 0).wait()

        @pl.when((i == 0) | (be_ref[i] != be_ref[jnp.maximum(i - 1, 0)]))
        def _():
            w1s[...] = w1_ref[0, 0].astype(BF16)
            w3s[...] = w3_ref[0, 0].astype(BF16)
            w2s[...] = w2_ref[0, 0].astype(BF16)

        x = xbuf[par].astype(BF16)
        nxt = jnp.minimum(i + 1, na - 1) * blk
        for r in rows:
            g_copy(oth, src_ref[nxt + r], r).start(priority=1)
            s_copy(oth, r, dst_ref[i * blk + r]).start(priority=1)
        h1 = jnp.dot(x, w1s[...], preferred_element_type=F32)
        h3 = jnp.dot(x, w3s[...], preferred_element_type=F32)
        ybuf[par] = jnp.dot((h1 * _sigmoid(h1) * h3).astype(BF16), w2s[...], preferred_element_type=F32)

        @pl.when(i == na - 1)
        def _():
            for r in rows:
                s_copy(par, r, dst_ref[(i + 1) * blk + r]).start(priority=1)
            for r in rows:
                g_copy(oth, 0, r).wait()
                s_copy(oth, r, 0).wait()
                s_copy(par, r, 0).wait()


def _experts(slot_src, slot_dst, blk_e, n_act, x, w1, w3, w2, l, blk):
    n = x.shape[0]
    n_slots = slot_src.shape[0]
    wspec = lambda a: pl.BlockSpec((1, 1) + a.shape[2:], lambda i, s, d, be, na: (l, be[jnp.minimum(i, na[0] - 1)], 0, 0))
    return pl.pallas_call(
        functools.partial(_expert_kernel, blk=blk),
        grid_spec=pltpu.PrefetchScalarGridSpec(
            num_scalar_prefetch=4,
            grid=(n_slots // blk,),
            in_specs=[pl.BlockSpec(memory_space=pl.ANY), wspec(w1), wspec(w3), wspec(w2)],
            out_specs=pl.BlockSpec(memory_space=pl.ANY),
            scratch_shapes=[pltpu.VMEM((2, blk, D_MODEL), F32), pltpu.VMEM((2, blk, D_MODEL), F32),
                            pltpu.SemaphoreType.DMA((2,)), pltpu.SemaphoreType.DMA((2,)),
                            pltpu.VMEM((D_MODEL, EXPERT_FF), BF16), pltpu.VMEM((D_MODEL, EXPERT_FF), BF16),
                            pltpu.VMEM((EXPERT_FF, D_MODEL), BF16)],
        ),
        out_shape=jax.ShapeDtypeStruct((2 * n + n_slots + blk, D_MODEL), F32),
        compiler_params=_cparams("arbitrary"),
        name="moe_experts",
    )(slot_src, slot_dst, blk_e, n_act, x, w1, w3, w2)


def _combine_kernel(x_ref, route_ref, y0_ref, y1_ref, ln_ref, o_ref):
    y = route_ref[:, 2:3] * y0_ref[...] + route_ref[:, 3:4] * y1_ref[...]
    o_ref[...] = _layer_norm(ALPHA * x_ref[...] + y, ln_ref[0:1, :], ln_ref[1:2, :])


def _combine(x, route, ye, ln, tm):
    n = x.shape[0]
    row = lambda off: pl.BlockSpec((tm, D_MODEL), lambda i: (i + off, 0))
    return pl.pallas_call(
        _combine_kernel,
        grid=(n // tm,),
        in_specs=[row(0), pl.BlockSpec((tm, LANE), lambda i: (i, 0)), row(0), row(n // tm),
                  pl.BlockSpec(ln.shape, lambda i: (0, 0))],
        out_specs=row(0),
        out_shape=jax.ShapeDtypeStruct((n, D_MODEL), F32),
        compiler_params=_cparams("arbitrary"),
        name="moe_combine_ln",
    )(x, route, ye, ye, ln)


def _hier_moe_ln(x, wts, blk):
    n = x.shape[0]
    route, cnt = _router(x, wts["wr"], wts["br"], _tile(n, 512))
    counts = cnt[0, ROUTE_OFF:ROUTE_OFF + N_EXPERTS].astype(jnp.int32)
    pcounts = (counts + blk - 1) // blk * blk
    pend = jnp.cumsum(pcounts)
    pstart = pend - pcounts
    eidx = route[:, 0:2].astype(jnp.int32)
    dest = (pstart[eidx] + route[:, 4:6].astype(jnp.int32)).reshape(-1)
    n_slots = -(-(2 * n + N_EXPERTS * (blk - 1)) // blk) * blk
    n_blk = n_slots // blk
    blk_start = jnp.arange(n_blk, dtype=jnp.int32) * blk
    blk_e = jnp.minimum(jnp.sum(pend[None, :] <= blk_start[:, None], axis=1), N_EXPERTS - 1).astype(jnp.int32)
    n_act = (pend[-1:] // blk).astype(jnp.int32)
    slot_src, slot_dst = _slotmap(dest, pstart + counts, pend, n, n_slots, blk)
    ye = _experts(slot_src, slot_dst, blk_e, n_act, x, wts["w1"], wts["w3"], wts["w2"], wts["l"], blk)
    return _combine(x, route, ye, wts["ln3"], _tile(n, 256))


def _layer(x, mk, mv, conv_st, shift_st, wkv, wts):
    b, t, _ = x.shape
    n = b * t
    x2 = x.reshape(n, D_MODEL)
    p = _proj(x2, wts["w_in"], _tile(n, 512), D_IN_PAD // 3).reshape(b, t, D_IN_PAD)
    shift_pad = jnp.pad(shift_st, ((0, 0), (0, 0), (0, D_SHIFT_PAD - D_SHIFT)))
    C = min(CHUNK, t)
    consts = _rwkv_consts(C)
    yc, r, k, v, w, kk, bb, g, nconv, nshift = _prep(p, conv_st, shift_pad, wts["pvec"], wts["mu"], wts["wl"], consts[3], _tile(t, 256))
    yr, nwkv = _rwkv(r, w, k, v, kk, bb, g, wkv, wts["hvec"], consts, _tile(t, 256), C)
    tm = _tile(n, 256)
    x1 = _out_ln(x2, yc.reshape(n, D_CONV), yr.reshape(n, D_RWKV), wts["w_out"], wts["ln1"], tm)
    xa = _attn(x1.reshape(b, t, D_MODEL), mk, mv, wts["xa_q"], wts["xa_o"], wts["ln2"], _tile(t, 256))
    y = _hier_moe_ln(xa.reshape(n, D_MODEL), wts, 256 if n >= 8192 else 128)
    return y.reshape(b, t, D_MODEL), nconv, nshift[:, :, :D_SHIFT], nwkv


def _ln_rows(g, b):
    return jnp.zeros((8, D_MODEL), F32).at[0].set(g).at[1].set(b)


def kernel(x_prompt, x_sample, mem_prompt, cache_mem_k, cache_mem_v, state_conv, state_shift, state_wkv, w_in, conv_w, shift_mu, decay_w0, decay_w2, iclr_a0, iclr_a2, gate_g2, k_k, k_a, r_k, lnx_g, lnx_b, w_out, ln1_g, ln1_b, xa_q, xa_k, xa_v, xa_o, ln2_g, ln2_b, router_grp, router_grp_b, router_sub, router_sub_b, moe_w1, moe_w3, moe_w2, ln3_g, ln3_b):
    depth = w_in.shape[0]
    bp = x_prompt.shape[0]
    bs = x_sample.shape[0]
    yp, ys = x_prompt, x_sample
    outs = [[] for _ in range(8)]
    for l in range(depth):
        wl = jnp.zeros((D_LORA_PAD, 3 * D_RWKV), F32)
        wl = wl.at[0:DECAY_LORA, 0:D_RWKV].set(decay_w2[l])
        wl = wl.at[DECAY_LORA:DECAY_LORA + A_LORA, D_RWKV:2 * D_RWKV].set(iclr_a2[l])
        wl = wl.at[DECAY_LORA + A_LORA:D_LORA, 2 * D_RWKV:].set(gate_g2[l])
        wr = jnp.zeros((D_MODEL, LANE), F32)
        wr = wr.at[:, 0:N_GROUPS].set(router_grp[l])
        wr = wr.at[:, ROUTE_OFF:ROUTE_OFF + N_EXPERTS].set(jnp.transpose(router_sub[l], (1, 0, 2)).reshape(D_MODEL, N_EXPERTS))
        br = jnp.zeros((1, LANE), F32)
        br = br.at[0, 0:N_GROUPS].set(router_grp_b[l]).at[0, ROUTE_OFF:ROUTE_OFF + N_EXPERTS].set(router_sub_b[l].reshape(-1))
        wts = dict(
            w_in=jnp.pad(w_in[l], ((0, 0), (0, D_IN_PAD - D_IN))).astype(BF16),
            pvec=jnp.zeros((8, D_RWKV), F32).at[0].set(decay_w0[l]).at[1].set(iclr_a0[l]).at[2].set(k_k[l]).at[3].set(k_a[l]).at[4:7].set(conv_w[l]),
            mu=jnp.pad(shift_mu[l], (0, D_SHIFT_PAD - D_SHIFT)).reshape(1, D_SHIFT_PAD),
            wl=wl.astype(BF16),
            hvec=jnp.zeros((8, D_RWKV), F32).at[0].set(r_k[l].reshape(-1)).at[1].set(lnx_g[l]).at[2].set(lnx_b[l]),
            w_out=w_out[l].astype(BF16), ln1=_ln_rows(ln1_g[l], ln1_b[l]),
            xa_q=xa_q[l].astype(BF16), xa_o=xa_o[l].astype(BF16), ln2=_ln_rows(ln2_g[l], ln2_b[l]),
            wr=wr.astype(BF16), br=br,
            w1=moe_w1, w3=moe_w3, w2=moe_w2, l=l, ln3=_ln_rows(ln3_g[l], ln3_b[l]),
        )
        mem2 = mem_prompt.reshape(bp * N_MEM, D_MODEL)
        mk = _proj(mem2, xa_k[l].astype(BF16), _tile(bp * N_MEM, 256), 1024).reshape(bp, N_MEM, D_MODEL)
        mv = _proj(mem2, xa_v[l].astype(BF16), _tile(bp * N_MEM, 256), 1024).reshape(bp, N_MEM, D_MODEL)
        yp, c_p, s_p, w_p = _layer(yp, mk, mv, jnp.zeros((bp, 2, D_CONV), F32), jnp.zeros((bp, 1, D_SHIFT), F32),
                                   jnp.zeros((bp, RWKV_HEADS, RWKV_HEAD, RWKV_HEAD), F32), wts)
        ys, c_s, s_s, w_s = _layer(ys, cache_mem_k[l].reshape(bs, N_MEM, D_MODEL), cache_mem_v[l].reshape(bs, N_MEM, D_MODEL),
                                   state_conv[l], state_shift[l], state_wkv[l], wts)
        for o, val in zip(outs, (mk.reshape(bp, N_MEM, XA_HEADS, XA_HEAD), mv.reshape(bp, N_MEM, XA_HEADS, XA_HEAD),
                                 c_p, s_p, w_p, c_s, s_s, w_s)):
            o.append(val)
    return (yp, ys) + tuple(jnp.stack(o) for o in outs)
```

```python
import functools

import jax
import jax.numpy as jnp
from jax import lax
from jax.experimental import pallas as pl
from jax.experimental.pallas import tpu as pltpu

F32 = jnp.float32
BF16 = jnp.bfloat16

D_MODEL = 2048
D_CONV = 1024
D_RWKV = 1024
RWKV_HEAD = 64
RWKV_HEADS = 16
DECAY_LORA = 64
A_LORA = 64
GATE_LORA = 160
D_LORA = DECAY_LORA + A_LORA + GATE_LORA
D_SHIFT = 3 * D_RWKV + D_LORA
D_IN = 3 * D_CONV + D_SHIFT
LANE = 128
D_LORA_PAD = 384
D_SHIFT_PAD = 3 * D_RWKV + D_LORA_PAD
D_IN_PAD = 3 * D_CONV + D_SHIFT_PAD
N_MEM = 256
XA_HEADS = 4
XA_HEAD = 512
N_GROUPS = 4
EXP_PER_GROUP = 8
N_EXPERTS = 32
EXPERT_FF = 512
LN_EPS = 1e-5
GN_EPS = 64e-5
ALPHA = 2.0 ** 0.25
CHUNK = 64
ROUTE_OFF = N_GROUPS
VMEM_LIMIT = 56 * 1024 * 1024


def _cparams(*sem):
    return pltpu.CompilerParams(dimension_semantics=sem, vmem_limit_bytes=VMEM_LIMIT)


def _dot(a, b):
    return jnp.dot(a.astype(BF16), b.astype(BF16), preferred_element_type=F32)


def _dot_nt(a, b):
    return lax.dot_general(a.astype(BF16), b.astype(BF16), (((1,), (1,)), ((), ())), preferred_element_type=F32)


def _dot_tn(a, b):
    return lax.dot_general(a.astype(BF16), b.astype(BF16), (((0,), (0,)), ((), ())), preferred_element_type=F32)


def _split3(x):
    h1 = x.astype(BF16)
    r1 = x - h1.astype(F32)
    h2 = r1.astype(BF16)
    h3 = (r1 - h2.astype(F32)).astype(BF16)
    return h1, h2, h3


def _dot_exact_rhs(x, m3):
    return jnp.dot(jnp.concatenate(_split3(x), axis=1), m3, preferred_element_type=F32)


def _layer_norm(z, g, b):
    mu = jnp.mean(z, axis=-1, keepdims=True)
    d = z - mu
    var = jnp.mean(d * d, axis=-1, keepdims=True)
    return d * lax.rsqrt(var + LN_EPS) * g + b


def _sigmoid(x):
    return 1.0 / (1.0 + jnp.exp(-x))


def _tile(n, pref):
    t = min(n, pref)
    while t > 8 and (n % t or t % 8):
        t -= 8
    return t if n % t == 0 else n


def _proj_kernel(x_ref, w_ref, o_ref):
    o_ref[...] = _dot(x_ref[...], w_ref[...])


def _proj(x, w, tm, tn):
    n, k = x.shape
    m = w.shape[1]
    return pl.pallas_call(
        _proj_kernel,
        grid=(m // tn, n // tm),
        in_specs=[pl.BlockSpec((tm, k), lambda j, i: (i, 0)), pl.BlockSpec((k, tn), lambda j, i: (0, j))],
        out_specs=pl.BlockSpec((tm, tn), lambda j, i: (i, j)),
        out_shape=jax.ShapeDtypeStruct((n, m), F32),
        compiler_params=_cparams("arbitrary", "arbitrary"),
        name="proj",
    )(x, w)


def _shift_rows(x, heads):
    s = len(heads)
    out = pltpu.roll(x, s, axis=0)
    row = lax.broadcasted_iota(jnp.int32, x.shape, 0)
    for i, h in enumerate(heads):
        out = jnp.where(row == i, h, out)
    return out


def _segsum64(x, e):
    return jnp.concatenate([_dot_exact_rhs(x[:, j * LANE:(j + 1) * LANE], e) for j in range(x.shape[1] // LANE)], axis=1)


def _prep_kernel(p_ref, cst_ref, sst_ref, pv_ref, mu_ref, wl_ref, e_ref,
                 yc_ref, r_ref, k_ref, v_ref, w_ref, kk_ref, bb_ref, g_ref, nconv_ref, nshift_ref,
                 cu, cp):
    tt = p_ref.shape[1]

    @pl.when(pl.program_id(1) == 0)
    def _():
        cu[0:2, :] = cst_ref[0]
        cp[0:1, :] = sst_ref[0]

    bg = p_ref[0, :, 0:D_CONV]
    u = p_ref[0, :, D_CONV:2 * D_CONV] * p_ref[0, :, 2 * D_CONV:3 * D_CONV]
    u1 = _shift_rows(u, [cu[1:2, :]])
    u2 = _shift_rows(u, [cu[0:1, :], cu[1:2, :]])
    yc_ref[0] = bg * (pv_ref[4:5, :] * u2 + pv_ref[5:6, :] * u1 + pv_ref[6:7, :] * u)
    tail = u[tt - 2:tt, :]
    cu[0:2, :] = tail
    nconv_ref[0] = tail

    prw = p_ref[0, :, 3 * D_CONV:]
    prev = _shift_rows(prw, [cp[0:1, :]])
    q = prw + mu_ref[...] * (prev - prw)
    last = prw[tt - 1:tt, :]
    cp[0:1, :] = last
    nshift_ref[0] = last

    r = q[:, 0:D_RWKV]
    k = q[:, D_RWKV:2 * D_RWKV]
    v = q[:, 2 * D_RWKV:3 * D_RWKV]
    ql = q[:, 3 * D_RWKV:]
    lane = lax.broadcasted_iota(jnp.int32, ql.shape, 1)
    act = jnp.where(lane < DECAY_LORA, jnp.tanh(ql), jnp.where(lane < DECAY_LORA + A_LORA, ql, _sigmoid(ql)))
    lo = _dot(act, wl_ref[...])
    z = -(pv_ref[0:1, :] + lo[:, 0:D_RWKV])
    w_log = -(jnp.maximum(z, 0.0) + jnp.log(1.0 + jnp.exp(-jnp.abs(z)))) - 0.5
    a = _sigmoid(pv_ref[1:2, :] + lo[:, D_RWKV:2 * D_RWKV])
    kk = k * pv_ref[2:3, :]
    kk = kk * lax.rsqrt(jnp.maximum(_segsum64(kk * kk, e_ref[...]), 1e-24))
    r_ref[0] = r
    k_ref[0] = k * (1.0 + (a - 1.0) * pv_ref[3:4, :])
    v_ref[0] = v
    w_ref[0] = -jnp.exp(w_log)
    kk_ref[0] = kk
    bb_ref[0] = kk * a
    g_ref[0] = lo[:, 2 * D_RWKV:]


def _prep(p, conv_st, shift_st, pvec, mu, wl, e, tt):
    b, t, _ = p.shape
    row = lambda w: pl.BlockSpec((1, tt, w), lambda i, j: (i, j, 0))
    full = lambda a: pl.BlockSpec(a.shape, lambda i, j: (0,) * a.ndim)
    st = lambda a: pl.BlockSpec((1,) + a.shape[1:], lambda i, j: (i, 0, 0))
    seq = jax.ShapeDtypeStruct((b, t, D_RWKV), F32)
    return pl.pallas_call(
        _prep_kernel,
        grid=(b, t // tt),
        in_specs=[row(D_IN_PAD), st(conv_st), st(shift_st), full(pvec), full(mu), full(wl), full(e)],
        out_specs=[row(D_RWKV)] * 8 + [st(conv_st), st(shift_st)],
        out_shape=[seq] * 8 + [jax.ShapeDtypeStruct(conv_st.shape, F32), jax.ShapeDtypeStruct(shift_st.shape, F32)],
        scratch_shapes=[pltpu.VMEM((8, D_CONV), F32), pltpu.VMEM((8, D_SHIFT_PAD), F32)],
        compiler_params=_cparams("arbitrary", "arbitrary"),
        name="mixer_prep",
    )(p, conv_st, shift_st, pvec, mu, wl, e)


N_PAIRS = RWKV_HEADS // 2


def _rwkv_kernel(r_ref, w_ref, k_ref, v_ref, kk_ref, bb_ref, g_ref, s0_ref, hv_ref, ms_ref, mi_ref, lt_ref, e_ref,
                 y_ref, sout_ref, S, *, C):
    t = pl.program_id(1)
    nt = pl.num_programs(1)
    H = RWKV_HEAD
    lane = lax.broadcasted_iota(jnp.int32, (1, LANE), 1)
    m0 = lane < H

    @pl.when(t == 0)
    def _():
        z = jnp.zeros((H, H), F32)
        for hp in range(N_PAIRS):
            S[hp, 0:H, :] = jnp.concatenate([s0_ref[0, 2 * hp], z], axis=1)
            S[hp, H:2 * H, :] = jnp.concatenate([z, s0_ref[0, 2 * hp + 1]], axis=1)

    pair = lambda x: jnp.concatenate([jnp.where(m0, x, 0.0), jnp.where(m0, 0.0, x)], axis=0)

    def pair_chunk(hp, sl):
        ln = slice(hp * LANE, (hp + 1) * LANE)
        rk = hv_ref[0:1, ln]
        lng = hv_ref[1:2, ln]
        lnb = hv_ref[2:3, ln]
        r = r_ref[0, sl, ln]
        w = w_ref[0, sl, ln]
        k = k_ref[0, sl, ln]
        v = v_ref[0, sl, ln]
        kk = kk_ref[0, sl, ln]
        bb = bb_ref[0, sl, ln]
        cw = jnp.dot(lt_ref[...], jnp.concatenate(_split3(w), axis=0), preferred_element_type=F32)
        yield
        p = jnp.exp(cw)
        ip = jnp.exp(-cw)
        pc = p[C - 1:C, :]
        la = pair(-(jnp.exp(cw - w) * kk))
        lr = pair(p * r)
        rb = pair(bb * ip)
        rkk = pair(k * ip)
        vst = pair(v)
        lhs = jnp.concatenate([la, lr], axis=0)
        rhs = jnp.concatenate([rb, rkk], axis=0)
        gram = _dot_nt(lhs, rhs)
        sm = S[hp]
        ws = _dot_nt(lhs, sm)
        yield
        ms = ms_ref[...] > 0.5
        mi = mi_ref[...] > 0.5
        a_ab = jnp.where(ms, gram[:2 * C, :2 * C], 0.0)
        a_ak = jnp.where(ms, gram[:2 * C, 2 * C:], 0.0)
        a_rb = jnp.where(mi, gram[2 * C:, :2 * C], 0.0)
        a_rk = jnp.where(mi, gram[2 * C:, 2 * C:], 0.0)
        x = ws[:2 * C] + _dot(a_ak, vst)
        tm = (mi_ref[...] - ms_ref[...]) + a_ab
        pw = a_ab
        n = 1
        while 2 * n < C:
            pw = _dot(pw, pw)
            yield
            tm = tm + _dot(pw, tm)
            yield
            n *= 2
        u = _dot(tm, x)
        yield
        uv = jnp.concatenate([u, vst], axis=0)
        ost = ws[2 * C:] + _dot(jnp.concatenate([a_rb, a_rk], axis=1), uv)
        o = ost[:C] + ost[C:]
        S[hp] = sm * pc + _dot_tn(uv, rhs * pc)
        yield
        e3 = e_ref[...]
        sums = _dot_exact_rhs(jnp.concatenate([o, r * k * rk], axis=0), e3)
        yield
        d = o - sums[:C] * (1.0 / H)
        var = _dot_exact_rhs(d * d, e3) * (1.0 / H)
        yield
        on = d * lax.rsqrt(var + GN_EPS) * lng + lnb
        y_ref[0, sl, ln] = (on + sums[C:] * v) * g_ref[0, sl, ln]

    def chunk(c, carry):
        sl = pl.ds(pl.multiple_of(c * C, C), C)
        live = [pair_chunk(hp, sl) for hp in range(N_PAIRS)]
        while live:
            live = [g for g in live if next(g, live) is not live]
        return carry

    lax.fori_loop(0, r_ref.shape[1] // C, chunk, 0)

    @pl.when(t == nt - 1)
    def _():
        for hp in range(N_PAIRS):
            sout_ref[0, 2 * hp] = S[hp, 0:H, 0:H]
            sout_ref[0, 2 * hp + 1] = S[hp, H:2 * H, H:2 * H]


def _rwkv(r, w, k, v, kk, bb, g, s0, hvec, consts, tc, C):
    b, t, _ = r.shape
    ms, mi, lt, e3 = consts
    seq = pl.BlockSpec((1, tc, D_RWKV), lambda i, j: (i, j, 0))
    st = pl.BlockSpec((1, RWKV_HEADS, RWKV_HEAD, RWKV_HEAD), lambda i, j: (i, 0, 0, 0))
    full = lambda a: pl.BlockSpec(a.shape, lambda i, j: (0,) * a.ndim)
    return pl.pallas_call(
        functools.partial(_rwkv_kernel, C=C),
        grid=(b, t // tc),
        in_specs=[seq] * 7 + [st, full(hvec), full(ms), full(mi), full(lt), full(e3)],
        out_specs=[seq, st],
        out_shape=[jax.ShapeDtypeStruct((b, t, D_RWKV), F32), jax.ShapeDtypeStruct(s0.shape, F32)],
        scratch_shapes=[pltpu.VMEM((N_PAIRS, LANE, LANE), F32)],
        compiler_params=_cparams("arbitrary", "arbitrary"),
        name="rwkv7",
    )(r, w, k, v, kk, bb, g, s0, hvec, ms, mi, lt, e3)


def _rwkv_consts(C):
    i = jnp.arange(2 * C)
    same = (i[:, None] // C) == (i[None, :] // C)
    ms = (same & ((i[None, :] % C) < (i[:, None] % C))).astype(F32)
    mi = (same & ((i[None, :] % C) <= (i[:, None] % C))).astype(F32)
    j = jnp.arange(C)
    lt = jnp.tile((j[None, :] <= j[:, None]).astype(BF16), (1, 3))
    l = jnp.arange(LANE)
    e3 = jnp.tile(((l[:, None] // RWKV_HEAD) == (l[None, :] // RWKV_HEAD)).astype(BF16), (3, 1))
    return ms, mi, lt, e3


def _out_ln_kernel(x_ref, yc_ref, yr_ref, w_ref, ln_ref, o_ref):
    z = ALPHA * x_ref[...] + _dot(yc_ref[...], w_ref[0:D_CONV, :]) + _dot(yr_ref[...], w_ref[D_CONV:, :])
    o_ref[...] = _layer_norm(z, ln_ref[0:1, :], ln_ref[1:2, :])


def _out_ln(x, yc, yr, w, ln, tm):
    n = x.shape[0]
    row = lambda wd: pl.BlockSpec((tm, wd), lambda i: (i, 0))
    full = lambda a: pl.BlockSpec(a.shape, lambda i: (0,) * a.ndim)
    return pl.pallas_call(
        _out_ln_kernel,
        grid=(n // tm,),
        in_specs=[row(D_MODEL), row(D_CONV), row(D_RWKV), full(w), full(ln)],
        out_specs=row(D_MODEL),
        out_shape=jax.ShapeDtypeStruct((n, D_MODEL), F32),
        compiler_params=_cparams("arbitrary"),
        name="out_proj_ln",
    )(x, yc, yr, w, ln)


ROW_TILES = D_MODEL // LANE


def _store_rowmajor(ref, val):
    for c in range(ROW_TILES):
        ref[pl.ds(c, val.shape[0], stride=ROW_TILES), :] = val[:, c * LANE:(c + 1) * LANE]


def _load_rowmajor(ref, rows):
    return jnp.concatenate([ref[pl.ds(c, rows, stride=ROW_TILES), :] for c in range(ROW_TILES)], axis=1)


def _attn_kernel(x_ref, mk_ref, mv_ref, wq_ref, wo_ref, ln_ref, o_ref, orm_ref):
    x = x_ref[0]
    q = _dot(x, wq_ref[...])
    heads = []
    for h in range(XA_HEADS):
        sl = slice(h * XA_HEAD, (h + 1) * XA_HEAD)
        s = _dot_nt(q[:, sl], mk_ref[0, :, sl]) * (XA_HEAD ** -0.5)
        s = s - jnp.max(s, axis=-1, keepdims=True)
        pe = jnp.exp(s)
        pr = pe / jnp.sum(pe, axis=-1, keepdims=True)
        heads.append(_dot(pr, mv_ref[0, :, sl]))
    o = jnp.concatenate(heads, axis=1)
    z = ALPHA * x + _dot(o, wo_ref[...])
    y = _layer_norm(z, ln_ref[0:1, :], ln_ref[1:2, :])
    o_ref[0] = y
    _store_rowmajor(orm_ref, y)


def _attn(x, mk, mv, wq, wo, ln, tq):
    b, t, _ = x.shape
    row = pl.BlockSpec((1, tq, D_MODEL), lambda i, j: (i, j, 0))
    mem = pl.BlockSpec((1, N_MEM, D_MODEL), lambda i, j: (i, 0, 0))
    full = lambda a: pl.BlockSpec(a.shape, lambda i, j: (0,) * a.ndim)
    return pl.pallas_call(
        _attn_kernel,
        grid=(b, t // tq),
        in_specs=[row, mem, mem, full(wq), full(wo), full(ln)],
        out_specs=[row, pl.BlockSpec((tq * ROW_TILES, LANE), lambda i, j: (i * (t // tq) + j, 0))],
        out_shape=[jax.ShapeDtypeStruct(x.shape, F32), jax.ShapeDtypeStruct((b * t * ROW_TILES, LANE), F32)],
        compiler_params=_cparams("arbitrary", "arbitrary"),
        name="mem_attn_ln",
    )(x, mk, mv, wq, wo, ln)


def _router_kernel(x_ref, wr_ref, br_ref, lt_ref, route_ref, cnt_ref, cnt):
    i = pl.program_id(0)

    @pl.when(i == 0)
    def _():
        cnt[...] = jnp.zeros_like(cnt)

    neg = -1e30
    big = 1 << 20
    logits = _dot(x_ref[...], wr_ref[...]) + br_ref[...]
    lane = lax.broadcasted_iota(jnp.int32, logits.shape, 1)
    first_arg = lambda val, m: jnp.min(jnp.where(val == m, lane, big), axis=-1, keepdims=True)
    lg = jnp.where(lane < N_GROUPS, logits, neg)
    mg = jnp.max(lg, axis=-1, keepdims=True)
    grp = first_arg(lg, mg)
    p_grp = 1.0 / jnp.sum(jnp.exp(lg - mg), axis=-1, keepdims=True)
    lo = ROUTE_OFF + EXP_PER_GROUP * grp
    ls = jnp.where((lane >= lo) & (lane < lo + EXP_PER_GROUP), logits, neg)
    m1 = jnp.max(ls, axis=-1, keepdims=True)
    j1 = first_arg(ls, m1)
    ls2 = jnp.where(lane == j1, neg, ls)
    m2 = jnp.max(ls2, axis=-1, keepdims=True)
    j2 = first_arg(ls2, m2)
    t2 = jnp.exp(m2 - m1)
    g1 = p_grp * (1.0 / (1.0 + t2))
    g2 = p_grp * (t2 / (1.0 + t2))
    oh1 = lane == j1
    oh2 = lane == j2
    oh = jnp.where(oh1 | oh2, 1.0, 0.0)
    before = jnp.dot(lt_ref[...], oh.astype(BF16), preferred_element_type=F32) + cnt[0:1, :]
    rank1 = jnp.sum(jnp.where(oh1, before, 0.0), axis=-1, keepdims=True)
    rank2 = jnp.sum(jnp.where(oh2, before, 0.0), axis=-1, keepdims=True)
    total = cnt[0:1, :] + jnp.sum(oh, axis=0, keepdims=True)
    cnt[0:1, :] = total
    cnt_ref[...] = jnp.broadcast_to(total, cnt_ref.shape)
    e1 = (j1 - ROUTE_OFF).astype(F32)
    e2 = (j2 - ROUTE_OFF).astype(F32)
    out = jnp.zeros(logits.shape, F32)
    for col, val in enumerate((e1, e2, g1, g2, rank1, rank2)):
        out = jnp.where(lane == col, val, out)
    route_ref[...] = out


def _router(x, wr, br, tm):
    n = x.shape[0]
    j = jnp.arange(tm)
    lt = (j[None, :] < j[:, None]).astype(BF16)
    full = lambda a: pl.BlockSpec(a.shape, lambda i: (0,) * a.ndim)
    return pl.pallas_call(
        _router_kernel,
        grid=(n // tm,),
        in_specs=[pl.BlockSpec((tm, D_MODEL), lambda i: (i, 0)), full(wr), full(br), full(lt)],
        out_specs=[pl.BlockSpec((tm, LANE), lambda i: (i, 0)), pl.BlockSpec((8, LANE), lambda i: (0, 0))],
        out_shape=[jax.ShapeDtypeStruct((n, LANE), F32), jax.ShapeDtypeStruct((8, LANE), F32)],
        scratch_shapes=[pltpu.VMEM((8, LANE), F32)],
        compiler_params=_cparams("arbitrary"),
        name="router",
    )(x, wr, br, lt)


def _slotmap_kernel(dest_ref, lo_ref, hi_ref, src_ref, dst_ref, *, n_tok, blk):
    n_slots = src_ref.shape[0]

    def spare(lo, hi):
        def body(s, c):
            src_ref[s] = 0
            dst_ref[blk + s] = (2 * n_tok + s) * ROW_TILES
            return c

        lax.fori_loop(lo, hi, body, 0)

    def lead(s, c):
        dst_ref[s] = (2 * n_tok + n_slots + s) * ROW_TILES
        return c

    lax.fori_loop(0, blk, lead, 0)
    for e in range(N_EXPERTS):
        spare(lo_ref[e], hi_ref[e])
    spare(hi_ref[N_EXPERTS - 1], n_slots)

    def put(a, c):
        s = dest_ref[a]
        tok = lax.shift_right_logical(a, jnp.int32(1))
        src_ref[s] = tok * ROW_TILES
        dst_ref[blk + s] = ((a & 1) * n_tok + tok) * ROW_TILES
        return c

    lax.fori_loop(0, 2 * n_tok, put, 0, unroll=8)


def _slotmap(dest, pad_lo, pad_hi, n_tok, n_slots, blk):
    smem = pl.BlockSpec(memory_space=pltpu.SMEM)
    return pl.pallas_call(
        functools.partial(_slotmap_kernel, n_tok=n_tok, blk=blk),
        in_specs=[smem, smem, smem],
        out_specs=[smem, smem],
        out_shape=[jax.ShapeDtypeStruct((n_slots,), jnp.int32), jax.ShapeDtypeStruct((n_slots + blk,), jnp.int32)],
        name="moe_slot_map",
    )(dest, pad_lo, pad_hi)


def _expert_kernel(src_ref, dst_ref, be_ref, na_ref, x_hbm, w1_ref, w3_ref, w2_ref, out_hbm,
                   xbuf, ybuf, gsem, ssem, w1s, w3s, w2s, *, blk):
    i = pl.program_id(0)
    na = na_ref[0]
    par = i % 2
    oth = 1 - par
    rows = range(blk)

    def g_copy(p, src_row, r):
        src = x_hbm.at[pl.ds(pl.multiple_of(src_row, ROW_TILES), ROW_TILES)]
        return pltpu.make_async_copy(src, xbuf.at[p, pl.ds(r * ROW_TILES, ROW_TILES)], gsem.at[p])

    def s_copy(p, r, dst_row):
        dst = out_hbm.at[pl.ds(pl.multiple_of(dst_row, ROW_TILES), ROW_TILES)]
        return pltpu.make_async_copy(ybuf.at[p, pl.ds(r * ROW_TILES, ROW_TILES)], dst, ssem.at[p])

    @pl.when(i == 0)
    def _():
        for r in rows:
            g_copy(0, src_ref[r], r).start(priority=1)
        ybuf[1] = jnp.zeros(ybuf.shape[1:], F32)

    @pl.when(i < na)
    def _():
        for r in rows:
            g_copy(par, 0, r).wait()

        @pl.when(i >= 1)
        def _():
            for r in rows:
                s_copy(par, r, 0).wait()

        @pl.when((i == 0) | (be_ref[i] != be_ref[jnp.maximum(i - 1, 0)]))
        def _():
            w1s[...] = w1_ref[0, 0].astype(BF16)
            w3s[...] = w3_ref[0, 0].astype(BF16)
            w2s[...] = w2_ref[0, 0].astype(BF16)

        x = _load_rowmajor(xbuf.at[par], blk).astype(BF16)
        nxt = jnp.minimum(i + 1, na - 1) * blk
        for r in rows:
            g_copy(oth, src_ref[nxt + r], r).start(priority=1)
            s_copy(oth, r, dst_ref[i * blk + r]).start(priority=1)
        h1 = jnp.dot(x, w1s[...], preferred_element_type=F32)
        h3 = jnp.dot(x, w3s[...], preferred_element_type=F32)
        _store_rowmajor(ybuf.at[par], jnp.dot((h1 * _sigmoid(h1) * h3).astype(BF16), w2s[...], preferred_element_type=F32))

        @pl.when(i == na - 1)
        def _():
            for r in rows:
                s_copy(par, r, dst_ref[(i + 1) * blk + r]).start(priority=1)
            for r in rows:
                g_copy(oth, 0, r).wait()
                s_copy(oth, r, 0).wait()
                s_copy(par, r, 0).wait()


def _experts(slot_src, slot_dst, blk_e, n_act, x, w1, w3, w2, l, blk):
    n = x.shape[0] // ROW_TILES
    n_slots = slot_src.shape[0]
    wspec = lambda a: pl.BlockSpec((1, 1) + a.shape[2:], lambda i, s, d, be, na: (l, be[jnp.minimum(i, na[0] - 1)], 0, 0))
    return pl.pallas_call(
        functools.partial(_expert_kernel, blk=blk),
        grid_spec=pltpu.PrefetchScalarGridSpec(
            num_scalar_prefetch=4,
            grid=(n_slots // blk,),
            in_specs=[pl.BlockSpec(memory_space=pl.ANY), wspec(w1), wspec(w3), wspec(w2)],
            out_specs=pl.BlockSpec(memory_space=pl.ANY),
            scratch_shapes=[pltpu.VMEM((2, blk * ROW_TILES, LANE), F32), pltpu.VMEM((2, blk * ROW_TILES, LANE), F32),
                            pltpu.SemaphoreType.DMA((2,)), pltpu.SemaphoreType.DMA((2,)),
                            pltpu.VMEM((D_MODEL, EXPERT_FF), BF16), pltpu.VMEM((D_MODEL, EXPERT_FF), BF16),
                            pltpu.VMEM((EXPERT_FF, D_MODEL), BF16)],
        ),
        out_shape=jax.ShapeDtypeStruct(((2 * n + n_slots + blk) * ROW_TILES, LANE), F32),
        compiler_params=_cparams("arbitrary"),
        name="moe_experts",
    )(slot_src, slot_dst, blk_e, n_act, x, w1, w3, w2)


def _combine_kernel(x_ref, route_ref, y0_ref, y1_ref, ln_ref, o_ref):
    tm = x_ref.shape[0]
    y = route_ref[:, 2:3] * _load_rowmajor(y0_ref, tm) + route_ref[:, 3:4] * _load_rowmajor(y1_ref, tm)
    o_ref[...] = _layer_norm(ALPHA * x_ref[...] + y, ln_ref[0:1, :], ln_ref[1:2, :])


def _combine(x, route, ye, ln, tm):
    n = x.shape[0]
    row = pl.BlockSpec((tm, D_MODEL), lambda i: (i, 0))
    rowmajor = lambda off: pl.BlockSpec((tm * ROW_TILES, LANE), lambda i: (i + off, 0))
    return pl.pallas_call(
        _combine_kernel,
        grid=(n // tm,),
        in_specs=[row, pl.BlockSpec((tm, LANE), lambda i: (i, 0)), rowmajor(0), rowmajor(n // tm),
                  pl.BlockSpec(ln.shape, lambda i: (0, 0))],
        out_specs=row,
        out_shape=jax.ShapeDtypeStruct((n, D_MODEL), F32),
        compiler_params=_cparams("arbitrary"),
        name="moe_combine_ln",
    )(x, route, ye, ye, ln)


def _hier_moe_ln(x, x_rowmajor, wts, blk):
    n = x.shape[0]
    route, cnt = _router(x, wts["wr"], wts["br"], _tile(n, 512))
    counts = cnt[0, ROUTE_OFF:ROUTE_OFF + N_EXPERTS].astype(jnp.int32)
    pcounts = (counts + blk - 1) // blk * blk
    pend = jnp.cumsum(pcounts)
    pstart = pend - pcounts
    eidx = route[:, 0:2].astype(jnp.int32)
    dest = (pstart[eidx] + route[:, 4:6].astype(jnp.int32)).reshape(-1)
    n_slots = -(-(2 * n + N_EXPERTS * (blk - 1)) // blk) * blk
    n_blk = n_slots // blk
    blk_start = jnp.arange(n_blk, dtype=jnp.int32) * blk
    blk_e = jnp.minimum(jnp.sum(pend[None, :] <= blk_start[:, None], axis=1), N_EXPERTS - 1).astype(jnp.int32)
    n_act = (pend[-1:] // blk).astype(jnp.int32)
    slot_src, slot_dst = _slotmap(dest, pstart + counts, pend, n, n_slots, blk)
    ye = _experts(slot_src, slot_dst, blk_e, n_act, x_rowmajor, wts["w1"], wts["w3"], wts["w2"], wts["l"], blk)
    return _combine(x, route, ye, wts["ln3"], _tile(n, 256))


def _layer(x, mk, mv, conv_st, shift_st, wkv, wts):
    b, t, _ = x.shape
    n = b * t
    x2 = x.reshape(n, D_MODEL)
    p = _proj(x2, wts["w_in"], _tile(n, 512), D_IN_PAD // 3).reshape(b, t, D_IN_PAD)
    shift_pad = jnp.pad(shift_st, ((0, 0), (0, 0), (0, D_SHIFT_PAD - D_SHIFT)))
    C = min(CHUNK, t)
    consts = _rwkv_consts(C)
    yc, r, k, v, w, kk, bb, g, nconv, nshift = _prep(p, conv_st, shift_pad, wts["pvec"], wts["mu"], wts["wl"], consts[3], _tile(t, 256))
    yr, nwkv = _rwkv(r, w, k, v, kk, bb, g, wkv, wts["hvec"], consts, _tile(t, 256), C)
    tm = _tile(n, 256)
    x1 = _out_ln(x2, yc.reshape(n, D_CONV), yr.reshape(n, D_RWKV), wts["w_out"], wts["ln1"], tm)
    xa, xa_rowmajor = _attn(x1.reshape(b, t, D_MODEL), mk, mv, wts["xa_q"], wts["xa_o"], wts["ln2"], _tile(t, 256))
    y = _hier_moe_ln(xa.reshape(n, D_MODEL), xa_rowmajor, wts, 256 if n >= 8192 else 128)
    return y.reshape(b, t, D_MODEL), nconv, nshift[:, :, :D_SHIFT], nwkv


def _ln_rows(g, b):
    return jnp.zeros((8, D_MODEL), F32).at[0].set(g).at[1].set(b)


def kernel(x_prompt, x_sample, mem_prompt, cache_mem_k, cache_mem_v, state_conv, state_shift, state_wkv, w_in, conv_w, shift_mu, decay_w0, decay_w2, iclr_a0, iclr_a2, gate_g2, k_k, k_a, r_k, lnx_g, lnx_b, w_out, ln1_g, ln1_b, xa_q, xa_k, xa_v, xa_o, ln2_g, ln2_b, router_grp, router_grp_b, router_sub, router_sub_b, moe_w1, moe_w3, moe_w2, ln3_g, ln3_b):
    depth = w_in.shape[0]
    bp = x_prompt.shape[0]
    bs = x_sample.shape[0]
    yp, ys = x_prompt, x_sample
    outs = [[] for _ in range(8)]
    for l in range(depth):
        wl = jnp.zeros((D_LORA_PAD, 3 * D_RWKV), F32)
        wl = wl.at[0:DECAY_LORA, 0:D_RWKV].set(decay_w2[l])
        wl = wl.at[DECAY_LORA:DECAY_LORA + A_LORA, D_RWKV:2 * D_RWKV].set(iclr_a2[l])
        wl = wl.at[DECAY_LORA + A_LORA:D_LORA, 2 * D_RWKV:].set(gate_g2[l])
        wr = jnp.zeros((D_MODEL, LANE), F32)
        wr = wr.at[:, 0:N_GROUPS].set(router_grp[l])
        wr = wr.at[:, ROUTE_OFF:ROUTE_OFF + N_EXPERTS].set(jnp.transpose(router_sub[l], (1, 0, 2)).reshape(D_MODEL, N_EXPERTS))
        br = jnp.zeros((1, LANE), F32)
        br = br.at[0, 0:N_GROUPS].set(router_grp_b[l]).at[0, ROUTE_OFF:ROUTE_OFF + N_EXPERTS].set(router_sub_b[l].reshape(-1))
        wts = dict(
            w_in=jnp.pad(w_in[l], ((0, 0), (0, D_IN_PAD - D_IN))).astype(BF16),
            pvec=jnp.zeros((8, D_RWKV), F32).at[0].set(decay_w0[l]).at[1].set(iclr_a0[l]).at[2].set(k_k[l]).at[3].set(k_a[l]).at[4:7].set(conv_w[l]),
            mu=jnp.pad(shift_mu[l], (0, D_SHIFT_PAD - D_SHIFT)).reshape(1, D_SHIFT_PAD),
            wl=wl.astype(BF16),
            hvec=jnp.zeros((8, D_RWKV), F32).at[0].set(r_k[l].reshape(-1)).at[1].set(lnx_g[l]).at[2].set(lnx_b[l]),
            w_out=w_out[l].astype(BF16), ln1=_ln_rows(ln1_g[l], ln1_b[l]),
            xa_q=xa_q[l].astype(BF16), xa_o=xa_o[l].astype(BF16), ln2=_ln_rows(ln2_g[l], ln2_b[l]),
            wr=wr.astype(BF16), br=br,
            w1=moe_w1, w3=moe_w3, w2=moe_w2, l=l, ln3=_ln_rows(ln3_g[l], ln3_b[l]),
        )
        mem2 = mem_prompt.reshape(bp * N_MEM, D_MODEL)
        mk = _proj(mem2, xa_k[l].astype(BF16), _tile(bp * N_MEM, 256), 1024).reshape(bp, N_MEM, D_MODEL)
        mv = _proj(mem2, xa_v[l].astype(BF16), _tile(bp * N_MEM, 256), 1024).reshape(bp, N_MEM, D_MODEL)
        yp, c_p, s_p, w_p = _layer(yp, mk, mv, jnp.zeros((bp, 2, D_CONV), F32), jnp.zeros((bp, 1, D_SHIFT), F32),
                                   jnp.zeros((bp, RWKV_HEADS, RWKV_HEAD, RWKV_HEAD), F32), wts)
        ys, c_s, s_s, w_s = _layer(ys, cache_mem_k[l].reshape(bs, N_MEM, D_MODEL), cache_mem_v[l].reshape(bs, N_MEM, D_MODEL),
                                   state_conv[l], state_shift[l], state_wkv[l], wts)
        for o, val in zip(outs, (mk.reshape(bp, N_MEM, XA_HEADS, XA_HEAD), mv.reshape(bp, N_MEM, XA_HEADS, XA_HEAD),
                                 c_p, s_p, w_p, c_s, s_s, w_s)):
            o.append(val)
    return (yp, ys) + tuple(jnp.stack(o) for o in outs)
```

```python
import functools

import jax
import jax.numpy as jnp
from jax import lax
from jax.experimental import pallas as pl
from jax.experimental.pallas import tpu as pltpu

F32 = jnp.float32
BF16 = jnp.bfloat16

D_MODEL = 2048
D_CONV = 1024
D_RWKV = 1024
RWKV_HEAD = 64
RWKV_HEADS = 16
DECAY_LORA = 64
A_LORA = 64
GATE_LORA = 160
D_LORA = DECAY_LORA + A_LORA + GATE_LORA
D_SHIFT = 3 * D_RWKV + D_LORA
D_IN = 3 * D_CONV + D_SHIFT
LANE = 128
D_LORA_PAD = 384
D_SHIFT_PAD = 3 * D_RWKV + D_LORA_PAD
D_IN_PAD = 3 * D_CONV + D_SHIFT_PAD
N_MEM = 256
XA_HEADS = 4
XA_HEAD = 512
N_GROUPS = 4
EXP_PER_GROUP = 8
N_EXPERTS = 32
EXPERT_FF = 512
LN_EPS = 1e-5
GN_EPS = 64e-5
ALPHA = 2.0 ** 0.25
CHUNK = 64
ROUTE_OFF = N_GROUPS
VMEM_LIMIT = 56 * 1024 * 1024


def _cparams(*sem):
    return pltpu.CompilerParams(dimension_semantics=sem, vmem_limit_bytes=VMEM_LIMIT)


def _dot(a, b):
    return jnp.dot(a.astype(BF16), b.astype(BF16), preferred_element_type=F32)


def _dot_nt(a, b):
    return lax.dot_general(a.astype(BF16), b.astype(BF16), (((1,), (1,)), ((), ())), preferred_element_type=F32)


def _dot_tn(a, b):
    return lax.dot_general(a.astype(BF16), b.astype(BF16), (((0,), (0,)), ((), ())), preferred_element_type=F32)


def _split3(x):
    h1 = x.astype(BF16)
    r1 = x - h1.astype(F32)
    h2 = r1.astype(BF16)
    h3 = (r1 - h2.astype(F32)).astype(BF16)
    return h1, h2, h3


def _dot_exact_rhs(x, m3):
    return jnp.dot(jnp.concatenate(_split3(x), axis=1), m3, preferred_element_type=F32)


def _layer_norm(z, g, b):
    mu = jnp.mean(z, axis=-1, keepdims=True)
    d = z - mu
    var = jnp.mean(d * d, axis=-1, keepdims=True)
    return d * lax.rsqrt(var + LN_EPS) * g + b


def _sigmoid(x):
    return 1.0 / (1.0 + jnp.exp(-x))


def _tile(n, pref):
    t = min(n, pref)
    while t > 8 and (n % t or t % 8):
        t -= 8
    return t if n % t == 0 else n


def _proj_kernel(x_ref, w_ref, o_ref):
    o_ref[...] = _dot(x_ref[...], w_ref[...])


def _proj(x, w, tm, tn):
    n, k = x.shape
    m = w.shape[1]
    return pl.pallas_call(
        _proj_kernel,
        grid=(m // tn, n // tm),
        in_specs=[pl.BlockSpec((tm, k), lambda j, i: (i, 0)), pl.BlockSpec((k, tn), lambda j, i: (0, j))],
        out_specs=pl.BlockSpec((tm, tn), lambda j, i: (i, j)),
        out_shape=jax.ShapeDtypeStruct((n, m), F32),
        compiler_params=_cparams("arbitrary", "arbitrary"),
        name="proj",
    )(x, w)


def _shift_rows(x, heads):
    s = len(heads)
    out = pltpu.roll(x, s, axis=0)
    row = lax.broadcasted_iota(jnp.int32, x.shape, 0)
    for i, h in enumerate(heads):
        out = jnp.where(row == i, h, out)
    return out


def _segsum64(x, e):
    return jnp.concatenate([_dot_exact_rhs(x[:, j * LANE:(j + 1) * LANE], e) for j in range(x.shape[1] // LANE)], axis=1)


def _prep_kernel(p_ref, cst_ref, sst_ref, pv_ref, mu_ref, wl_ref, e_ref,
                 yc_ref, r_ref, k_ref, v_ref, w_ref, kk_ref, bb_ref, g_ref, nconv_ref, nshift_ref,
                 cu, cp):
    tt = p_ref.shape[1]

    @pl.when(pl.program_id(1) == 0)
    def _():
        cu[0:2, :] = cst_ref[0]
        cp[0:1, :] = sst_ref[0]

    bg = p_ref[0, :, 0:D_CONV]
    u = p_ref[0, :, D_CONV:2 * D_CONV] * p_ref[0, :, 2 * D_CONV:3 * D_CONV]
    u1 = _shift_rows(u, [cu[1:2, :]])
    u2 = _shift_rows(u, [cu[0:1, :], cu[1:2, :]])
    yc_ref[0] = bg * (pv_ref[4:5, :] * u2 + pv_ref[5:6, :] * u1 + pv_ref[6:7, :] * u)
    tail = u[tt - 2:tt, :]
    cu[0:2, :] = tail
    nconv_ref[0] = tail

    prw = p_ref[0, :, 3 * D_CONV:]
    prev = _shift_rows(prw, [cp[0:1, :]])
    q = prw + mu_ref[...] * (prev - prw)
    last = prw[tt - 1:tt, :]
    cp[0:1, :] = last
    nshift_ref[0] = last

    r = q[:, 0:D_RWKV]
    k = q[:, D_RWKV:2 * D_RWKV]
    v = q[:, 2 * D_RWKV:3 * D_RWKV]
    ql = q[:, 3 * D_RWKV:]
    lane = lax.broadcasted_iota(jnp.int32, ql.shape, 1)
    act = jnp.where(lane < DECAY_LORA, jnp.tanh(ql), jnp.where(lane < DECAY_LORA + A_LORA, ql, _sigmoid(ql)))
    lo = _dot(act, wl_ref[...])
    z = -(pv_ref[0:1, :] + lo[:, 0:D_RWKV])
    w_log = -(jnp.maximum(z, 0.0) + jnp.log(1.0 + jnp.exp(-jnp.abs(z)))) - 0.5
    a = _sigmoid(pv_ref[1:2, :] + lo[:, D_RWKV:2 * D_RWKV])
    kk = k * pv_ref[2:3, :]
    kk = kk * lax.rsqrt(jnp.maximum(_segsum64(kk * kk, e_ref[...]), 1e-24))
    r_ref[0] = r
    k_ref[0] = k * (1.0 + (a - 1.0) * pv_ref[3:4, :])
    v_ref[0] = v
    w_ref[0] = -jnp.exp(w_log)
    kk_ref[0] = kk
    bb_ref[0] = kk * a
    g_ref[0] = lo[:, 2 * D_RWKV:]


def _prep(p, conv_st, shift_st, pvec, mu, wl, e, tt):
    b, t, _ = p.shape
    row = lambda w: pl.BlockSpec((1, tt, w), lambda i, j: (i, j, 0))
    full = lambda a: pl.BlockSpec(a.shape, lambda i, j: (0,) * a.ndim)
    st = lambda a: pl.BlockSpec((1,) + a.shape[1:], lambda i, j: (i, 0, 0))
    seq = jax.ShapeDtypeStruct((b, t, D_RWKV), F32)
    return pl.pallas_call(
        _prep_kernel,
        grid=(b, t // tt),
        in_specs=[row(D_IN_PAD), st(conv_st), st(shift_st), full(pvec), full(mu), full(wl), full(e)],
        out_specs=[row(D_RWKV)] * 8 + [st(conv_st), st(shift_st)],
        out_shape=[seq] * 8 + [jax.ShapeDtypeStruct(conv_st.shape, F32), jax.ShapeDtypeStruct(shift_st.shape, F32)],
        scratch_shapes=[pltpu.VMEM((8, D_CONV), F32), pltpu.VMEM((8, D_SHIFT_PAD), F32)],
        compiler_params=_cparams("arbitrary", "arbitrary"),
        name="mixer_prep",
    )(p, conv_st, shift_st, pvec, mu, wl, e)


N_PAIRS = RWKV_HEADS // 2


def _rwkv_kernel(r_ref, w_ref, k_ref, v_ref, kk_ref, bb_ref, g_ref, s0_ref, hv_ref, ms_ref, mi_ref, lt_ref, e_ref,
                 y_ref, sout_ref, S, *, C):
    t = pl.program_id(1)
    nt = pl.num_programs(1)
    H = RWKV_HEAD
    lane = lax.broadcasted_iota(jnp.int32, (1, LANE), 1)
    m0 = lane < H

    @pl.when(t == 0)
    def _():
        z = jnp.zeros((H, H), F32)
        for hp in range(N_PAIRS):
            S[hp, 0:H, :] = jnp.concatenate([s0_ref[0, 2 * hp], z], axis=1)
            S[hp, H:2 * H, :] = jnp.concatenate([z, s0_ref[0, 2 * hp + 1]], axis=1)

    pair = lambda x: jnp.concatenate([jnp.where(m0, x, 0.0), jnp.where(m0, 0.0, x)], axis=0)

    def pair_chunk(hp, sl):
        ln = slice(hp * LANE, (hp + 1) * LANE)
        rk = hv_ref[0:1, ln]
        lng = hv_ref[1:2, ln]
        lnb = hv_ref[2:3, ln]
        r = r_ref[0, sl, ln]
        w = w_ref[0, sl, ln]
        k = k_ref[0, sl, ln]
        v = v_ref[0, sl, ln]
        kk = kk_ref[0, sl, ln]
        bb = bb_ref[0, sl, ln]
        cw = jnp.dot(lt_ref[...], jnp.concatenate(_split3(w), axis=0), preferred_element_type=F32)
        yield
        p = jnp.exp(cw)
        ip = jnp.exp(-cw)
        pc = p[C - 1:C, :]
        la = pair(-(jnp.exp(cw - w) * kk))
        lr = pair(p * r)
        rb = pair(bb * ip)
        rkk = pair(k * ip)
        vst = pair(v)
        lhs = jnp.concatenate([la, lr], axis=0)
        rhs = jnp.concatenate([rb, rkk], axis=0)
        gram = _dot_nt(lhs, rhs)
        sm = S[hp]
        ws = _dot_nt(lhs, sm)
        yield
        ms = ms_ref[...] > 0.5
        mi = mi_ref[...] > 0.5
        a_ab = jnp.where(ms, gram[:2 * C, :2 * C], 0.0)
        a_ak = jnp.where(ms, gram[:2 * C, 2 * C:], 0.0)
        a_rb = jnp.where(mi, gram[2 * C:, :2 * C], 0.0)
        a_rk = jnp.where(mi, gram[2 * C:, 2 * C:], 0.0)
        x = ws[:2 * C] + _dot(a_ak, vst)
        tm = (mi_ref[...] - ms_ref[...]) + a_ab
        pw = a_ab
        n = 1
        while 2 * n < C:
            pw = _dot(pw, pw)
            yield
            tm = tm + _dot(pw, tm)
            yield
            n *= 2
        u = _dot(tm, x)
        yield
        uv = jnp.concatenate([u, vst], axis=0)
        ost = ws[2 * C:] + _dot(jnp.concatenate([a_rb, a_rk], axis=1), uv)
        o = ost[:C] + ost[C:]
        S[hp] = sm * pc + _dot_tn(uv, rhs * pc)
        yield
        e3 = e_ref[...]
        sums = _dot_exact_rhs(jnp.concatenate([o, r * k * rk], axis=0), e3)
        yield
        d = o - sums[:C] * (1.0 / H)
        var = _dot_exact_rhs(d * d, e3) * (1.0 / H)
        yield
        on = d * lax.rsqrt(var + GN_EPS) * lng + lnb
        y_ref[0, sl, ln] = (on + sums[C:] * v) * g_ref[0, sl, ln]

    def chunk(c, carry):
        sl = pl.ds(pl.multiple_of(c * C, C), C)
        live = [pair_chunk(hp, sl) for hp in range(N_PAIRS)]
        while live:
            live = [g for g in live if next(g, live) is not live]
        return carry

    lax.fori_loop(0, r_ref.shape[1] // C, chunk, 0)

    @pl.when(t == nt - 1)
    def _():
        for hp in range(N_PAIRS):
            sout_ref[0, 2 * hp] = S[hp, 0:H, 0:H]
            sout_ref[0, 2 * hp + 1] = S[hp, H:2 * H, H:2 * H]


def _rwkv(r, w, k, v, kk, bb, g, s0, hvec, consts, tc, C):
    b, t, _ = r.shape
    ms, mi, lt, e3 = consts
    seq = pl.BlockSpec((1, tc, D_RWKV), lambda i, j: (i, j, 0))
    st = pl.BlockSpec((1, RWKV_HEADS, RWKV_HEAD, RWKV_HEAD), lambda i, j: (i, 0, 0, 0))
    full = lambda a: pl.BlockSpec(a.shape, lambda i, j: (0,) * a.ndim)
    return pl.pallas_call(
        functools.partial(_rwkv_kernel, C=C),
        grid=(b, t // tc),
        in_specs=[seq] * 7 + [st, full(hvec), full(ms), full(mi), full(lt), full(e3)],
        out_specs=[seq, st],
        out_shape=[jax.ShapeDtypeStruct((b, t, D_RWKV), F32), jax.ShapeDtypeStruct(s0.shape, F32)],
        scratch_shapes=[pltpu.VMEM((N_PAIRS, LANE, LANE), F32)],
        compiler_params=_cparams("arbitrary", "arbitrary"),
        name="rwkv7",
    )(r, w, k, v, kk, bb, g, s0, hvec, ms, mi, lt, e3)


def _rwkv_consts(C):
    i = jnp.arange(2 * C)
    same = (i[:, None] // C) == (i[None, :] // C)
    ms = (same & ((i[None, :] % C) < (i[:, None] % C))).astype(F32)
    mi = (same & ((i[None, :] % C) <= (i[:, None] % C))).astype(F32)
    j = jnp.arange(C)
    lt = jnp.tile((j[None, :] <= j[:, None]).astype(BF16), (1, 3))
    l = jnp.arange(LANE)
    e3 = jnp.tile(((l[:, None] // RWKV_HEAD) == (l[None, :] // RWKV_HEAD)).astype(BF16), (3, 1))
    return ms, mi, lt, e3


def _out_ln_kernel(x_ref, yc_ref, yr_ref, w_ref, ln_ref, o_ref):
    z = ALPHA * x_ref[...] + _dot(yc_ref[...], w_ref[0:D_CONV, :]) + _dot(yr_ref[...], w_ref[D_CONV:, :])
    o_ref[...] = _layer_norm(z, ln_ref[0:1, :], ln_ref[1:2, :])


def _out_ln(x, yc, yr, w, ln, tm):
    n = x.shape[0]
    row = lambda wd: pl.BlockSpec((tm, wd), lambda i: (i, 0))
    full = lambda a: pl.BlockSpec(a.shape, lambda i: (0,) * a.ndim)
    return pl.pallas_call(
        _out_ln_kernel,
        grid=(n // tm,),
        in_specs=[row(D_MODEL), row(D_CONV), row(D_RWKV), full(w), full(ln)],
        out_specs=row(D_MODEL),
        out_shape=jax.ShapeDtypeStruct((n, D_MODEL), F32),
        compiler_params=_cparams("arbitrary"),
        name="out_proj_ln",
    )(x, yc, yr, w, ln)


ROW_TILES = D_MODEL // LANE


def _store_rowmajor(ref, val):
    for c in range(ROW_TILES):
        ref[pl.ds(c, val.shape[0], stride=ROW_TILES), :] = val[:, c * LANE:(c + 1) * LANE]


def _load_rowmajor(ref, rows):
    return jnp.concatenate([ref[pl.ds(c, rows, stride=ROW_TILES), :] for c in range(ROW_TILES)], axis=1)


def _attn_kernel(x_ref, mk_ref, mv_ref, wq_ref, wo_ref, ln_ref, o_ref, orm_ref):
    x = x_ref[0]
    q = _dot(x, wq_ref[...])
    heads = []
    for h in range(XA_HEADS):
        sl = slice(h * XA_HEAD, (h + 1) * XA_HEAD)
        s = _dot_nt(q[:, sl], mk_ref[0, :, sl]) * (XA_HEAD ** -0.5)
        s = s - jnp.max(s, axis=-1, keepdims=True)
        pe = jnp.exp(s)
        pr = pe / jnp.sum(pe, axis=-1, keepdims=True)
        heads.append(_dot(pr, mv_ref[0, :, sl]))
    o = jnp.concatenate(heads, axis=1)
    z = ALPHA * x + _dot(o, wo_ref[...])
    y = _layer_norm(z, ln_ref[0:1, :], ln_ref[1:2, :])
    o_ref[0] = y
    _store_rowmajor(orm_ref, y)


def _attn(x, mk, mv, wq, wo, ln, tq):
    b, t, _ = x.shape
    row = pl.BlockSpec((1, tq, D_MODEL), lambda i, j: (i, j, 0))
    mem = pl.BlockSpec((1, N_MEM, D_MODEL), lambda i, j: (i, 0, 0))
    full = lambda a: pl.BlockSpec(a.shape, lambda i, j: (0,) * a.ndim)
    return pl.pallas_call(
        _attn_kernel,
        grid=(b, t // tq),
        in_specs=[row, mem, mem, full(wq), full(wo), full(ln)],
        out_specs=[row, pl.BlockSpec((tq * ROW_TILES, LANE), lambda i, j: (i * (t // tq) + j, 0))],
        out_shape=[jax.ShapeDtypeStruct(x.shape, F32), jax.ShapeDtypeStruct((b * t * ROW_TILES, LANE), F32)],
        compiler_params=_cparams("arbitrary", "arbitrary"),
        name="mem_attn_ln",
    )(x, mk, mv, wq, wo, ln)


def _router_kernel(x_ref, wr_ref, br_ref, lt_ref, route_ref, cnt_ref, cnt):
    i = pl.program_id(0)

    @pl.when(i == 0)
    def _():
        cnt[...] = jnp.zeros_like(cnt)

    neg = -1e30
    big = 1 << 20
    logits = _dot(x_ref[...], wr_ref[...]) + br_ref[...]
    lane = lax.broadcasted_iota(jnp.int32, logits.shape, 1)
    first_arg = lambda val, m: jnp.min(jnp.where(val == m, lane, big), axis=-1, keepdims=True)
    lg = jnp.where(lane < N_GROUPS, logits, neg)
    mg = jnp.max(lg, axis=-1, keepdims=True)
    grp = first_arg(lg, mg)
    p_grp = 1.0 / jnp.sum(jnp.exp(lg - mg), axis=-1, keepdims=True)
    lo = ROUTE_OFF + EXP_PER_GROUP * grp
    ls = jnp.where((lane >= lo) & (lane < lo + EXP_PER_GROUP), logits, neg)
    m1 = jnp.max(ls, axis=-1, keepdims=True)
    j1 = first_arg(ls, m1)
    ls2 = jnp.where(lane == j1, neg, ls)
    m2 = jnp.max(ls2, axis=-1, keepdims=True)
    j2 = first_arg(ls2, m2)
    t2 = jnp.exp(m2 - m1)
    g1 = p_grp * (1.0 / (1.0 + t2))
    g2 = p_grp * (t2 / (1.0 + t2))
    oh1 = lane == j1
    oh2 = lane == j2
    oh = jnp.where(oh1 | oh2, 1.0, 0.0)
    before = jnp.dot(lt_ref[...], oh.astype(BF16), preferred_element_type=F32) + cnt[0:1, :]
    rank1 = jnp.sum(jnp.where(oh1, before, 0.0), axis=-1, keepdims=True)
    rank2 = jnp.sum(jnp.where(oh2, before, 0.0), axis=-1, keepdims=True)
    total = cnt[0:1, :] + jnp.sum(oh, axis=0, keepdims=True)
    cnt[0:1, :] = total
    cnt_ref[...] = jnp.broadcast_to(total, cnt_ref.shape)
    e1 = (j1 - ROUTE_OFF).astype(F32)
    e2 = (j2 - ROUTE_OFF).astype(F32)
    out = jnp.zeros(logits.shape, F32)
    for col, val in enumerate((e1, e2, g1, g2, rank1, rank2)):
        out = jnp.where(lane == col, val, out)
    route_ref[...] = out


def _router(x, wr, br, tm):
    n = x.shape[0]
    j = jnp.arange(tm)
    lt = (j[None, :] < j[:, None]).astype(BF16)
    full = lambda a: pl.BlockSpec(a.shape, lambda i: (0,) * a.ndim)
    return pl.pallas_call(
        _router_kernel,
        grid=(n // tm,),
        in_specs=[pl.BlockSpec((tm, D_MODEL), lambda i: (i, 0)), full(wr), full(br), full(lt)],
        out_specs=[pl.BlockSpec((tm, LANE), lambda i: (i, 0)), pl.BlockSpec((8, LANE), lambda i: (0, 0))],
        out_shape=[jax.ShapeDtypeStruct((n, LANE), F32), jax.ShapeDtypeStruct((8, LANE), F32)],
        scratch_shapes=[pltpu.VMEM((8, LANE), F32)],
        compiler_params=_cparams("arbitrary"),
        name="router",
    )(x, wr, br, lt)


def _slotmap_kernel(dest_ref, lo_ref, hi_ref, code_ref, *, n_tok, blk):
    n_slots = code_ref.shape[0] - blk

    def spare(lo, hi):
        def body(j, c):
            code_ref[j] = 2 * n_tok + j
            return c

        lax.fori_loop(lo, hi, body, 0)

    spare(0, blk)
    for e in range(N_EXPERTS):
        spare(blk + lo_ref[e], blk + hi_ref[e])
    spare(blk + hi_ref[N_EXPERTS - 1], blk + n_slots)

    def put(a, c):
        code_ref[blk + dest_ref[a]] = a
        return c

    lax.fori_loop(0, 2 * n_tok, put, 0, unroll=16)


def _slotmap(dest, pad_lo, pad_hi, n_tok, n_slots, blk):
    smem = pl.BlockSpec(memory_space=pltpu.SMEM)
    return pl.pallas_call(
        functools.partial(_slotmap_kernel, n_tok=n_tok, blk=blk),
        in_specs=[smem, smem, smem],
        out_specs=smem,
        out_shape=jax.ShapeDtypeStruct((n_slots + blk,), jnp.int32),
        name="moe_slot_map",
    )(dest, pad_lo, pad_hi)


def _expert_kernel(code_ref, be_ref, na_ref, x_hbm, w1_ref, w3_ref, w2_ref, out_hbm,
                   xbuf, ybuf, gsem, ssem, w1s, w3s, w2s, *, blk, n_tok):
    i = pl.program_id(0)
    na = na_ref[0]
    par = i % 2
    oth = 1 - par
    rows = range(blk)

    def g_copy(p, code, r):
        tok = jnp.where(code >= 2 * n_tok, 0, lax.shift_right_logical(code, jnp.int32(1)))
        src = x_hbm.at[pl.ds(pl.multiple_of(tok * ROW_TILES, ROW_TILES), ROW_TILES)]
        return pltpu.make_async_copy(src, xbuf.at[p, pl.ds(r * ROW_TILES, ROW_TILES)], gsem.at[p])

    def s_copy(p, r, code):
        row = jnp.where(code >= 2 * n_tok, code, (code & 1) * n_tok + lax.shift_right_logical(code, jnp.int32(1)))
        dst = out_hbm.at[pl.ds(pl.multiple_of(row * ROW_TILES, ROW_TILES), ROW_TILES)]
        return pltpu.make_async_copy(ybuf.at[p, pl.ds(r * ROW_TILES, ROW_TILES)], dst, ssem.at[p])

    @pl.when(i == 0)
    def _():
        for r in rows:
            g_copy(0, code_ref[blk + r], r).start(priority=r % 2)
        ybuf[1] = jnp.zeros(ybuf.shape[1:], F32)

    @pl.when(i < na)
    def _():
        for r in rows:
            g_copy(par, 0, r).wait()

        @pl.when(i >= 1)
        def _():
            for r in rows:
                s_copy(par, r, 0).wait()

        @pl.when((i == 0) | (be_ref[i] != be_ref[jnp.maximum(i - 1, 0)]))
        def _():
            w1s[...] = w1_ref[0, 0].astype(BF16)
            w3s[...] = w3_ref[0, 0].astype(BF16)
            w2s[...] = w2_ref[0, 0].astype(BF16)

        x = _load_rowmajor(xbuf.at[par], blk).astype(BF16)
        nxt = (jnp.minimum(i + 1, na - 1) + 1) * blk
        for r in rows:
            g_copy(oth, code_ref[nxt + r], r).start(priority=r % 2)
            s_copy(oth, r, code_ref[i * blk + r]).start(priority=1 - r % 2)
        h1 = jnp.dot(x, w1s[...], preferred_element_type=F32)
        h3 = jnp.dot(x, w3s[...], preferred_element_type=F32)
        _store_rowmajor(ybuf.at[par], jnp.dot((h1 * _sigmoid(h1) * h3).astype(BF16), w2s[...], preferred_element_type=F32))

        @pl.when(i == na - 1)
        def _():
            for r in rows:
                s_copy(par, r, code_ref[(i + 1) * blk + r]).start(priority=r % 2)
            for r in rows:
                g_copy(oth, 0, r).wait()
                s_copy(oth, r, 0).wait()
                s_copy(par, r, 0).wait()


def _experts(code, blk_e, n_act, x, w1, w3, w2, l, blk):
    n = x.shape[0] // ROW_TILES
    n_slots = code.shape[0] - blk
    wspec = lambda a: pl.BlockSpec((1, 1) + a.shape[2:], lambda i, c, be, na: (l, be[jnp.minimum(i, na[0] - 1)], 0, 0))
    return pl.pallas_call(
        functools.partial(_expert_kernel, blk=blk, n_tok=n),
        grid_spec=pltpu.PrefetchScalarGridSpec(
            num_scalar_prefetch=3,
            grid=(n_slots // blk,),
            in_specs=[pl.BlockSpec(memory_space=pl.ANY), wspec(w1), wspec(w3), wspec(w2)],
            out_specs=pl.BlockSpec(memory_space=pl.ANY),
            scratch_shapes=[pltpu.VMEM((2, blk * ROW_TILES, LANE), F32), pltpu.VMEM((2, blk * ROW_TILES, LANE), F32),
                            pltpu.SemaphoreType.DMA((2,)), pltpu.SemaphoreType.DMA((2,)),
                            pltpu.VMEM((D_MODEL, EXPERT_FF), BF16), pltpu.VMEM((D_MODEL, EXPERT_FF), BF16),
                            pltpu.VMEM((EXPERT_FF, D_MODEL), BF16)],
        ),
        out_shape=jax.ShapeDtypeStruct(((2 * n + n_slots + blk) * ROW_TILES, LANE), F32),
        compiler_params=_cparams("arbitrary"),
        name="moe_experts",
    )(code, blk_e, n_act, x, w1, w3, w2)


def _combine_kernel(x_ref, route_ref, y0_ref, y1_ref, ln_ref, o_ref):
    tm = x_ref.shape[0]
    y = route_ref[:, 2:3] * _load_rowmajor(y0_ref, tm) + route_ref[:, 3:4] * _load_rowmajor(y1_ref, tm)
    o_ref[...] = _layer_norm(ALPHA * x_ref[...] + y, ln_ref[0:1, :], ln_ref[1:2, :])


def _combine(x, route, ye, ln, tm):
    n = x.shape[0]
    row = pl.BlockSpec((tm, D_MODEL), lambda i: (i, 0))
    rowmajor = lambda off: pl.BlockSpec((tm * ROW_TILES, LANE), lambda i: (i + off, 0))
    return pl.pallas_call(
        _combine_kernel,
        grid=(n // tm,),
        in_specs=[row, pl.BlockSpec((tm, LANE), lambda i: (i, 0)), rowmajor(0), rowmajor(n // tm),
                  pl.BlockSpec(ln.shape, lambda i: (0, 0))],
        out_specs=row,
        out_shape=jax.ShapeDtypeStruct((n, D_MODEL), F32),
        compiler_params=_cparams("arbitrary"),
        name="moe_combine_ln",
    )(x, route, ye, ye, ln)


def _hier_moe_ln(x, x_rowmajor, wts, blk):
    n = x.shape[0]
    route, cnt = _router(x, wts["wr"], wts["br"], _tile(n, 512))
    counts = cnt[0, ROUTE_OFF:ROUTE_OFF + N_EXPERTS].astype(jnp.int32)
    pcounts = (counts + blk - 1) // blk * blk
    pend = jnp.cumsum(pcounts)
    pstart = pend - pcounts
    eidx = route[:, 0:2].astype(jnp.int32)
    dest = (pstart[eidx] + route[:, 4:6].astype(jnp.int32)).reshape(-1)
    n_slots = -(-(2 * n + N_EXPERTS * (blk - 1)) // blk) * blk
    n_blk = n_slots // blk
    blk_start = jnp.arange(n_blk, dtype=jnp.int32) * blk
    blk_e = jnp.minimum(jnp.sum(pend[None, :] <= blk_start[:, None], axis=1), N_EXPERTS - 1).astype(jnp.int32)
    n_act = (pend[-1:] // blk).astype(jnp.int32)
    code = _slotmap(dest, pstart + counts, pend, n, n_slots, blk)
    ye = _experts(code, blk_e, n_act, x_rowmajor, wts["w1"], wts["w3"], wts["w2"], wts["l"], blk)
    return _combine(x, route, ye, wts["ln3"], _tile(n, 256))


def _layer(x, mk, mv, conv_st, shift_st, wkv, wts):
    b, t, _ = x.shape
    n = b * t
    x2 = x.reshape(n, D_MODEL)
    p = _proj(x2, wts["w_in"], _tile(n, 512), D_IN_PAD // 3).reshape(b, t, D_IN_PAD)
    shift_pad = jnp.pad(shift_st, ((0, 0), (0, 0), (0, D_SHIFT_PAD - D_SHIFT)))
    C = min(CHUNK, t)
    consts = _rwkv_consts(C)
    yc, r, k, v, w, kk, bb, g, nconv, nshift = _prep(p, conv_st, shift_pad, wts["pvec"], wts["mu"], wts["wl"], consts[3], _tile(t, 256))
    yr, nwkv = _rwkv(r, w, k, v, kk, bb, g, wkv, wts["hvec"], consts, _tile(t, 256), C)
    tm = _tile(n, 256)
    x1 = _out_ln(x2, yc.reshape(n, D_CONV), yr.reshape(n, D_RWKV), wts["w_out"], wts["ln1"], tm)
    xa, xa_rowmajor = _attn(x1.reshape(b, t, D_MODEL), mk, mv, wts["xa_q"], wts["xa_o"], wts["ln2"], _tile(t, 256))
    y = _hier_moe_ln(xa.reshape(n, D_MODEL), xa_rowmajor, wts, 256 if n >= 8192 else 128)
    return y.reshape(b, t, D_MODEL), nconv, nshift[:, :, :D_SHIFT], nwkv


def _ln_rows(g, b):
    return jnp.zeros((8, D_MODEL), F32).at[0].set(g).at[1].set(b)


def kernel(x_prompt, x_sample, mem_prompt, cache_mem_k, cache_mem_v, state_conv, state_shift, state_wkv, w_in, conv_w, shift_mu, decay_w0, decay_w2, iclr_a0, iclr_a2, gate_g2, k_k, k_a, r_k, lnx_g, lnx_b, w_out, ln1_g, ln1_b, xa_q, xa_k, xa_v, xa_o, ln2_g, ln2_b, router_grp, router_grp_b, router_sub, router_sub_b, moe_w1, moe_w3, moe_w2, ln3_g, ln3_b):
    depth = w_in.shape[0]
    bp = x_prompt.shape[0]
    bs = x_sample.shape[0]
    yp, ys = x_prompt, x_sample
    outs = [[] for _ in range(8)]
    for l in range(depth):
        wl = jnp.zeros((D_LORA_PAD, 3 * D_RWKV), F32)
        wl = wl.at[0:DECAY_LORA, 0:D_RWKV].set(decay_w2[l])
        wl = wl.at[DECAY_LORA:DECAY_LORA + A_LORA, D_RWKV:2 * D_RWKV].set(iclr_a2[l])
        wl = wl.at[DECAY_LORA + A_LORA:D_LORA, 2 * D_RWKV:].set(gate_g2[l])
        wr = jnp.zeros((D_MODEL, LANE), F32)
        wr = wr.at[:, 0:N_GROUPS].set(router_grp[l])
        wr = wr.at[:, ROUTE_OFF:ROUTE_OFF + N_EXPERTS].set(jnp.transpose(router_sub[l], (1, 0, 2)).reshape(D_MODEL, N_EXPERTS))
        br = jnp.zeros((1, LANE), F32)
        br = br.at[0, 0:N_GROUPS].set(router_grp_b[l]).at[0, ROUTE_OFF:ROUTE_OFF + N_EXPERTS].set(router_sub_b[l].reshape(-1))
        wts = dict(
            w_in=jnp.pad(w_in[l], ((0, 0), (0, D_IN_PAD - D_IN))).astype(BF16),
            pvec=jnp.zeros((8, D_RWKV), F32).at[0].set(decay_w0[l]).at[1].set(iclr_a0[l]).at[2].set(k_k[l]).at[3].set(k_a[l]).at[4:7].set(conv_w[l]),
            mu=jnp.pad(shift_mu[l], (0, D_SHIFT_PAD - D_SHIFT)).reshape(1, D_SHIFT_PAD),
            wl=wl.astype(BF16),
            hvec=jnp.zeros((8, D_RWKV), F32).at[0].set(r_k[l].reshape(-1)).at[1].set(lnx_g[l]).at[2].set(lnx_b[l]),
            w_out=w_out[l].astype(BF16), ln1=_ln_rows(ln1_g[l], ln1_b[l]),
            xa_q=xa_q[l].astype(BF16), xa_o=xa_o[l].astype(BF16), ln2=_ln_rows(ln2_g[l], ln2_b[l]),
            wr=wr.astype(BF16), br=br,
            w1=moe_w1, w3=moe_w3, w2=moe_w2, l=l, ln3=_ln_rows(ln3_g[l], ln3_b[l]),
        )
        mem2 = mem_prompt.reshape(bp * N_MEM, D_MODEL)
        mk = _proj(mem2, xa_k[l].astype(BF16), _tile(bp * N_MEM, 256), 1024).reshape(bp, N_MEM, D_MODEL)
        mv = _proj(mem2, xa_v[l].astype(BF16), _tile(bp * N_MEM, 256), 1024).reshape(bp, N_MEM, D_MODEL)
        yp, c_p, s_p, w_p = _layer(yp, mk, mv, jnp.zeros((bp, 2, D_CONV), F32), jnp.zeros((bp, 1, D_SHIFT), F32),
                                   jnp.zeros((bp, RWKV_HEADS, RWKV_HEAD, RWKV_HEAD), F32), wts)
        ys, c_s, s_s, w_s = _layer(ys, cache_mem_k[l].reshape(bs, N_MEM, D_MODEL), cache_mem_v[l].reshape(bs, N_MEM, D_MODEL),
                                   state_conv[l], state_shift[l], state_wkv[l], wts)
        for o, val in zip(outs, (mk.reshape(bp, N_MEM, XA_HEADS, XA_HEAD), mv.reshape(bp, N_MEM, XA_HEADS, XA_HEAD),
                                 c_p, s_p, w_p, c_s, s_s, w_s)):
            o.append(val)
    return (yp, ys) + tuple(jnp.stack(o) for o in outs)
```

```python
import functools
import math

import jax
import jax.numpy as jnp
from jax import lax
from jax.experimental import pallas as pl
from jax.experimental.pallas import tpu as pltpu

F32 = jnp.float32
BF16 = jnp.bfloat16

D_MODEL = 2048
D_CONV = 1024
D_RWKV = 1024
RWKV_HEAD = 64
RWKV_HEADS = 16
DECAY_LORA = 64
A_LORA = 64
GATE_LORA = 160
D_LORA = DECAY_LORA + A_LORA + GATE_LORA
D_SHIFT = 3 * D_RWKV + D_LORA
D_IN = 3 * D_CONV + D_SHIFT
LANE = 128
D_LORA_PAD = 384
D_SHIFT_PAD = 3 * D_RWKV + D_LORA_PAD
D_IN_PAD = 3 * D_CONV + D_SHIFT_PAD
N_MEM = 256
XA_HEADS = 4
XA_HEAD = 512
N_GROUPS = 4
EXP_PER_GROUP = 8
N_EXPERTS = 32
EXPERT_FF = 512
LN_EPS = 1e-5
GN_EPS = 64e-5
ALPHA = 2.0 ** 0.25
CHUNK = 64
ROUTE_OFF = N_GROUPS
MOE_BLOCK = 256
VMEM_LIMIT = 56 * 1024 * 1024


def _cparams(*sem):
    return pltpu.CompilerParams(dimension_semantics=sem, vmem_limit_bytes=VMEM_LIMIT)


def _dot(a, b):
    return jnp.dot(a.astype(BF16), b.astype(BF16), preferred_element_type=F32)


def _dot_nt(a, b):
    return lax.dot_general(a.astype(BF16), b.astype(BF16), (((1,), (1,)), ((), ())), preferred_element_type=F32)


def _dot_tn(a, b):
    return lax.dot_general(a.astype(BF16), b.astype(BF16), (((0,), (0,)), ((), ())), preferred_element_type=F32)


def _split3(x):
    h1 = x.astype(BF16)
    r1 = x - h1.astype(F32)
    h2 = r1.astype(BF16)
    h3 = (r1 - h2.astype(F32)).astype(BF16)
    return h1, h2, h3


def _dot_exact_rhs(x, m3):
    return jnp.dot(jnp.concatenate(_split3(x), axis=1), m3, preferred_element_type=F32)


def _layer_norm(z, g, b):
    mu = jnp.mean(z, axis=-1, keepdims=True)
    d = z - mu
    var = jnp.mean(d * d, axis=-1, keepdims=True)
    return d * lax.rsqrt(var + LN_EPS) * g + b


def _sigmoid(x):
    return 1.0 / (1.0 + jnp.exp(-x))


def _tile(n, pref):
    t = min(n, pref)
    while t > 8 and (n % t or t % 8):
        t -= 8
    return t if n % t == 0 else n


def _proj_kernel(x_ref, w_ref, o_ref):
    o_ref[...] = _dot(x_ref[...], w_ref[...])


def _proj(x, w, tm, tn):
    n, k = x.shape
    m = w.shape[1]
    return pl.pallas_call(
        _proj_kernel,
        grid=(m // tn, n // tm),
        in_specs=[pl.BlockSpec((tm, k), lambda j, i: (i, 0)), pl.BlockSpec((k, tn), lambda j, i: (0, j))],
        out_specs=pl.BlockSpec((tm, tn), lambda j, i: (i, j)),
        out_shape=jax.ShapeDtypeStruct((n, m), F32),
        compiler_params=_cparams("arbitrary", "arbitrary"),
        name="proj",
    )(x, w)


def _shift_rows(x, heads):
    s = len(heads)
    out = pltpu.roll(x, s, axis=0)
    row = lax.broadcasted_iota(jnp.int32, x.shape, 0)
    for i, h in enumerate(heads):
        out = jnp.where(row == i, h, out)
    return out


def _segsum64(x, e):
    return jnp.concatenate([_dot_exact_rhs(x[:, j * LANE:(j + 1) * LANE], e) for j in range(x.shape[1] // LANE)], axis=1)


def _prep_kernel(p_ref, cst_ref, sst_ref, pv_ref, mu_ref, wl_ref, e_ref,
                 yc_ref, r_ref, k_ref, v_ref, w_ref, kk_ref, bb_ref, g_ref, nconv_ref, nshift_ref,
                 cu, cp):
    tt = p_ref.shape[1]

    @pl.when(pl.program_id(1) == 0)
    def _():
        cu[0:2, :] = cst_ref[0]
        cp[0:1, :] = sst_ref[0]

    bg = p_ref[0, :, 0:D_CONV]
    u = p_ref[0, :, D_CONV:2 * D_CONV] * p_ref[0, :, 2 * D_CONV:3 * D_CONV]
    u1 = _shift_rows(u, [cu[1:2, :]])
    u2 = _shift_rows(u, [cu[0:1, :], cu[1:2, :]])
    yc_ref[0] = bg * (pv_ref[4:5, :] * u2 + pv_ref[5:6, :] * u1 + pv_ref[6:7, :] * u)
    tail = u[tt - 2:tt, :]
    cu[0:2, :] = tail
    nconv_ref[0] = tail

    prw = p_ref[0, :, 3 * D_CONV:]
    prev = _shift_rows(prw, [cp[0:1, :]])
    q = prw + mu_ref[...] * (prev - prw)
    last = prw[tt - 1:tt, :]
    cp[0:1, :] = last
    nshift_ref[0] = last

    r = q[:, 0:D_RWKV]
    k = q[:, D_RWKV:2 * D_RWKV]
    v = q[:, 2 * D_RWKV:3 * D_RWKV]
    ql = q[:, 3 * D_RWKV:]
    lane = lax.broadcasted_iota(jnp.int32, ql.shape, 1)
    act = jnp.where(lane < DECAY_LORA, jnp.tanh(ql), jnp.where(lane < DECAY_LORA + A_LORA, ql, _sigmoid(ql)))
    lo = _dot(act, wl_ref[...])
    z = -(pv_ref[0:1, :] + lo[:, 0:D_RWKV])
    w_log = -(jnp.maximum(z, 0.0) + jnp.log(1.0 + jnp.exp(-jnp.abs(z)))) - 0.5
    a = _sigmoid(pv_ref[1:2, :] + lo[:, D_RWKV:2 * D_RWKV])
    kk = k * pv_ref[2:3, :]
    kk = kk * lax.rsqrt(jnp.maximum(_segsum64(kk * kk, e_ref[...]), 1e-24))
    r_ref[0] = r
    k_ref[0] = k * (1.0 + (a - 1.0) * pv_ref[3:4, :])
    v_ref[0] = v
    w_ref[0] = -jnp.exp(w_log)
    kk_ref[0] = kk
    bb_ref[0] = kk * a
    g_ref[0] = lo[:, 2 * D_RWKV:]


def _prep(p, conv_st, shift_st, pvec, mu, wl, e, tt):
    b, t, _ = p.shape
    row = lambda w: pl.BlockSpec((1, tt, w), lambda i, j: (i, j, 0))
    full = lambda a: pl.BlockSpec(a.shape, lambda i, j: (0,) * a.ndim)
    st = lambda a: pl.BlockSpec((1,) + a.shape[1:], lambda i, j: (i, 0, 0))
    seq = jax.ShapeDtypeStruct((b, t, D_RWKV), F32)
    return pl.pallas_call(
        _prep_kernel,
        grid=(b, t // tt),
        in_specs=[row(D_IN_PAD), st(conv_st), st(shift_st), full(pvec), full(mu), full(wl), full(e)],
        out_specs=[row(D_RWKV)] * 8 + [st(conv_st), st(shift_st)],
        out_shape=[seq] * 8 + [jax.ShapeDtypeStruct(conv_st.shape, F32), jax.ShapeDtypeStruct(shift_st.shape, F32)],
        scratch_shapes=[pltpu.VMEM((8, D_CONV), F32), pltpu.VMEM((8, D_SHIFT_PAD), F32)],
        compiler_params=_cparams("arbitrary", "arbitrary"),
        name="mixer_prep",
    )(p, conv_st, shift_st, pvec, mu, wl, e)


N_PAIRS = RWKV_HEADS // 2


def _rwkv_kernel(r_ref, w_ref, k_ref, v_ref, kk_ref, bb_ref, g_ref, s0_ref, hv_ref, ms_ref, mi_ref, lt_ref, e_ref,
                 y_ref, sout_ref, S, *, C):
    t = pl.program_id(1)
    nt = pl.num_programs(1)
    H = RWKV_HEAD
    lane = lax.broadcasted_iota(jnp.int32, (1, LANE), 1)
    m0 = lane < H

    @pl.when(t == 0)
    def _():
        z = jnp.zeros((H, H), F32)
        for hp in range(N_PAIRS):
            S[hp, 0:H, :] = jnp.concatenate([s0_ref[0, 2 * hp], z], axis=1)
            S[hp, H:2 * H, :] = jnp.concatenate([z, s0_ref[0, 2 * hp + 1]], axis=1)

    pair = lambda x: jnp.concatenate([jnp.where(m0, x, 0.0), jnp.where(m0, 0.0, x)], axis=0)

    def pair_chunk(hp, sl):
        ln = slice(hp * LANE, (hp + 1) * LANE)
        rk = hv_ref[0:1, ln]
        lng = hv_ref[1:2, ln]
        lnb = hv_ref[2:3, ln]
        r = r_ref[0, sl, ln]
        w = w_ref[0, sl, ln]
        k = k_ref[0, sl, ln]
        v = v_ref[0, sl, ln]
        kk = kk_ref[0, sl, ln]
        bb = bb_ref[0, sl, ln]
        cw = jnp.dot(lt_ref[...], jnp.concatenate(_split3(w), axis=0), preferred_element_type=F32)
        yield
        p = jnp.exp(cw)
        ip = jnp.exp(-cw)
        pc = p[C - 1:C, :]
        la = pair(-(jnp.exp(cw - w) * kk))
        lr = pair(p * r)
        rb = pair(bb * ip)
        rkk = pair(k * ip)
        vst = pair(v)
        lhs = jnp.concatenate([la, lr], axis=0)
        rhs = jnp.concatenate([rb, rkk], axis=0)
        gram = _dot_nt(lhs, rhs)
        sm = S[hp]
        ws = _dot_nt(lhs, sm)
        yield
        ms = ms_ref[...] > 0.5
        mi = mi_ref[...] > 0.5
        a_ab = jnp.where(ms, gram[:2 * C, :2 * C], 0.0)
        a_ak = jnp.where(ms, gram[:2 * C, 2 * C:], 0.0)
        a_rb = jnp.where(mi, gram[2 * C:, :2 * C], 0.0)
        a_rk = jnp.where(mi, gram[2 * C:, 2 * C:], 0.0)
        x = ws[:2 * C] + _dot(a_ak, vst)
        tm = (mi_ref[...] - ms_ref[...]) + a_ab
        pw = a_ab
        n = 1
        while 2 * n < C:
            pw = _dot(pw, pw)
            yield
            tm = tm + _dot(pw, tm)
            yield
            n *= 2
        u = _dot(tm, x)
        yield
        uv = jnp.concatenate([u, vst], axis=0)
        ost = ws[2 * C:] + _dot(jnp.concatenate([a_rb, a_rk], axis=1), uv)
        o = ost[:C] + ost[C:]
        S[hp] = sm * pc + _dot_tn(uv, rhs * pc)
        yield
        e3 = e_ref[...]
        sums = _dot_exact_rhs(jnp.concatenate([o, r * k * rk], axis=0), e3)
        yield
        d = o - sums[:C] * (1.0 / H)
        var = _dot_exact_rhs(d * d, e3) * (1.0 / H)
        yield
        on = d * lax.rsqrt(var + GN_EPS) * lng + lnb
        y_ref[0, sl, ln] = (on + sums[C:] * v) * g_ref[0, sl, ln]

    def chunk(c, carry):
        sl = pl.ds(pl.multiple_of(c * C, C), C)
        live = [pair_chunk(hp, sl) for hp in range(N_PAIRS)]
        while live:
            live = [g for g in live if next(g, live) is not live]
        return carry

    lax.fori_loop(0, r_ref.shape[1] // C, chunk, 0)

    @pl.when(t == nt - 1)
    def _():
        for hp in range(N_PAIRS):
            sout_ref[0, 2 * hp] = S[hp, 0:H, 0:H]
            sout_ref[0, 2 * hp + 1] = S[hp, H:2 * H, H:2 * H]


def _rwkv(r, w, k, v, kk, bb, g, s0, hvec, consts, tc, C):
    b, t, _ = r.shape
    ms, mi, lt, e3 = consts
    seq = pl.BlockSpec((1, tc, D_RWKV), lambda i, j: (i, j, 0))
    st = pl.BlockSpec((1, RWKV_HEADS, RWKV_HEAD, RWKV_HEAD), lambda i, j: (i, 0, 0, 0))
    full = lambda a: pl.BlockSpec(a.shape, lambda i, j: (0,) * a.ndim)
    return pl.pallas_call(
        functools.partial(_rwkv_kernel, C=C),
        grid=(b, t // tc),
        in_specs=[seq] * 7 + [st, full(hvec), full(ms), full(mi), full(lt), full(e3)],
        out_specs=[seq, st],
        out_shape=[jax.ShapeDtypeStruct((b, t, D_RWKV), F32), jax.ShapeDtypeStruct(s0.shape, F32)],
        scratch_shapes=[pltpu.VMEM((N_PAIRS, LANE, LANE), F32)],
        compiler_params=_cparams("arbitrary", "arbitrary"),
        name="rwkv7",
    )(r, w, k, v, kk, bb, g, s0, hvec, ms, mi, lt, e3)


def _rwkv_consts(C):
    i = jnp.arange(2 * C)
    same = (i[:, None] // C) == (i[None, :] // C)
    ms = (same & ((i[None, :] % C) < (i[:, None] % C))).astype(F32)
    mi = (same & ((i[None, :] % C) <= (i[:, None] % C))).astype(F32)
    j = jnp.arange(C)
    lt = jnp.tile((j[None, :] <= j[:, None]).astype(BF16), (1, 3))
    l = jnp.arange(LANE)
    e3 = jnp.tile(((l[:, None] // RWKV_HEAD) == (l[None, :] // RWKV_HEAD)).astype(BF16), (3, 1))
    return ms, mi, lt, e3


def _out_ln_kernel(x_ref, yc_ref, yr_ref, w_ref, ln_ref, o_ref):
    z = ALPHA * x_ref[...] + _dot(yc_ref[...], w_ref[0:D_CONV, :]) + _dot(yr_ref[...], w_ref[D_CONV:, :])
    o_ref[...] = _layer_norm(z, ln_ref[0:1, :], ln_ref[1:2, :])


def _out_ln(x, yc, yr, w, ln, tm):
    n = x.shape[0]
    row = lambda wd: pl.BlockSpec((tm, wd), lambda i: (i, 0))
    full = lambda a: pl.BlockSpec(a.shape, lambda i: (0,) * a.ndim)
    return pl.pallas_call(
        _out_ln_kernel,
        grid=(n // tm,),
        in_specs=[row(D_MODEL), row(D_CONV), row(D_RWKV), full(w), full(ln)],
        out_specs=row(D_MODEL),
        out_shape=jax.ShapeDtypeStruct((n, D_MODEL), F32),
        compiler_params=_cparams("arbitrary"),
        name="out_proj_ln",
    )(x, yc, yr, w, ln)


ROW_TILES = D_MODEL // LANE


def _store_rowmajor(ref, val):
    for c in range(ROW_TILES):
        ref[pl.ds(c, val.shape[0], stride=ROW_TILES), :] = val[:, c * LANE:(c + 1) * LANE]


def _load_rowmajor(ref, rows):
    return jnp.concatenate([ref[pl.ds(c, rows, stride=ROW_TILES), :] for c in range(ROW_TILES)], axis=1)


def _attn_kernel(x_ref, mk_ref, mv_ref, wq_ref, wo_ref, ln_ref, *rest):
    o_ref, orm_ref = rest[-2:]
    x = x_ref[0]
    q = _dot(x, wq_ref[...])
    heads = []
    for h in range(XA_HEADS):
        sl = slice(h * XA_HEAD, (h + 1) * XA_HEAD)
        s = _dot_nt(q[:, sl], mk_ref[0, :, sl]) * (XA_HEAD ** -0.5)
        s = s - jnp.max(s, axis=-1, keepdims=True)
        pe = jnp.exp(s)
        pr = pe / jnp.sum(pe, axis=-1, keepdims=True)
        heads.append(_dot(pr, mv_ref[0, :, sl]))
    o = jnp.concatenate(heads, axis=1)
    z = ALPHA * x + _dot(o, wo_ref[...])
    y = _layer_norm(z, ln_ref[0:1, :], ln_ref[1:2, :])
    o_ref[...] = y
    _store_rowmajor(orm_ref, y)


def _attn(x, mk, mv, wq, wo, ln, tq, n_total, row_off, prev):
    b, t, _ = x.shape
    nt = t // tq
    row = pl.BlockSpec((1, tq, D_MODEL), lambda i, j: (i, j, 0))
    mem = pl.BlockSpec((1, N_MEM, D_MODEL), lambda i, j: (i, 0, 0))
    full = lambda a: pl.BlockSpec(a.shape, lambda i, j: (0,) * a.ndim)
    out_idx = lambda i, j: (row_off // tq + i * nt + j, 0)
    args = (x, mk, mv, wq, wo, ln) + tuple(prev or ())
    return pl.pallas_call(
        _attn_kernel,
        grid=(b, nt),
        in_specs=[row, mem, mem, full(wq), full(wo), full(ln)] + [pl.BlockSpec(memory_space=pl.ANY)] * (len(args) - 6),
        out_specs=[pl.BlockSpec((tq, D_MODEL), out_idx), pl.BlockSpec((tq * ROW_TILES, LANE), out_idx)],
        out_shape=[jax.ShapeDtypeStruct((n_total, D_MODEL), F32), jax.ShapeDtypeStruct((n_total * ROW_TILES, LANE), F32)],
        input_output_aliases={6: 0, 7: 1} if prev else {},
        compiler_params=_cparams("arbitrary", "arbitrary"),
        name="mem_attn_ln",
    )(*args)


def _router_kernel(x_ref, wr_ref, br_ref, lt_ref, route_ref, cnt_ref, cnt):
    i = pl.program_id(0)

    @pl.when(i == 0)
    def _():
        cnt[...] = jnp.zeros_like(cnt)

    neg = -1e30
    big = 1 << 20
    logits = _dot(x_ref[...], wr_ref[...]) + br_ref[...]
    lane = lax.broadcasted_iota(jnp.int32, logits.shape, 1)
    first_arg = lambda val, m: jnp.min(jnp.where(val == m, lane, big), axis=-1, keepdims=True)
    lg = jnp.where(lane < N_GROUPS, logits, neg)
    mg = jnp.max(lg, axis=-1, keepdims=True)
    grp = first_arg(lg, mg)
    p_grp = 1.0 / jnp.sum(jnp.exp(lg - mg), axis=-1, keepdims=True)
    lo = ROUTE_OFF + EXP_PER_GROUP * grp
    ls = jnp.where((lane >= lo) & (lane < lo + EXP_PER_GROUP), logits, neg)
    m1 = jnp.max(ls, axis=-1, keepdims=True)
    j1 = first_arg(ls, m1)
    ls2 = jnp.where(lane == j1, neg, ls)
    m2 = jnp.max(ls2, axis=-1, keepdims=True)
    j2 = first_arg(ls2, m2)
    t2 = jnp.exp(m2 - m1)
    g1 = p_grp * (1.0 / (1.0 + t2))
    g2 = p_grp * (t2 / (1.0 + t2))
    oh1 = lane == j1
    oh2 = lane == j2
    oh = jnp.where(oh1 | oh2, 1.0, 0.0)
    before = jnp.dot(lt_ref[...], oh.astype(BF16), preferred_element_type=F32) + cnt[0:1, :]
    rank1 = jnp.sum(jnp.where(oh1, before, 0.0), axis=-1, keepdims=True)
    rank2 = jnp.sum(jnp.where(oh2, before, 0.0), axis=-1, keepdims=True)
    total = cnt[0:1, :] + jnp.sum(oh, axis=0, keepdims=True)
    cnt[0:1, :] = total
    cnt_ref[...] = jnp.broadcast_to(total, cnt_ref.shape)
    e1 = (j1 - ROUTE_OFF).astype(F32)
    e2 = (j2 - ROUTE_OFF).astype(F32)
    out = jnp.zeros(logits.shape, F32)
    for col, val in enumerate((e1, e2, g1, g2, rank1, rank2)):
        out = jnp.where(lane == col, val, out)
    route_ref[...] = out


def _router(x, wr, br, tm):
    n = x.shape[0]
    j = jnp.arange(tm)
    lt = (j[None, :] < j[:, None]).astype(BF16)
    full = lambda a: pl.BlockSpec(a.shape, lambda i: (0,) * a.ndim)
    return pl.pallas_call(
        _router_kernel,
        grid=(n // tm,),
        in_specs=[pl.BlockSpec((tm, D_MODEL), lambda i: (i, 0)), full(wr), full(br), full(lt)],
        out_specs=[pl.BlockSpec((tm, LANE), lambda i: (i, 0)), pl.BlockSpec((8, LANE), lambda i: (0, 0))],
        out_shape=[jax.ShapeDtypeStruct((n, LANE), F32), jax.ShapeDtypeStruct((8, LANE), F32)],
        scratch_shapes=[pltpu.VMEM((8, LANE), F32)],
        compiler_params=_cparams("arbitrary"),
        name="router",
    )(x, wr, br, lt)


def _slotmap_kernel(dest_ref, lo_ref, hi_ref, code_ref, *, n_tok, blk):
    n_slots = code_ref.shape[0] - blk

    def spare(lo, hi):
        def body(j, c):
            code_ref[j] = 2 * n_tok + j
            return c

        lax.fori_loop(lo, hi, body, 0)

    spare(0, blk)
    for e in range(N_EXPERTS):
        spare(blk + lo_ref[e], blk + hi_ref[e])
    spare(blk + hi_ref[N_EXPERTS - 1], blk + n_slots)

    def put(a, c):
        code_ref[blk + dest_ref[a]] = a
        return c

    lax.fori_loop(0, 2 * n_tok, put, 0, unroll=16)


def _slotmap(dest, pad_lo, pad_hi, n_tok, n_slots, blk):
    smem = pl.BlockSpec(memory_space=pltpu.SMEM)
    return pl.pallas_call(
        functools.partial(_slotmap_kernel, n_tok=n_tok, blk=blk),
        in_specs=[smem, smem, smem],
        out_specs=smem,
        out_shape=jax.ShapeDtypeStruct((n_slots + blk,), jnp.int32),
        name="moe_slot_map",
    )(dest, pad_lo, pad_hi)


def _expert_kernel(code_ref, be_ref, na_ref, x_hbm, w1_ref, w3_ref, w2_ref, out_hbm,
                   xbuf, ybuf, gsem, ssem, w1s, w3s, w2s, *, blk, n_tok):
    i = pl.program_id(0)
    na = na_ref[0]
    par = i % 2
    oth = 1 - par
    rows = range(blk)

    def g_copy(p, code, r):
        tok = jnp.where(code >= 2 * n_tok, 0, lax.shift_right_logical(code, jnp.int32(1)))
        src = x_hbm.at[pl.ds(pl.multiple_of(tok * ROW_TILES, ROW_TILES), ROW_TILES)]
        return pltpu.make_async_copy(src, xbuf.at[p, pl.ds(r * ROW_TILES, ROW_TILES)], gsem.at[p])

    def s_copy(p, r, code):
        row = jnp.where(code >= 2 * n_tok, code, (code & 1) * n_tok + lax.shift_right_logical(code, jnp.int32(1)))
        dst = out_hbm.at[pl.ds(pl.multiple_of(row * ROW_TILES, ROW_TILES), ROW_TILES)]
        return pltpu.make_async_copy(ybuf.at[p, pl.ds(r * ROW_TILES, ROW_TILES)], dst, ssem.at[p])

    @pl.when(i == 0)
    def _():
        for r in rows:
            g_copy(0, code_ref[blk + r], r).start(priority=r % 2)
        ybuf[1] = jnp.zeros(ybuf.shape[1:], F32)

    @pl.when(i < na)
    def _():
        for r in rows:
            g_copy(par, 0, r).wait()

        @pl.when(i >= 1)
        def _():
            for r in rows:
                s_copy(par, r, 0).wait()

        @pl.when((i == 0) | (be_ref[i] != be_ref[jnp.maximum(i - 1, 0)]))
        def _():
            w1s[...] = w1_ref[0, 0].astype(BF16)
            w3s[...] = w3_ref[0, 0].astype(BF16)
            w2s[...] = w2_ref[0, 0].astype(BF16)

        x = _load_rowmajor(xbuf.at[par], blk).astype(BF16)
        nxt = (jnp.minimum(i + 1, na - 1) + 1) * blk
        for r in rows:
            g_copy(oth, code_ref[nxt + r], r).start(priority=r % 2)
            s_copy(oth, r, code_ref[i * blk + r]).start(priority=1 - r % 2)
        h1 = jnp.dot(x, w1s[...], preferred_element_type=F32)
        h3 = jnp.dot(x, w3s[...], preferred_element_type=F32)
        _store_rowmajor(ybuf.at[par], jnp.dot((h1 * _sigmoid(h1) * h3).astype(BF16), w2s[...], preferred_element_type=F32))

        @pl.when(i == na - 1)
        def _():
            for r in rows:
                s_copy(par, r, code_ref[(i + 1) * blk + r]).start(priority=r % 2)
            for r in rows:
                g_copy(oth, 0, r).wait()
                s_copy(oth, r, 0).wait()
                s_copy(par, r, 0).wait()


def _experts(code, blk_e, n_act, x, w1, w3, w2, l, blk):
    n = x.shape[0] // ROW_TILES
    n_slots = code.shape[0] - blk
    wspec = lambda a: pl.BlockSpec((1, 1) + a.shape[2:], lambda i, c, be, na: (l, be[jnp.minimum(i, na[0] - 1)], 0, 0))
    return pl.pallas_call(
        functools.partial(_expert_kernel, blk=blk, n_tok=n),
        grid_spec=pltpu.PrefetchScalarGridSpec(
            num_scalar_prefetch=3,
            grid=(n_slots // blk,),
            in_specs=[pl.BlockSpec(memory_space=pl.ANY), wspec(w1), wspec(w3), wspec(w2)],
            out_specs=pl.BlockSpec(memory_space=pl.ANY),
            scratch_shapes=[pltpu.VMEM((2, blk * ROW_TILES, LANE), F32), pltpu.VMEM((2, blk * ROW_TILES, LANE), F32),
                            pltpu.SemaphoreType.DMA((2,)), pltpu.SemaphoreType.DMA((2,)),
                            pltpu.VMEM((D_MODEL, EXPERT_FF), BF16), pltpu.VMEM((D_MODEL, EXPERT_FF), BF16),
                            pltpu.VMEM((EXPERT_FF, D_MODEL), BF16)],
        ),
        out_shape=jax.ShapeDtypeStruct(((2 * n + n_slots + blk) * ROW_TILES, LANE), F32),
        compiler_params=_cparams("arbitrary"),
        name="moe_experts",
    )(code, blk_e, n_act, x, w1, w3, w2)


def _combine_kernel(x_ref, route_ref, y0_ref, y1_ref, ln_ref, oa_ref, ob_ref, *, tiles_a):
    tm = x_ref.shape[0]
    y = route_ref[:, 2:3] * _load_rowmajor(y0_ref, tm) + route_ref[:, 3:4] * _load_rowmajor(y1_ref, tm)
    out = _layer_norm(ALPHA * x_ref[...] + y, ln_ref[0:1, :], ln_ref[1:2, :])
    first = pl.program_id(0) < tiles_a

    @pl.when(first)
    def _():
        oa_ref[...] = out

    @pl.when(jnp.logical_not(first))
    def _():
        ob_ref[...] = out


def _combine(x, route, ye, ln, tm, n_a):
    n = x.shape[0]
    tiles_a = n_a // tm
    row = pl.BlockSpec((tm, D_MODEL), lambda i: (i, 0))
    rowmajor = lambda off: pl.BlockSpec((tm * ROW_TILES, LANE), lambda i: (i + off, 0))
    return pl.pallas_call(
        functools.partial(_combine_kernel, tiles_a=tiles_a),
        grid=(n // tm,),
        in_specs=[row, pl.BlockSpec((tm, LANE), lambda i: (i, 0)), rowmajor(0), rowmajor(n // tm),
                  pl.BlockSpec(ln.shape, lambda i: (0, 0))],
        out_specs=[pl.BlockSpec((tm, D_MODEL), lambda i: (jnp.minimum(i, tiles_a - 1), 0)),
                   pl.BlockSpec((tm, D_MODEL), lambda i: (jnp.maximum(i - tiles_a, 0), 0))],
        out_shape=[jax.ShapeDtypeStruct((n_a, D_MODEL), F32), jax.ShapeDtypeStruct((n - n_a, D_MODEL), F32)],
        compiler_params=_cparams("arbitrary"),
        name="moe_combine_ln",
    )(x, route, ye, ye, ln)


def _hier_moe_ln(x, x_rowmajor, wts, blk, n_a):
    n = x.shape[0]
    route, cnt = _router(x, wts["wr"], wts["br"], _tile(n, 512))
    counts = cnt[0, ROUTE_OFF:ROUTE_OFF + N_EXPERTS].astype(jnp.int32)
    pcounts = (counts + blk - 1) // blk * blk
    pend = jnp.cumsum(pcounts)
    pstart = pend - pcounts
    eidx = route[:, 0:2].astype(jnp.int32)
    dest = (pstart[eidx] + route[:, 4:6].astype(jnp.int32)).reshape(-1)
    n_slots = -(-(2 * n + N_EXPERTS * (blk - 1)) // blk) * blk
    n_blk = n_slots // blk
    blk_start = jnp.arange(n_blk, dtype=jnp.int32) * blk
    blk_e = jnp.minimum(jnp.sum(pend[None, :] <= blk_start[:, None], axis=1), N_EXPERTS - 1).astype(jnp.int32)
    n_act = (pend[-1:] // blk).astype(jnp.int32)
    code = _slotmap(dest, pstart + counts, pend, n, n_slots, blk)
    ye = _experts(code, blk_e, n_act, x_rowmajor, wts["w1"], wts["w3"], wts["w2"], wts["l"], blk)
    return _combine(x, route, ye, wts["ln3"], _tile(math.gcd(n_a, n - n_a), 256), n_a)


def _mixer_attn(x, mk, mv, conv_st, shift_st, wkv, wts, n_total, row_off, prev):
    b, t, _ = x.shape
    n = b * t
    x2 = x.reshape(n, D_MODEL)
    p = _proj(x2, wts["w_in"], _tile(n, 512), D_IN_PAD // 3).reshape(b, t, D_IN_PAD)
    shift_pad = jnp.pad(shift_st, ((0, 0), (0, 0), (0, D_SHIFT_PAD - D_SHIFT)))
    C = min(CHUNK, t)
    consts = _rwkv_consts(C)
    yc, r, k, v, w, kk, bb, g, nconv, nshift = _prep(p, conv_st, shift_pad, wts["pvec"], wts["mu"], wts["wl"], consts[3], _tile(t, 256))
    yr, nwkv = _rwkv(r, w, k, v, kk, bb, g, wkv, wts["hvec"], consts, _tile(t, 256), C)
    tm = _tile(n, 256)
    x1 = _out_ln(x2, yc.reshape(n, D_CONV), yr.reshape(n, D_RWKV), wts["w_out"], wts["ln1"], tm)
    bufs = _attn(x1.reshape(b, t, D_MODEL), mk, mv, wts["xa_q"], wts["xa_o"], wts["ln2"], _tile(t, 256), n_total, row_off, prev)
    return bufs, nconv, nshift[:, :, :D_SHIFT], nwkv


def _ln_rows(g, b):
    return jnp.zeros((8, D_MODEL), F32).at[0].set(g).at[1].set(b)


def kernel(x_prompt, x_sample, mem_prompt, cache_mem_k, cache_mem_v, state_conv, state_shift, state_wkv, w_in, conv_w, shift_mu, decay_w0, decay_w2, iclr_a0, iclr_a2, gate_g2, k_k, k_a, r_k, lnx_g, lnx_b, w_out, ln1_g, ln1_b, xa_q, xa_k, xa_v, xa_o, ln2_g, ln2_b, router_grp, router_grp_b, router_sub, router_sub_b, moe_w1, moe_w3, moe_w2, ln3_g, ln3_b):
    depth = w_in.shape[0]
    bp = x_prompt.shape[0]
    bs = x_sample.shape[0]
    yp, ys = x_prompt, x_sample
    outs = [[] for _ in range(8)]
    for l in range(depth):
        wl = jnp.zeros((D_LORA_PAD, 3 * D_RWKV), F32)
        wl = wl.at[0:DECAY_LORA, 0:D_RWKV].set(decay_w2[l])
        wl = wl.at[DECAY_LORA:DECAY_LORA + A_LORA, D_RWKV:2 * D_RWKV].set(iclr_a2[l])
        wl = wl.at[DECAY_LORA + A_LORA:D_LORA, 2 * D_RWKV:].set(gate_g2[l])
        wr = jnp.zeros((D_MODEL, LANE), F32)
        wr = wr.at[:, 0:N_GROUPS].set(router_grp[l])
        wr = wr.at[:, ROUTE_OFF:ROUTE_OFF + N_EXPERTS].set(jnp.transpose(router_sub[l], (1, 0, 2)).reshape(D_MODEL, N_EXPERTS))
        br = jnp.zeros((1, LANE), F32)
        br = br.at[0, 0:N_GROUPS].set(router_grp_b[l]).at[0, ROUTE_OFF:ROUTE_OFF + N_EXPERTS].set(router_sub_b[l].reshape(-1))
        wts = dict(
            w_in=jnp.pad(w_in[l], ((0, 0), (0, D_IN_PAD - D_IN))).astype(BF16),
            pvec=jnp.zeros((8, D_RWKV), F32).at[0].set(decay_w0[l]).at[1].set(iclr_a0[l]).at[2].set(k_k[l]).at[3].set(k_a[l]).at[4:7].set(conv_w[l]),
            mu=jnp.pad(shift_mu[l], (0, D_SHIFT_PAD - D_SHIFT)).reshape(1, D_SHIFT_PAD),
            wl=wl.astype(BF16),
            hvec=jnp.zeros((8, D_RWKV), F32).at[0].set(r_k[l].reshape(-1)).at[1].set(lnx_g[l]).at[2].set(lnx_b[l]),
            w_out=w_out[l].astype(BF16), ln1=_ln_rows(ln1_g[l], ln1_b[l]),
            xa_q=xa_q[l].astype(BF16), xa_o=xa_o[l].astype(BF16), ln2=_ln_rows(ln2_g[l], ln2_b[l]),
            wr=wr.astype(BF16), br=br,
            w1=moe_w1, w3=moe_w3, w2=moe_w2, l=l, ln3=_ln_rows(ln3_g[l], ln3_b[l]),
        )
        mem2 = mem_prompt.reshape(bp * N_MEM, D_MODEL)
        mk = _proj(mem2, xa_k[l].astype(BF16), _tile(bp * N_MEM, 256), 1024).reshape(bp, N_MEM, D_MODEL)
        mv = _proj(mem2, xa_v[l].astype(BF16), _tile(bp * N_MEM, 256), 1024).reshape(bp, N_MEM, D_MODEL)
        n_p = yp.shape[0] * yp.shape[1]
        n_s = ys.shape[0] * ys.shape[1]
        bufs, c_p, s_p, w_p = _mixer_attn(yp, mk, mv, jnp.zeros((bp, 2, D_CONV), F32), jnp.zeros((bp, 1, D_SHIFT), F32),
                                          jnp.zeros((bp, RWKV_HEADS, RWKV_HEAD, RWKV_HEAD), F32), wts, n_p + n_s, 0, None)
        bufs, c_s, s_s, w_s = _mixer_attn(ys, cache_mem_k[l].reshape(bs, N_MEM, D_MODEL), cache_mem_v[l].reshape(bs, N_MEM, D_MODEL),
                                          state_conv[l], state_shift[l], state_wkv[l], wts, n_p + n_s, n_p, bufs)
        y_p, y_s = _hier_moe_ln(bufs[0], bufs[1], wts, MOE_BLOCK, n_p)
        yp = y_p.reshape(yp.shape)
        ys = y_s.reshape(ys.shape)
        for o, val in zip(outs, (mk.reshape(bp, N_MEM, XA_HEADS, XA_HEAD), mv.reshape(bp, N_MEM, XA_HEADS, XA_HEAD),
                                 c_p, s_p, w_p, c_s, s_s, w_s)):
            o.append(val)
    return (yp, ys) + tuple(jnp.stack(o) for o in outs)
```

```python
import functools
import math

import jax
import jax.numpy as jnp
from jax import lax
from jax.experimental import pallas as pl
from jax.experimental.pallas import tpu as pltpu

F32 = jnp.float32
BF16 = jnp.bfloat16

D_MODEL = 2048
D_CONV = 1024
D_RWKV = 1024
RWKV_HEAD = 64
RWKV_HEADS = 16
DECAY_LORA = 64
A_LORA = 64
GATE_LORA = 160
D_LORA = DECAY_LORA + A_LORA + GATE_LORA
D_SHIFT = 3 * D_RWKV + D_LORA
D_IN = 3 * D_CONV + D_SHIFT
LANE = 128
D_LORA_PAD = 384
D_SHIFT_PAD = 3 * D_RWKV + D_LORA_PAD
D_IN_PAD = 3 * D_CONV + D_SHIFT_PAD
N_MEM = 256
XA_HEADS = 4
XA_HEAD = 512
N_GROUPS = 4
EXP_PER_GROUP = 8
N_EXPERTS = 32
EXPERT_FF = 512
LN_EPS = 1e-5
GN_EPS = 64e-5
ALPHA = 2.0 ** 0.25
CHUNK = 64
ROUTE_OFF = N_GROUPS
MOE_BLOCK = 256
VMEM_LIMIT = 56 * 1024 * 1024


def _cparams(*sem):
    return pltpu.CompilerParams(dimension_semantics=sem, vmem_limit_bytes=VMEM_LIMIT)


def _dot(a, b):
    return jnp.dot(a.astype(BF16), b.astype(BF16), preferred_element_type=F32)


def _dot_nt(a, b):
    return lax.dot_general(a.astype(BF16), b.astype(BF16), (((1,), (1,)), ((), ())), preferred_element_type=F32)


def _dot_tn(a, b):
    return lax.dot_general(a.astype(BF16), b.astype(BF16), (((0,), (0,)), ((), ())), preferred_element_type=F32)


def _split3(x):
    h1 = x.astype(BF16)
    r1 = x - h1.astype(F32)
    h2 = r1.astype(BF16)
    h3 = (r1 - h2.astype(F32)).astype(BF16)
    return h1, h2, h3


def _dot_exact_rhs(x, m3):
    return jnp.dot(jnp.concatenate(_split3(x), axis=1), m3, preferred_element_type=F32)


def _layer_norm(z, g, b):
    mu = jnp.mean(z, axis=-1, keepdims=True)
    d = z - mu
    var = jnp.mean(d * d, axis=-1, keepdims=True)
    return d * lax.rsqrt(var + LN_EPS) * g + b


def _sigmoid(x):
    return 1.0 / (1.0 + jnp.exp(-x))


def _tile(n, pref):
    t = min(n, pref)
    while t > 8 and (n % t or t % 8):
        t -= 8
    return t if n % t == 0 else n


def _proj_kernel(x_ref, w_ref, o_ref):
    o_ref[...] = _dot(x_ref[...], w_ref[...])


def _proj(x, w, tm, tn):
    n, k = x.shape
    m = w.shape[1]
    return pl.pallas_call(
        _proj_kernel,
        grid=(m // tn, n // tm),
        in_specs=[pl.BlockSpec((tm, k), lambda j, i: (i, 0)), pl.BlockSpec((k, tn), lambda j, i: (0, j))],
        out_specs=pl.BlockSpec((tm, tn), lambda j, i: (i, j)),
        out_shape=jax.ShapeDtypeStruct((n, m), F32),
        compiler_params=_cparams("arbitrary", "arbitrary"),
        name="proj",
    )(x, w)


def _shift_rows(x, heads):
    s = len(heads)
    out = pltpu.roll(x, s, axis=0)
    row = lax.broadcasted_iota(jnp.int32, x.shape, 0)
    for i, h in enumerate(heads):
        out = jnp.where(row == i, h, out)
    return out


def _segsum64(x, e):
    return jnp.concatenate([_dot_exact_rhs(x[:, j * LANE:(j + 1) * LANE], e) for j in range(x.shape[1] // LANE)], axis=1)


def _mixer_inputs(p_ref, cst_ref, sst_ref, pv_ref, mu_ref, wl_ref, e_ref, yc_ref, nconv_ref, nshift_ref, cu, cp,
                  r_ref, k_ref, v_ref, w_ref, kk_ref, bb_ref, g_ref):
    tt = p_ref.shape[1]

    @pl.when(pl.program_id(1) == 0)
    def _():
        cu[0:2, :] = cst_ref[0]
        cp[0:1, :] = sst_ref[0]

    bg = p_ref[0, :, 0:D_CONV]
    u = p_ref[0, :, D_CONV:2 * D_CONV] * p_ref[0, :, 2 * D_CONV:3 * D_CONV]
    u1 = _shift_rows(u, [cu[1:2, :]])
    u2 = _shift_rows(u, [cu[0:1, :], cu[1:2, :]])
    yc_ref[0] = bg * (pv_ref[4:5, :] * u2 + pv_ref[5:6, :] * u1 + pv_ref[6:7, :] * u)
    tail = u[tt - 2:tt, :]
    cu[0:2, :] = tail
    nconv_ref[0] = tail

    prw = p_ref[0, :, 3 * D_CONV:]
    prev = _shift_rows(prw, [cp[0:1, :]])
    q = prw + mu_ref[...] * (prev - prw)
    last = prw[tt - 1:tt, :]
    cp[0:1, :] = last
    nshift_ref[0] = last

    r = q[:, 0:D_RWKV]
    k = q[:, D_RWKV:2 * D_RWKV]
    v = q[:, 2 * D_RWKV:3 * D_RWKV]
    ql = q[:, 3 * D_RWKV:]
    lane = lax.broadcasted_iota(jnp.int32, ql.shape, 1)
    act = jnp.where(lane < DECAY_LORA, jnp.tanh(ql), jnp.where(lane < DECAY_LORA + A_LORA, ql, _sigmoid(ql)))
    lo = _dot(act, wl_ref[...])
    z = -(pv_ref[0:1, :] + lo[:, 0:D_RWKV])
    w_log = -(jnp.maximum(z, 0.0) + jnp.log(1.0 + jnp.exp(-jnp.abs(z)))) - 0.5
    a = _sigmoid(pv_ref[1:2, :] + lo[:, D_RWKV:2 * D_RWKV])
    kk = k * pv_ref[2:3, :]
    kk = kk * lax.rsqrt(jnp.maximum(_segsum64(kk * kk, e_ref[...]), 1e-24))
    r_ref[...] = r
    k_ref[...] = k * (1.0 + (a - 1.0) * pv_ref[3:4, :])
    v_ref[...] = v
    w_ref[...] = -jnp.exp(w_log)
    kk_ref[...] = kk
    bb_ref[...] = kk * a
    g_ref[...] = lo[:, 2 * D_RWKV:]


N_PAIRS = RWKV_HEADS // 2


def _mixer_kernel(p_ref, cst_ref, sst_ref, s0_ref, pv_ref, mu_ref, wl_ref, hv_ref, ms_ref, mi_ref, lt_ref, e_ref,
                  yc_ref, y_ref, nconv_ref, nshift_ref, sout_ref,
                  cu, cp, S, r_ref, k_ref, v_ref, w_ref, kk_ref, bb_ref, g_ref, *, C):
    _mixer_inputs(p_ref, cst_ref, sst_ref, pv_ref, mu_ref, wl_ref, e_ref, yc_ref, nconv_ref, nshift_ref, cu, cp,
                  r_ref, k_ref, v_ref, w_ref, kk_ref, bb_ref, g_ref)
    t = pl.program_id(1)
    nt = pl.num_programs(1)
    H = RWKV_HEAD
    lane = lax.broadcasted_iota(jnp.int32, (1, LANE), 1)
    m0 = lane < H

    @pl.when(t == 0)
    def _():
        z = jnp.zeros((H, H), F32)
        for hp in range(N_PAIRS):
            S[hp, 0:H, :] = jnp.concatenate([s0_ref[0, 2 * hp], z], axis=1)
            S[hp, H:2 * H, :] = jnp.concatenate([z, s0_ref[0, 2 * hp + 1]], axis=1)

    pair = lambda x: jnp.concatenate([jnp.where(m0, x, 0.0), jnp.where(m0, 0.0, x)], axis=0)

    def pair_chunk(hp, sl):
        ln = slice(hp * LANE, (hp + 1) * LANE)
        rk = hv_ref[0:1, ln]
        lng = hv_ref[1:2, ln]
        lnb = hv_ref[2:3, ln]
        r = r_ref[sl, ln]
        w = w_ref[sl, ln]
        k = k_ref[sl, ln]
        v = v_ref[sl, ln]
        kk = kk_ref[sl, ln]
        bb = bb_ref[sl, ln]
        cw = jnp.dot(lt_ref[...], jnp.concatenate(_split3(w), axis=0), preferred_element_type=F32)
        yield
        p = jnp.exp(cw)
        ip = jnp.exp(-cw)
        pc = p[C - 1:C, :]
        la = pair(-(jnp.exp(cw - w) * kk))
        lr = pair(p * r)
        rb = pair(bb * ip)
        rkk = pair(k * ip)
        vst = pair(v)
        lhs = jnp.concatenate([la, lr], axis=0)
        rhs = jnp.concatenate([rb, rkk], axis=0)
        gram = _dot_nt(lhs, rhs)
        sm = S[hp]
        ws = _dot_nt(lhs, sm)
        yield
        ms = ms_ref[...] > 0.5
        mi = mi_ref[...] > 0.5
        a_ab = jnp.where(ms, gram[:2 * C, :2 * C], 0.0)
        a_ak = jnp.where(ms, gram[:2 * C, 2 * C:], 0.0)
        a_rb = jnp.where(mi, gram[2 * C:, :2 * C], 0.0)
        a_rk = jnp.where(mi, gram[2 * C:, 2 * C:], 0.0)
        x = ws[:2 * C] + _dot(a_ak, vst)
        tm = (mi_ref[...] - ms_ref[...]) + a_ab
        pw = a_ab
        n = 1
        while 2 * n < C:
            pw = _dot(pw, pw)
            yield
            tm = tm + _dot(pw, tm)
            yield
            n *= 2
        u = _dot(tm, x)
        yield
        uv = jnp.concatenate([u, vst], axis=0)
        ost = ws[2 * C:] + _dot(jnp.concatenate([a_rb, a_rk], axis=1), uv)
        o = ost[:C] + ost[C:]
        S[hp] = sm * pc + _dot_tn(uv, rhs * pc)
        yield
        e3 = e_ref[...]
        sums = _dot_exact_rhs(jnp.concatenate([o, r * k * rk], axis=0), e3)
        yield
        d = o - sums[:C] * (1.0 / H)
        var = _dot_exact_rhs(d * d, e3) * (1.0 / H)
        yield
        on = d * lax.rsqrt(var + GN_EPS) * lng + lnb
        y_ref[0, sl, ln] = (on + sums[C:] * v) * g_ref[sl, ln]

    def chunk(c, carry):
        sl = pl.ds(pl.multiple_of(c * C, C), C)
        live = [pair_chunk(hp, sl) for hp in range(N_PAIRS)]
        while live:
            live = [g for g in live if next(g, live) is not live]
        return carry

    lax.fori_loop(0, r_ref.shape[0] // C, chunk, 0)

    @pl.when(t == nt - 1)
    def _():
        for hp in range(N_PAIRS):
            sout_ref[0, 2 * hp] = S[hp, 0:H, 0:H]
            sout_ref[0, 2 * hp + 1] = S[hp, H:2 * H, H:2 * H]


def _mixer(p, conv_st, shift_st, s0, pvec, mu, wl, hvec, consts, tt, C):
    b, t, _ = p.shape
    ms, mi, lt, e3 = consts
    row = lambda w: pl.BlockSpec((1, tt, w), lambda i, j: (i, j, 0))
    st = lambda a: pl.BlockSpec((1,) + a.shape[1:], lambda i, j: (i,) + (0,) * (a.ndim - 1))
    full = lambda a: pl.BlockSpec(a.shape, lambda i, j: (0,) * a.ndim)
    seq = jax.ShapeDtypeStruct((b, t, D_RWKV), F32)
    return pl.pallas_call(
        functools.partial(_mixer_kernel, C=C),
        grid=(b, t // tt),
        in_specs=[row(D_IN_PAD), st(conv_st), st(shift_st), st(s0), full(pvec), full(mu), full(wl), full(hvec),
                  full(ms), full(mi), full(lt), full(e3)],
        out_specs=[row(D_CONV), row(D_RWKV), st(conv_st), st(shift_st), st(s0)],
        out_shape=[seq, seq, jax.ShapeDtypeStruct(conv_st.shape, F32), jax.ShapeDtypeStruct(shift_st.shape, F32),
                   jax.ShapeDtypeStruct(s0.shape, F32)],
        scratch_shapes=[pltpu.VMEM((8, D_CONV), F32), pltpu.VMEM((8, D_SHIFT_PAD), F32),
                        pltpu.VMEM((N_PAIRS, LANE, LANE), F32)] + [pltpu.VMEM((tt, D_RWKV), F32)] * 7,
        compiler_params=_cparams("arbitrary", "arbitrary"),
        name="mixer",
    )(p, conv_st, shift_st, s0, pvec, mu, wl, hvec, ms, mi, lt, e3)


def _rwkv_consts(C):
    i = jnp.arange(2 * C)
    same = (i[:, None] // C) == (i[None, :] // C)
    ms = (same & ((i[None, :] % C) < (i[:, None] % C))).astype(F32)
    mi = (same & ((i[None, :] % C) <= (i[:, None] % C))).astype(F32)
    j = jnp.arange(C)
    lt = jnp.tile((j[None, :] <= j[:, None]).astype(BF16), (1, 3))
    l = jnp.arange(LANE)
    e3 = jnp.tile(((l[:, None] // RWKV_HEAD) == (l[None, :] // RWKV_HEAD)).astype(BF16), (3, 1))
    return ms, mi, lt, e3


def _out_ln_kernel(x_ref, yc_ref, yr_ref, w_ref, ln_ref, o_ref):
    z = ALPHA * x_ref[...] + _dot(yc_ref[...], w_ref[0:D_CONV, :]) + _dot(yr_ref[...], w_ref[D_CONV:, :])
    o_ref[...] = _layer_norm(z, ln_ref[0:1, :], ln_ref[1:2, :])


def _out_ln(x, yc, yr, w, ln, tm):
    n = x.shape[0]
    row = lambda wd: pl.BlockSpec((tm, wd), lambda i: (i, 0))
    full = lambda a: pl.BlockSpec(a.shape, lambda i: (0,) * a.ndim)
    return pl.pallas_call(
        _out_ln_kernel,
        grid=(n // tm,),
        in_specs=[row(D_MODEL), row(D_CONV), row(D_RWKV), full(w), full(ln)],
        out_specs=row(D_MODEL),
        out_shape=jax.ShapeDtypeStruct((n, D_MODEL), F32),
        compiler_params=_cparams("arbitrary"),
        name="out_proj_ln",
    )(x, yc, yr, w, ln)


ROW_TILES = D_MODEL // LANE


def _store_rowmajor(ref, val):
    for c in range(ROW_TILES):
        ref[pl.ds(c, val.shape[0], stride=ROW_TILES), :] = val[:, c * LANE:(c + 1) * LANE]


def _load_rowmajor(ref, rows):
    return jnp.concatenate([ref[pl.ds(c, rows, stride=ROW_TILES), :] for c in range(ROW_TILES)], axis=1)


def _attn_kernel(x_ref, mk_ref, mv_ref, wq_ref, wo_ref, ln_ref, *rest):
    o_ref, orm_ref = rest[-2:]
    x = x_ref[0]
    q = _dot(x, wq_ref[...])
    heads = []
    for h in range(XA_HEADS):
        sl = slice(h * XA_HEAD, (h + 1) * XA_HEAD)
        s = _dot_nt(q[:, sl], mk_ref[0, :, sl]) * (XA_HEAD ** -0.5)
        s = s - jnp.max(s, axis=-1, keepdims=True)
        pe = jnp.exp(s)
        pr = pe / jnp.sum(pe, axis=-1, keepdims=True)
        heads.append(_dot(pr, mv_ref[0, :, sl]))
    o = jnp.concatenate(heads, axis=1)
    z = ALPHA * x + _dot(o, wo_ref[...])
    y = _layer_norm(z, ln_ref[0:1, :], ln_ref[1:2, :])
    o_ref[...] = y
    _store_rowmajor(orm_ref, y)


def _attn(x, mk, mv, wq, wo, ln, tq, n_total, row_off, prev):
    b, t, _ = x.shape
    nt = t // tq
    row = pl.BlockSpec((1, tq, D_MODEL), lambda i, j: (i, j, 0))
    mem = pl.BlockSpec((1, N_MEM, D_MODEL), lambda i, j: (i, 0, 0))
    full = lambda a: pl.BlockSpec(a.shape, lambda i, j: (0,) * a.ndim)
    out_idx = lambda i, j: (row_off // tq + i * nt + j, 0)
    args = (x, mk, mv, wq, wo, ln) + tuple(prev or ())
    return pl.pallas_call(
        _attn_kernel,
        grid=(b, nt),
        in_specs=[row, mem, mem, full(wq), full(wo), full(ln)] + [pl.BlockSpec(memory_space=pl.ANY)] * (len(args) - 6),
        out_specs=[pl.BlockSpec((tq, D_MODEL), out_idx), pl.BlockSpec((tq * ROW_TILES, LANE), out_idx)],
        out_shape=[jax.ShapeDtypeStruct((n_total, D_MODEL), F32), jax.ShapeDtypeStruct((n_total * ROW_TILES, LANE), F32)],
        input_output_aliases={6: 0, 7: 1} if prev else {},
        compiler_params=_cparams("arbitrary", "arbitrary"),
        name="mem_attn_ln",
    )(*args)


def _router_kernel(x_ref, wr_ref, br_ref, lt_ref, route_ref, cnt_ref, cnt):
    i = pl.program_id(0)

    @pl.when(i == 0)
    def _():
        cnt[...] = jnp.zeros_like(cnt)

    neg = -1e30
    big = 1 << 20
    logits = _dot(x_ref[...], wr_ref[...]) + br_ref[...]
    lane = lax.broadcasted_iota(jnp.int32, logits.shape, 1)
    first_arg = lambda val, m: jnp.min(jnp.where(val == m, lane, big), axis=-1, keepdims=True)
    lg = jnp.where(lane < N_GROUPS, logits, neg)
    mg = jnp.max(lg, axis=-1, keepdims=True)
    grp = first_arg(lg, mg)
    p_grp = 1.0 / jnp.sum(jnp.exp(lg - mg), axis=-1, keepdims=True)
    lo = ROUTE_OFF + EXP_PER_GROUP * grp
    ls = jnp.where((lane >= lo) & (lane < lo + EXP_PER_GROUP), logits, neg)
    m1 = jnp.max(ls, axis=-1, keepdims=True)
    j1 = first_arg(ls, m1)
    ls2 = jnp.where(lane == j1, neg, ls)
    m2 = jnp.max(ls2, axis=-1, keepdims=True)
    j2 = first_arg(ls2, m2)
    t2 = jnp.exp(m2 - m1)
    g1 = p_grp * (1.0 / (1.0 + t2))
    g2 = p_grp * (t2 / (1.0 + t2))
    oh1 = lane == j1
    oh2 = lane == j2
    oh = jnp.where(oh1 | oh2, 1.0, 0.0)
    before = jnp.dot(lt_ref[...], oh.astype(BF16), preferred_element_type=F32) + cnt[0:1, :]
    rank1 = jnp.sum(jnp.where(oh1, before, 0.0), axis=-1, keepdims=True)
    rank2 = jnp.sum(jnp.where(oh2, before, 0.0), axis=-1, keepdims=True)
    total = cnt[0:1, :] + jnp.sum(oh, axis=0, keepdims=True)
    cnt[0:1, :] = total
    cnt_ref[...] = jnp.broadcast_to(total, cnt_ref.shape)
    e1 = (j1 - ROUTE_OFF).astype(F32)
    e2 = (j2 - ROUTE_OFF).astype(F32)
    out = jnp.zeros(logits.shape, F32)
    for col, val in enumerate((e1, e2, g1, g2, rank1, rank2)):
        out = jnp.where(lane == col, val, out)
    route_ref[...] = out


def _router(x, wr, br, tm):
    n = x.shape[0]
    j = jnp.arange(tm)
    lt = (j[None, :] < j[:, None]).astype(BF16)
    full = lambda a: pl.BlockSpec(a.shape, lambda i: (0,) * a.ndim)
    return pl.pallas_call(
        _router_kernel,
        grid=(n // tm,),
        in_specs=[pl.BlockSpec((tm, D_MODEL), lambda i: (i, 0)), full(wr), full(br), full(lt)],
        out_specs=[pl.BlockSpec((tm, LANE), lambda i: (i, 0)), pl.BlockSpec((8, LANE), lambda i: (0, 0))],
        out_shape=[jax.ShapeDtypeStruct((n, LANE), F32), jax.ShapeDtypeStruct((8, LANE), F32)],
        scratch_shapes=[pltpu.VMEM((8, LANE), F32)],
        compiler_params=_cparams("arbitrary"),
        name="router",
    )(x, wr, br, lt)


def _slotmap_kernel(dest_ref, lo_ref, hi_ref, code_ref, *, n_tok, blk):
    n_slots = code_ref.shape[0] - blk

    def spare(lo, hi):
        def body(j, c):
            code_ref[j] = 2 * n_tok + j
            return c

        lax.fori_loop(lo, hi, body, 0)

    spare(0, blk)
    for e in range(N_EXPERTS):
        spare(blk + lo_ref[e], blk + hi_ref[e])
    spare(blk + hi_ref[N_EXPERTS - 1], blk + n_slots)

    def put(a, c):
        code_ref[blk + dest_ref[a]] = a
        return c

    lax.fori_loop(0, 2 * n_tok, put, 0, unroll=16)


def _slotmap(dest, pad_lo, pad_hi, n_tok, n_slots, blk):
    smem = pl.BlockSpec(memory_space=pltpu.SMEM)
    return pl.pallas_call(
        functools.partial(_slotmap_kernel, n_tok=n_tok, blk=blk),
        in_specs=[smem, smem, smem],
        out_specs=smem,
        out_shape=jax.ShapeDtypeStruct((n_slots + blk,), jnp.int32),
        name="moe_slot_map",
    )(dest, pad_lo, pad_hi)


def _expert_kernel(code_ref, be_ref, na_ref, x_hbm, w1_ref, w3_ref, w2_ref, out_hbm,
                   xbuf, ybuf, gsem, ssem, w1s, w3s, w2s, *, blk, n_tok):
    i = pl.program_id(0)
    na = na_ref[0]
    par = i % 2
    oth = 1 - par
    rows = range(blk)

    def g_copy(p, code, r):
        tok = jnp.where(code >= 2 * n_tok, 0, lax.shift_right_logical(code, jnp.int32(1)))
        src = x_hbm.at[pl.ds(pl.multiple_of(tok * ROW_TILES, ROW_TILES), ROW_TILES)]
        return pltpu.make_async_copy(src, xbuf.at[p, pl.ds(r * ROW_TILES, ROW_TILES)], gsem.at[p])

    def s_copy(p, r, code):
        row = jnp.where(code >= 2 * n_tok, code, (code & 1) * n_tok + lax.shift_right_logical(code, jnp.int32(1)))
        dst = out_hbm.at[pl.ds(pl.multiple_of(row * ROW_TILES, ROW_TILES), ROW_TILES)]
        return pltpu.make_async_copy(ybuf.at[p, pl.ds(r * ROW_TILES, ROW_TILES)], dst, ssem.at[p])

    @pl.when(i == 0)
    def _():
        for r in rows:
            g_copy(0, code_ref[blk + r], r).start(priority=r % 2)
        ybuf[1] = jnp.zeros(ybuf.shape[1:], F32)

    @pl.when(i < na)
    def _():
        for r in rows:
            g_copy(par, 0, r).wait()

        @pl.when(i >= 1)
        def _():
            for r in rows:
                s_copy(par, r, 0).wait()

        @pl.when((i == 0) | (be_ref[i] != be_ref[jnp.maximum(i - 1, 0)]))
        def _():
            w1s[...] = w1_ref[0, 0].astype(BF16)
            w3s[...] = w3_ref[0, 0].astype(BF16)
            w2s[...] = w2_ref[0, 0].astype(BF16)

        x = _load_rowmajor(xbuf.at[par], blk).astype(BF16)
        nxt = (jnp.minimum(i + 1, na - 1) + 1) * blk
        for r in rows:
            g_copy(oth, code_ref[nxt + r], r).start(priority=r % 2)
            s_copy(oth, r, code_ref[i * blk + r]).start(priority=1 - r % 2)
        h1 = jnp.dot(x, w1s[...], preferred_element_type=F32)
        h3 = jnp.dot(x, w3s[...], preferred_element_type=F32)
        _store_rowmajor(ybuf.at[par], jnp.dot((h1 * _sigmoid(h1) * h3).astype(BF16), w2s[...], preferred_element_type=F32))

        @pl.when(i == na - 1)
        def _():
            for r in rows:
                s_copy(par, r, code_ref[(i + 1) * blk + r]).start(priority=r % 2)
            for r in rows:
                g_copy(oth, 0, r).wait()
                s_copy(oth, r, 0).wait()
                s_copy(par, r, 0).wait()


def _experts(code, blk_e, n_act, x, w1, w3, w2, l, blk):
    n = x.shape[0] // ROW_TILES
    n_slots = code.shape[0] - blk
    wspec = lambda a: pl.BlockSpec((1, 1) + a.shape[2:], lambda i, c, be, na: (l, be[jnp.minimum(i, na[0] - 1)], 0, 0))
    return pl.pallas_call(
        functools.partial(_expert_kernel, blk=blk, n_tok=n),
        grid_spec=pltpu.PrefetchScalarGridSpec(
            num_scalar_prefetch=3,
            grid=(n_slots // blk,),
            in_specs=[pl.BlockSpec(memory_space=pl.ANY), wspec(w1), wspec(w3), wspec(w2)],
            out_specs=pl.BlockSpec(memory_space=pl.ANY),
            scratch_shapes=[pltpu.VMEM((2, blk * ROW_TILES, LANE), F32), pltpu.VMEM((2, blk * ROW_TILES, LANE), F32),
                            pltpu.SemaphoreType.DMA((2,)), pltpu.SemaphoreType.DMA((2,)),
                            pltpu.VMEM((D_MODEL, EXPERT_FF), BF16), pltpu.VMEM((D_MODEL, EXPERT_FF), BF16),
                            pltpu.VMEM((EXPERT_FF, D_MODEL), BF16)],
        ),
        out_shape=jax.ShapeDtypeStruct(((2 * n + n_slots + blk) * ROW_TILES, LANE), F32),
        compiler_params=_cparams("arbitrary"),
        name="moe_experts",
    )(code, blk_e, n_act, x, w1, w3, w2)


def _combine_kernel(x_ref, route_ref, y0_ref, y1_ref, ln_ref, oa_ref, ob_ref, *, tiles_a):
    tm = x_ref.shape[0]
    y = route_ref[:, 2:3] * _load_rowmajor(y0_ref, tm) + route_ref[:, 3:4] * _load_rowmajor(y1_ref, tm)
    out = _layer_norm(ALPHA * x_ref[...] + y, ln_ref[0:1, :], ln_ref[1:2, :])
    first = pl.program_id(0) < tiles_a

    @pl.when(first)
    def _():
        oa_ref[...] = out

    @pl.when(jnp.logical_not(first))
    def _():
        ob_ref[...] = out


def _combine(x, route, ye, ln, tm, n_a):
    n = x.shape[0]
    tiles_a = n_a // tm
    row = pl.BlockSpec((tm, D_MODEL), lambda i: (i, 0))
    rowmajor = lambda off: pl.BlockSpec((tm * ROW_TILES, LANE), lambda i: (i + off, 0))
    return pl.pallas_call(
        functools.partial(_combine_kernel, tiles_a=tiles_a),
        grid=(n // tm,),
        in_specs=[row, pl.BlockSpec((tm, LANE), lambda i: (i, 0)), rowmajor(0), rowmajor(n // tm),
                  pl.BlockSpec(ln.shape, lambda i: (0, 0))],
        out_specs=[pl.BlockSpec((tm, D_MODEL), lambda i: (jnp.minimum(i, tiles_a - 1), 0)),
                   pl.BlockSpec((tm, D_MODEL), lambda i: (jnp.maximum(i - tiles_a, 0), 0))],
        out_shape=[jax.ShapeDtypeStruct((n_a, D_MODEL), F32), jax.ShapeDtypeStruct((n - n_a, D_MODEL), F32)],
        compiler_params=_cparams("arbitrary"),
        name="moe_combine_ln",
    )(x, route, ye, ye, ln)


def _hier_moe_ln(x, x_rowmajor, wts, blk, n_a):
    n = x.shape[0]
    route, cnt = _router(x, wts["wr"], wts["br"], _tile(n, 512))
    counts = cnt[0, ROUTE_OFF:ROUTE_OFF + N_EXPERTS].astype(jnp.int32)
    pcounts = (counts + blk - 1) // blk * blk
    pend = jnp.cumsum(pcounts)
    pstart = pend - pcounts
    eidx = route[:, 0:2].astype(jnp.int32)
    dest = (pstart[eidx] + route[:, 4:6].astype(jnp.int32)).reshape(-1)
    n_slots = -(-(2 * n + N_EXPERTS * (blk - 1)) // blk) * blk
    n_blk = n_slots // blk
    blk_start = jnp.arange(n_blk, dtype=jnp.int32) * blk
    blk_e = jnp.minimum(jnp.sum(pend[None, :] <= blk_start[:, None], axis=1), N_EXPERTS - 1).astype(jnp.int32)
    n_act = (pend[-1:] // blk).astype(jnp.int32)
    code = _slotmap(dest, pstart + counts, pend, n, n_slots, blk)
    ye = _experts(code, blk_e, n_act, x_rowmajor, wts["w1"], wts["w3"], wts["w2"], wts["l"], blk)
    return _combine(x, route, ye, wts["ln3"], _tile(math.gcd(n_a, n - n_a), 256), n_a)


def _mixer_attn(x, mk, mv, conv_st, shift_st, wkv, wts, n_total, row_off, prev):
    b, t, _ = x.shape
    n = b * t
    x2 = x.reshape(n, D_MODEL)
    p = _proj(x2, wts["w_in"], _tile(n, 512), D_IN_PAD // 3).reshape(b, t, D_IN_PAD)
    shift_pad = jnp.pad(shift_st, ((0, 0), (0, 0), (0, D_SHIFT_PAD - D_SHIFT)))
    C = min(CHUNK, t)
    consts = _rwkv_consts(C)
    yc, yr, nconv, nshift, nwkv = _mixer(p, conv_st, shift_pad, wkv, wts["pvec"], wts["mu"], wts["wl"], wts["hvec"], consts,
                                         _tile(t, 256), C)
    tm = _tile(n, 256)
    x1 = _out_ln(x2, yc.reshape(n, D_CONV), yr.reshape(n, D_RWKV), wts["w_out"], wts["ln1"], tm)
    bufs = _attn(x1.reshape(b, t, D_MODEL), mk, mv, wts["xa_q"], wts["xa_o"], wts["ln2"], _tile(t, 256), n_total, row_off, prev)
    return bufs, nconv, nshift[:, :, :D_SHIFT], nwkv


def _ln_rows(g, b):
    return jnp.zeros((8, D_MODEL), F32).at[0].set(g).at[1].set(b)


def kernel(x_prompt, x_sample, mem_prompt, cache_mem_k, cache_mem_v, state_conv, state_shift, state_wkv, w_in, conv_w, shift_mu, decay_w0, decay_w2, iclr_a0, iclr_a2, gate_g2, k_k, k_a, r_k, lnx_g, lnx_b, w_out, ln1_g, ln1_b, xa_q, xa_k, xa_v, xa_o, ln2_g, ln2_b, router_grp, router_grp_b, router_sub, router_sub_b, moe_w1, moe_w3, moe_w2, ln3_g, ln3_b):
    depth = w_in.shape[0]
    bp = x_prompt.shape[0]
    bs = x_sample.shape[0]
    yp, ys = x_prompt, x_sample
    outs = [[] for _ in range(8)]
    for l in range(depth):
        wl = jnp.zeros((D_LORA_PAD, 3 * D_RWKV), F32)
        wl = wl.at[0:DECAY_LORA, 0:D_RWKV].set(decay_w2[l])
        wl = wl.at[DECAY_LORA:DECAY_LORA + A_LORA, D_RWKV:2 * D_RWKV].set(iclr_a2[l])
        wl = wl.at[DECAY_LORA + A_LORA:D_LORA, 2 * D_RWKV:].set(gate_g2[l])
        wr = jnp.zeros((D_MODEL, LANE), F32)
        wr = wr.at[:, 0:N_GROUPS].set(router_grp[l])
        wr = wr.at[:, ROUTE_OFF:ROUTE_OFF + N_EXPERTS].set(jnp.transpose(router_sub[l], (1, 0, 2)).reshape(D_MODEL, N_EXPERTS))
        br = jnp.zeros((1, LANE), F32)
        br = br.at[0, 0:N_GROUPS].set(router_grp_b[l]).at[0, ROUTE_OFF:ROUTE_OFF + N_EXPERTS].set(router_sub_b[l].reshape(-1))
        wts = dict(
            w_in=jnp.pad(w_in[l], ((0, 0), (0, D_IN_PAD - D_IN))).astype(BF16),
            pvec=jnp.zeros((8, D_RWKV), F32).at[0].set(decay_w0[l]).at[1].set(iclr_a0[l]).at[2].set(k_k[l]).at[3].set(k_a[l]).at[4:7].set(conv_w[l]),
            mu=jnp.pad(shift_mu[l], (0, D_SHIFT_PAD - D_SHIFT)).reshape(1, D_SHIFT_PAD),
            wl=wl.astype(BF16),
            hvec=jnp.zeros((8, D_RWKV), F32).at[0].set(r_k[l].reshape(-1)).at[1].set(lnx_g[l]).at[2].set(lnx_b[l]),
            w_out=w_out[l].astype(BF16), ln1=_ln_rows(ln1_g[l], ln1_b[l]),
            xa_q=xa_q[l].astype(BF16), xa_o=xa_o[l].astype(BF16), ln2=_ln_rows(ln2_g[l], ln2_b[l]),
            wr=wr.astype(BF16), br=br,
            w1=moe_w1, w3=moe_w3, w2=moe_w2, l=l, ln3=_ln_rows(ln3_g[l], ln3_b[l]),
        )
        mem2 = mem_prompt.reshape(bp * N_MEM, D_MODEL)
        mk = _proj(mem2, xa_k[l].astype(BF16), _tile(bp * N_MEM, 256), 1024).reshape(bp, N_MEM, D_MODEL)
        mv = _proj(mem2, xa_v[l].astype(BF16), _tile(bp * N_MEM, 256), 1024).reshape(bp, N_MEM, D_MODEL)
        n_p = yp.shape[0] * yp.shape[1]
        n_s = ys.shape[0] * ys.shape[1]
        bufs, c_p, s_p, w_p = _mixer_attn(yp, mk, mv, jnp.zeros((bp, 2, D_CONV), F32), jnp.zeros((bp, 1, D_SHIFT), F32),
                                          jnp.zeros((bp, RWKV_HEADS, RWKV_HEAD, RWKV_HEAD), F32), wts, n_p + n_s, 0, None)
        bufs, c_s, s_s, w_s = _mixer_attn(ys, cache_mem_k[l].reshape(bs, N_MEM, D_MODEL), cache_mem_v[l].reshape(bs, N_MEM, D_MODEL),
                                          state_conv[l], state_shift[l], state_wkv[l], wts, n_p + n_s, n_p, bufs)
        y_p, y_s = _hier_moe_ln(bufs[0], bufs[1], wts, MOE_BLOCK, n_p)
        yp = y_p.reshape(yp.shape)
        ys = y_s.reshape(ys.shape)
        for o, val in zip(outs, (mk.reshape(bp, N_MEM, XA_HEADS, XA_HEAD), mv.reshape(bp, N_MEM, XA_HEADS, XA_HEAD),
                                 c_p, s_p, w_p, c_s, s_s, w_s)):
            o.append(val)
    return (yp, ys) + tuple(jnp.stack(o) for o in outs)
```

```python
import functools
import math

import jax
import jax.numpy as jnp
from jax import lax
from jax.experimental import pallas as pl
from jax.experimental.pallas import tpu as pltpu

F32 = jnp.float32
BF16 = jnp.bfloat16

D_MODEL = 2048
D_CONV = 1024
D_RWKV = 1024
RWKV_HEAD = 64
RWKV_HEADS = 16
DECAY_LORA = 64
A_LORA = 64
GATE_LORA = 160
D_LORA = DECAY_LORA + A_LORA + GATE_LORA
D_SHIFT = 3 * D_RWKV + D_LORA
D_IN = 3 * D_CONV + D_SHIFT
LANE = 128
D_LORA_PAD = 384
D_SHIFT_PAD = 3 * D_RWKV + D_LORA_PAD
D_IN_PAD = 3 * D_CONV + D_SHIFT_PAD
N_MEM = 256
XA_HEADS = 4
XA_HEAD = 512
N_GROUPS = 4
EXP_PER_GROUP = 8
N_EXPERTS = 32
EXPERT_FF = 512
LN_EPS = 1e-5
GN_EPS = 64e-5
ALPHA = 2.0 ** 0.25
CHUNK = 64
ROUTE_OFF = N_GROUPS
MOE_BLOCK = 256
VMEM_LIMIT = 56 * 1024 * 1024


def _cparams(*sem):
    return pltpu.CompilerParams(dimension_semantics=sem, vmem_limit_bytes=VMEM_LIMIT)


def _dot(a, b):
    return jnp.dot(a.astype(BF16), b.astype(BF16), preferred_element_type=F32)


def _dot_nt(a, b):
    return lax.dot_general(a.astype(BF16), b.astype(BF16), (((1,), (1,)), ((), ())), preferred_element_type=F32)


def _dot_tn(a, b):
    return lax.dot_general(a.astype(BF16), b.astype(BF16), (((0,), (0,)), ((), ())), preferred_element_type=F32)


def _split3(x):
    h1 = x.astype(BF16)
    r1 = x - h1.astype(F32)
    h2 = r1.astype(BF16)
    h3 = (r1 - h2.astype(F32)).astype(BF16)
    return h1, h2, h3


def _dot_exact_rhs(x, m3):
    return jnp.dot(jnp.concatenate(_split3(x), axis=1), m3, preferred_element_type=F32)


def _layer_norm(z, g, b):
    mu = jnp.mean(z, axis=-1, keepdims=True)
    d = z - mu
    var = jnp.mean(d * d, axis=-1, keepdims=True)
    return d * lax.rsqrt(var + LN_EPS) * g + b


def _sigmoid(x):
    return 1.0 / (1.0 + jnp.exp(-x))


def _tile(n, pref):
    t = min(n, pref)
    while t > 8 and (n % t or t % 8):
        t -= 8
    return t if n % t == 0 else n


def _proj_kernel(x_ref, w_ref, o_ref, *o16_ref):
    y = _dot(x_ref[...], w_ref[...])
    o_ref[...] = y
    for r in o16_ref:
        r[...] = y.astype(BF16)


def _proj(x, w, tm, tn, bf16_twin=False):
    n, k = x.shape
    m = w.shape[1]
    out = pl.BlockSpec((tm, tn), lambda j, i: (i, j))
    res = pl.pallas_call(
        _proj_kernel,
        grid=(m // tn, n // tm),
        in_specs=[pl.BlockSpec((tm, k), lambda j, i: (i, 0)), pl.BlockSpec((k, tn), lambda j, i: (0, j))],
        out_specs=[out, out] if bf16_twin else [out],
        out_shape=[jax.ShapeDtypeStruct((n, m), F32)] + ([jax.ShapeDtypeStruct((n, m), BF16)] if bf16_twin else []),
        compiler_params=_cparams("arbitrary", "arbitrary"),
        name="proj",
    )(x, w)
    return res if bf16_twin else res[0]


def _shift_rows(x, heads):
    s = len(heads)
    out = pltpu.roll(x, s, axis=0)
    row = lax.broadcasted_iota(jnp.int32, x.shape, 0)
    for i, h in enumerate(heads):
        out = jnp.where(row == i, h, out)
    return out


def _segsum64(x, e):
    return jnp.concatenate([_dot_exact_rhs(x[:, j * LANE:(j + 1) * LANE], e) for j in range(x.shape[1] // LANE)], axis=1)


def _mixer_inputs(p_ref, cst_ref, sst_ref, pv_ref, mu_ref, wl_ref, e_ref, yc_ref, nconv_ref, nshift_ref, cu, cp,
                  r_ref, k_ref, v_ref, w_ref, kk_ref, bb_ref, g_ref):
    tt = p_ref.shape[1]

    @pl.when(pl.program_id(1) == 0)
    def _():
        cu[0:2, :] = cst_ref[0]
        cp[0:1, :] = sst_ref[0]

    bg = p_ref[0, :, 0:D_CONV]
    u = p_ref[0, :, D_CONV:2 * D_CONV] * p_ref[0, :, 2 * D_CONV:3 * D_CONV]
    u1 = _shift_rows(u, [cu[1:2, :]])
    u2 = _shift_rows(u, [cu[0:1, :], cu[1:2, :]])
    yc_ref[0] = bg * (pv_ref[4:5, :] * u2 + pv_ref[5:6, :] * u1 + pv_ref[6:7, :] * u)
    tail = u[tt - 2:tt, :]
    cu[0:2, :] = tail
    nconv_ref[0] = tail

    prw = p_ref[0, :, 3 * D_CONV:]
    prev = _shift_rows(prw, [cp[0:1, :]])
    q = prw + mu_ref[...] * (prev - prw)
    last = prw[tt - 1:tt, :]
    cp[0:1, :] = last
    nshift_ref[0] = last

    r = q[:, 0:D_RWKV]
    k = q[:, D_RWKV:2 * D_RWKV]
    v = q[:, 2 * D_RWKV:3 * D_RWKV]
    ql = q[:, 3 * D_RWKV:]
    lane = lax.broadcasted_iota(jnp.int32, ql.shape, 1)
    act = jnp.where(lane < DECAY_LORA, jnp.tanh(ql), jnp.where(lane < DECAY_LORA + A_LORA, ql, _sigmoid(ql)))
    lo = _dot(act, wl_ref[...])
    z = -(pv_ref[0:1, :] + lo[:, 0:D_RWKV])
    w_log = -(jnp.maximum(z, 0.0) + jnp.log(1.0 + jnp.exp(-jnp.abs(z)))) - 0.5
    a = _sigmoid(pv_ref[1:2, :] + lo[:, D_RWKV:2 * D_RWKV])
    kk = k * pv_ref[2:3, :]
    kk = kk * lax.rsqrt(jnp.maximum(_segsum64(kk * kk, e_ref[...]), 1e-24))
    r_ref[...] = r
    k_ref[...] = k * (1.0 + (a - 1.0) * pv_ref[3:4, :])
    v_ref[...] = v
    w_ref[...] = -jnp.exp(w_log)
    kk_ref[...] = kk
    bb_ref[...] = kk * a
    g_ref[...] = lo[:, 2 * D_RWKV:]


N_PAIRS = RWKV_HEADS // 2


def _mixer_kernel(p_ref, cst_ref, sst_ref, s0_ref, pv_ref, mu_ref, wl_ref, hv_ref, ms_ref, mi_ref, lt_ref, e_ref,
                  yc_ref, y_ref, nconv_ref, nshift_ref, sout_ref,
                  cu, cp, S, r_ref, k_ref, v_ref, w_ref, kk_ref, bb_ref, g_ref, *, C):
    _mixer_inputs(p_ref, cst_ref, sst_ref, pv_ref, mu_ref, wl_ref, e_ref, yc_ref, nconv_ref, nshift_ref, cu, cp,
                  r_ref, k_ref, v_ref, w_ref, kk_ref, bb_ref, g_ref)
    t = pl.program_id(1)
    nt = pl.num_programs(1)
    H = RWKV_HEAD
    lane = lax.broadcasted_iota(jnp.int32, (1, LANE), 1)
    m0 = lane < H

    @pl.when(t == 0)
    def _():
        z = jnp.zeros((H, H), F32)
        for hp in range(N_PAIRS):
            S[hp, 0:H, :] = jnp.concatenate([s0_ref[0, 2 * hp], z], axis=1)
            S[hp, H:2 * H, :] = jnp.concatenate([z, s0_ref[0, 2 * hp + 1]], axis=1)

    pair = lambda x: jnp.concatenate([jnp.where(m0, x, 0.0), jnp.where(m0, 0.0, x)], axis=0)

    def pair_chunk(hp, sl):
        ln = slice(hp * LANE, (hp + 1) * LANE)
        rk = hv_ref[0:1, ln]
        lng = hv_ref[1:2, ln]
        lnb = hv_ref[2:3, ln]
        r = r_ref[sl, ln]
        w = w_ref[sl, ln]
        k = k_ref[sl, ln]
        v = v_ref[sl, ln]
        kk = kk_ref[sl, ln]
        bb = bb_ref[sl, ln]
        cw = jnp.dot(lt_ref[...], jnp.concatenate(_split3(w), axis=0), preferred_element_type=F32)
        yield
        p = jnp.exp(cw)
        ip = jnp.exp(-cw)
        pc = p[C - 1:C, :]
        la = pair(-(jnp.exp(cw - w) * kk))
        lr = pair(p * r)
        rb = pair(bb * ip)
        rkk = pair(k * ip)
        vst = pair(v)
        lhs = jnp.concatenate([la, lr], axis=0)
        rhs = jnp.concatenate([rb, rkk], axis=0)
        gram = _dot_nt(lhs, rhs)
        sm = S[hp]
        ws = _dot_nt(lhs, sm)
        yield
        ms = ms_ref[...] > 0.5
        mi = mi_ref[...] > 0.5
        a_ab = jnp.where(ms, gram[:2 * C, :2 * C], 0.0)
        a_ak = jnp.where(ms, gram[:2 * C, 2 * C:], 0.0)
        a_rb = jnp.where(mi, gram[2 * C:, :2 * C], 0.0)
        a_rk = jnp.where(mi, gram[2 * C:, 2 * C:], 0.0)
        x = ws[:2 * C] + _dot(a_ak, vst)
        tm = (mi_ref[...] - ms_ref[...]) + a_ab
        pw = a_ab
        n = 1
        while 2 * n < C:
            pw = _dot(pw, pw)
            yield
            tm = tm + _dot(pw, tm)
            yield
            n *= 2
        u = _dot(tm, x)
        yield
        uv = jnp.concatenate([u, vst], axis=0)
        ost = ws[2 * C:] + _dot(jnp.concatenate([a_rb, a_rk], axis=1), uv)
        o = ost[:C] + ost[C:]
        S[hp] = sm * pc + _dot_tn(uv, rhs * pc)
        yield
        e3 = e_ref[...]
        sums = _dot_exact_rhs(jnp.concatenate([o, r * k * rk], axis=0), e3)
        yield
        d = o - sums[:C] * (1.0 / H)
        var = _dot_exact_rhs(d * d, e3) * (1.0 / H)
        yield
        on = d * lax.rsqrt(var + GN_EPS) * lng + lnb
        y_ref[0, sl, ln] = (on + sums[C:] * v) * g_ref[sl, ln]

    def chunk(c, carry):
        sl = pl.ds(pl.multiple_of(c * C, C), C)
        live = [pair_chunk(hp, sl) for hp in range(N_PAIRS)]
        while live:
            live = [g for g in live if next(g, live) is not live]
        return carry

    lax.fori_loop(0, r_ref.shape[0] // C, chunk, 0)

    @pl.when(t == nt - 1)
    def _():
        for hp in range(N_PAIRS):
            sout_ref[0, 2 * hp] = S[hp, 0:H, 0:H]
            sout_ref[0, 2 * hp + 1] = S[hp, H:2 * H, H:2 * H]


def _mixer(p, conv_st, shift_st, s0, pvec, mu, wl, hvec, consts, tt, C):
    b, t, _ = p.shape
    ms, mi, lt, e3 = consts
    row = lambda w: pl.BlockSpec((1, tt, w), lambda i, j: (i, j, 0))
    st = lambda a: pl.BlockSpec((1,) + a.shape[1:], lambda i, j: (i,) + (0,) * (a.ndim - 1))
    full = lambda a: pl.BlockSpec(a.shape, lambda i, j: (0,) * a.ndim)
    seq = jax.ShapeDtypeStruct((b, t, D_RWKV), F32)
    return pl.pallas_call(
        functools.partial(_mixer_kernel, C=C),
        grid=(b, t // tt),
        in_specs=[row(D_IN_PAD), st(conv_st), st(shift_st), st(s0), full(pvec), full(mu), full(wl), full(hvec),
                  full(ms), full(mi), full(lt), full(e3)],
        out_specs=[row(D_CONV), row(D_RWKV), st(conv_st), st(shift_st), st(s0)],
        out_shape=[seq, seq, jax.ShapeDtypeStruct(conv_st.shape, F32), jax.ShapeDtypeStruct(shift_st.shape, F32),
                   jax.ShapeDtypeStruct(s0.shape, F32)],
        scratch_shapes=[pltpu.VMEM((8, D_CONV), F32), pltpu.VMEM((8, D_SHIFT_PAD), F32),
                        pltpu.VMEM((N_PAIRS, LANE, LANE), F32)] + [pltpu.VMEM((tt, D_RWKV), F32)] * 7,
        compiler_params=_cparams("arbitrary", "arbitrary"),
        name="mixer",
    )(p, conv_st, shift_st, s0, pvec, mu, wl, hvec, ms, mi, lt, e3)


def _rwkv_consts(C):
    i = jnp.arange(2 * C)
    same = (i[:, None] // C) == (i[None, :] // C)
    ms = (same & ((i[None, :] % C) < (i[:, None] % C))).astype(F32)
    mi = (same & ((i[None, :] % C) <= (i[:, None] % C))).astype(F32)
    j = jnp.arange(C)
    lt = jnp.tile((j[None, :] <= j[:, None]).astype(BF16), (1, 3))
    l = jnp.arange(LANE)
    e3 = jnp.tile(((l[:, None] // RWKV_HEAD) == (l[None, :] // RWKV_HEAD)).astype(BF16), (3, 1))
    return ms, mi, lt, e3


def _out_ln_kernel(x_ref, yc_ref, yr_ref, w_ref, ln_ref, o_ref):
    z = ALPHA * x_ref[...] + _dot(yc_ref[...], w_ref[0:D_CONV, :]) + _dot(yr_ref[...], w_ref[D_CONV:, :])
    o_ref[...] = _layer_norm(z, ln_ref[0:1, :], ln_ref[1:2, :])


def _out_ln(x, yc, yr, w, ln, tm):
    n = x.shape[0]
    row = lambda wd: pl.BlockSpec((tm, wd), lambda i: (i, 0))
    full = lambda a: pl.BlockSpec(a.shape, lambda i: (0,) * a.ndim)
    return pl.pallas_call(
        _out_ln_kernel,
        grid=(n // tm,),
        in_specs=[row(D_MODEL), row(D_CONV), row(D_RWKV), full(w), full(ln)],
        out_specs=row(D_MODEL),
        out_shape=jax.ShapeDtypeStruct((n, D_MODEL), F32),
        compiler_params=_cparams("arbitrary"),
        name="out_proj_ln",
    )(x, yc, yr, w, ln)


ROW_TILES = D_MODEL // LANE


def _store_rowmajor(ref, val):
    for c in range(ROW_TILES):
        ref[pl.ds(c, val.shape[0], stride=ROW_TILES), :] = val[:, c * LANE:(c + 1) * LANE]


def _load_rowmajor(ref, rows):
    return jnp.concatenate([ref[pl.ds(c, rows, stride=ROW_TILES), :] for c in range(ROW_TILES)], axis=1)


def _attn_kernel(x_ref, mk_ref, mv_ref, wq_ref, wo_ref, ln_ref, *rest):
    o_ref, orm_ref = rest[-2:]
    nb, tq, _ = x_ref.shape
    x = x_ref[...].reshape(nb * tq, D_MODEL)
    q = _dot(x, wq_ref[...])
    seqs = []
    for s_i in range(nb):
        heads = []
        for h in range(XA_HEADS):
            sl = slice(h * XA_HEAD, (h + 1) * XA_HEAD)
            s = _dot_nt(q[s_i * tq:(s_i + 1) * tq, sl], mk_ref[s_i, :, sl]) * (XA_HEAD ** -0.5)
            s = s - jnp.max(s, axis=-1, keepdims=True)
            pe = jnp.exp(s)
            pr = pe / jnp.sum(pe, axis=-1, keepdims=True)
            heads.append(_dot(pr, mv_ref[s_i, :, sl]))
        seqs.append(jnp.concatenate(heads, axis=1))
    o = jnp.concatenate(seqs, axis=0)
    z = ALPHA * x + _dot(o, wo_ref[...])
    y = _layer_norm(z, ln_ref[0:1, :], ln_ref[1:2, :])
    o_ref[...] = y
    _store_rowmajor(orm_ref, y)


def _attn(x, mk, mv, wq, wo, ln, tq, n_total, row_off, prev):
    b, t, _ = x.shape
    nt = t // tq
    nb = 2 if (nt == 1 and b % 2 == 0 and tq % 8 == 0) else 1
    rows = nb * tq
    row = pl.BlockSpec((nb, tq, D_MODEL), lambda i, j: (i, j, 0))
    mem = pl.BlockSpec((nb, N_MEM, D_MODEL), lambda i, j: (i, 0, 0))
    full = lambda a: pl.BlockSpec(a.shape, lambda i, j: (0,) * a.ndim)
    out_idx = lambda i, j: (row_off // rows + i * nt + j, 0)
    args = (x, mk, mv, wq, wo, ln) + tuple(prev or ())
    return pl.pallas_call(
        _attn_kernel,
        grid=(b // nb, nt),
        in_specs=[row, mem, mem, full(wq), full(wo), full(ln)] + [pl.BlockSpec(memory_space=pl.ANY)] * (len(args) - 6),
        out_specs=[pl.BlockSpec((rows, D_MODEL), out_idx), pl.BlockSpec((rows * ROW_TILES, LANE), out_idx)],
        out_shape=[jax.ShapeDtypeStruct((n_total, D_MODEL), F32), jax.ShapeDtypeStruct((n_total * ROW_TILES, LANE), F32)],
        input_output_aliases={6: 0, 7: 1} if prev else {},
        compiler_params=_cparams("arbitrary", "arbitrary"),
        name="mem_attn_ln",
    )(*args)


def _router_kernel(x_ref, wr_ref, br_ref, lt_ref, route_ref, cnt_ref, cnt):
    i = pl.program_id(0)

    @pl.when(i == 0)
    def _():
        cnt[...] = jnp.zeros_like(cnt)

    neg = -1e30
    big = 1 << 20
    logits = _dot(x_ref[...], wr_ref[...]) + br_ref[...]
    lane = lax.broadcasted_iota(jnp.int32, logits.shape, 1)
    first_arg = lambda val, m: jnp.min(jnp.where(val == m, lane, big), axis=-1, keepdims=True)
    lg = jnp.where(lane < N_GROUPS, logits, neg)
    mg = jnp.max(lg, axis=-1, keepdims=True)
    grp = first_arg(lg, mg)
    p_grp = 1.0 / jnp.sum(jnp.exp(lg - mg), axis=-1, keepdims=True)
    lo = ROUTE_OFF + EXP_PER_GROUP * grp
    ls = jnp.where((lane >= lo) & (lane < lo + EXP_PER_GROUP), logits, neg)
    m1 = jnp.max(ls, axis=-1, keepdims=True)
    j1 = first_arg(ls, m1)
    ls2 = jnp.where(lane == j1, neg, ls)
    m2 = jnp.max(ls2, axis=-1, keepdims=True)
    j2 = first_arg(ls2, m2)
    t2 = jnp.exp(m2 - m1)
    g1 = p_grp * (1.0 / (1.0 + t2))
    g2 = p_grp * (t2 / (1.0 + t2))
    oh1 = lane == j1
    oh2 = lane == j2
    oh = jnp.where(oh1 | oh2, 1.0, 0.0)
    before = jnp.dot(lt_ref[...], oh.astype(BF16), preferred_element_type=F32) + cnt[0:1, :]
    rank1 = jnp.sum(jnp.where(oh1, before, 0.0), axis=-1, keepdims=True)
    rank2 = jnp.sum(jnp.where(oh2, before, 0.0), axis=-1, keepdims=True)
    total = cnt[0:1, :] + jnp.sum(oh, axis=0, keepdims=True)
    cnt[0:1, :] = total
    cnt_ref[...] = jnp.broadcast_to(total, cnt_ref.shape)
    e1 = (j1 - ROUTE_OFF).astype(F32)
    e2 = (j2 - ROUTE_OFF).astype(F32)
    out = jnp.zeros(logits.shape, F32)
    for col, val in enumerate((e1, e2, g1, g2, rank1, rank2)):
        out = jnp.where(lane == col, val, out)
    route_ref[...] = out


def _router(x, wr, br, tm):
    n = x.shape[0]
    j = jnp.arange(tm)
    lt = (j[None, :] < j[:, None]).astype(BF16)
    full = lambda a: pl.BlockSpec(a.shape, lambda i: (0,) * a.ndim)
    return pl.pallas_call(
        _router_kernel,
        grid=(n // tm,),
        in_specs=[pl.BlockSpec((tm, D_MODEL), lambda i: (i, 0)), full(wr), full(br), full(lt)],
        out_specs=[pl.BlockSpec((tm, LANE), lambda i: (i, 0)), pl.BlockSpec((8, LANE), lambda i: (0, 0))],
        out_shape=[jax.ShapeDtypeStruct((n, LANE), F32), jax.ShapeDtypeStruct((8, LANE), F32)],
        scratch_shapes=[pltpu.VMEM((8, LANE), F32)],
        compiler_params=_cparams("arbitrary"),
        name="router",
    )(x, wr, br, lt)


def _slotmap_kernel(dest_ref, lo_ref, hi_ref, code_ref, *, n_tok, blk):
    n_slots = code_ref.shape[0] - blk

    def spare(lo, hi):
        def body(j, c):
            code_ref[j] = 2 * n_tok + j
            return c

        lax.fori_loop(lo, hi, body, 0)

    spare(0, blk)
    for e in range(N_EXPERTS):
        spare(blk + lo_ref[e], blk + hi_ref[e])
    spare(blk + hi_ref[N_EXPERTS - 1], blk + n_slots)

    def put(a, c):
        code_ref[blk + dest_ref[a]] = a
        return c

    lax.fori_loop(0, 2 * n_tok, put, 0, unroll=16)


def _slotmap(dest, pad_lo, pad_hi, n_tok, n_slots, blk):
    smem = pl.BlockSpec(memory_space=pltpu.SMEM)
    return pl.pallas_call(
        functools.partial(_slotmap_kernel, n_tok=n_tok, blk=blk),
        in_specs=[smem, smem, smem],
        out_specs=smem,
        out_shape=jax.ShapeDtypeStruct((n_slots + blk,), jnp.int32),
        name="moe_slot_map",
    )(dest, pad_lo, pad_hi)


def _expert_kernel(code_ref, be_ref, na_ref, x_hbm, w1_ref, w3_ref, w2_ref, out_hbm,
                   xbuf, ybuf, gsem, ssem, w1s, w3s, w2s, *, blk, n_tok):
    i = pl.program_id(0)
    na = na_ref[0]
    par = i % 2
    oth = 1 - par
    rows = range(blk)

    def g_copy(p, code, r):
        tok = jnp.where(code >= 2 * n_tok, 0, lax.shift_right_logical(code, jnp.int32(1)))
        src = x_hbm.at[pl.ds(pl.multiple_of(tok * ROW_TILES, ROW_TILES), ROW_TILES)]
        return pltpu.make_async_copy(src, xbuf.at[p, pl.ds(r * ROW_TILES, ROW_TILES)], gsem.at[p])

    def s_copy(p, r, code):
        row = jnp.where(code >= 2 * n_tok, code, (code & 1) * n_tok + lax.shift_right_logical(code, jnp.int32(1)))
        dst = out_hbm.at[pl.ds(pl.multiple_of(row * ROW_TILES, ROW_TILES), ROW_TILES)]
        return pltpu.make_async_copy(ybuf.at[p, pl.ds(r * ROW_TILES, ROW_TILES)], dst, ssem.at[p])

    @pl.when(i == 0)
    def _():
        for r in rows:
            g_copy(0, code_ref[blk + r], r).start(priority=r % 2)
        ybuf[1] = jnp.zeros(ybuf.shape[1:], F32)

    @pl.when(i < na)
    def _():
        for r in rows:
            g_copy(par, 0, r).wait()

        @pl.when(i >= 1)
        def _():
            for r in rows:
                s_copy(par, r, 0).wait()

        @pl.when((i == 0) | (be_ref[i] != be_ref[jnp.maximum(i - 1, 0)]))
        def _():
            w1s[...] = w1_ref[0, 0].astype(BF16)
            w3s[...] = w3_ref[0, 0].astype(BF16)
            w2s[...] = w2_ref[0, 0].astype(BF16)

        x = _load_rowmajor(xbuf.at[par], blk).astype(BF16)
        nxt = (jnp.minimum(i + 1, na - 1) + 1) * blk
        for r in rows:
            g_copy(oth, code_ref[nxt + r], r).start(priority=r % 2)
            s_copy(oth, r, code_ref[i * blk + r]).start(priority=1 - r % 2)
        h1 = jnp.dot(x, w1s[...], preferred_element_type=F32)
        h3 = jnp.dot(x, w3s[...], preferred_element_type=F32)
        _store_rowmajor(ybuf.at[par], jnp.dot((h1 * _sigmoid(h1) * h3).astype(BF16), w2s[...], preferred_element_type=F32))

        @pl.when(i == na - 1)
        def _():
            for r in rows:
                s_copy(par, r, code_ref[(i + 1) * blk + r]).start(priority=r % 2)
            for r in rows:
                g_copy(oth, 0, r).wait()
                s_copy(oth, r, 0).wait()
                s_copy(par, r, 0).wait()


def _experts(code, blk_e, n_act, x, w1, w3, w2, l, blk):
    n = x.shape[0] // ROW_TILES
    n_slots = code.shape[0] - blk
    wspec = lambda a: pl.BlockSpec((1, 1) + a.shape[2:], lambda i, c, be, na: (l, be[jnp.minimum(i, na[0] - 1)], 0, 0))
    return pl.pallas_call(
        functools.partial(_expert_kernel, blk=blk, n_tok=n),
        grid_spec=pltpu.PrefetchScalarGridSpec(
            num_scalar_prefetch=3,
            grid=(n_slots // blk,),
            in_specs=[pl.BlockSpec(memory_space=pl.ANY), wspec(w1), wspec(w3), wspec(w2)],
            out_specs=pl.BlockSpec(memory_space=pl.ANY),
            scratch_shapes=[pltpu.VMEM((2, blk * ROW_TILES, LANE), F32), pltpu.VMEM((2, blk * ROW_TILES, LANE), F32),
                            pltpu.SemaphoreType.DMA((2,)), pltpu.SemaphoreType.DMA((2,)),
                            pltpu.VMEM((D_MODEL, EXPERT_FF), BF16), pltpu.VMEM((D_MODEL, EXPERT_FF), BF16),
                            pltpu.VMEM((EXPERT_FF, D_MODEL), BF16)],
        ),
        out_shape=jax.ShapeDtypeStruct(((2 * n + n_slots + blk) * ROW_TILES, LANE), F32),
        compiler_params=_cparams("arbitrary"),
        name="moe_experts",
    )(code, blk_e, n_act, x, w1, w3, w2)


def _combine_kernel(x_ref, route_ref, y0_ref, y1_ref, ln_ref, oa_ref, ob_ref, *, tiles_a):
    tm = x_ref.shape[0]
    y = route_ref[:, 2:3] * _load_rowmajor(y0_ref, tm) + route_ref[:, 3:4] * _load_rowmajor(y1_ref, tm)
    out = _layer_norm(ALPHA * x_ref[...] + y, ln_ref[0:1, :], ln_ref[1:2, :])
    first = pl.program_id(0) < tiles_a

    @pl.when(first)
    def _():
        oa_ref[...] = out

    @pl.when(jnp.logical_not(first))
    def _():
        ob_ref[...] = out


def _combine(x, route, ye, ln, tm, n_a):
    n = x.shape[0]
    tiles_a = n_a // tm
    row = pl.BlockSpec((tm, D_MODEL), lambda i: (i, 0))
    rowmajor = lambda off: pl.BlockSpec((tm * ROW_TILES, LANE), lambda i: (i + off, 0))
    return pl.pallas_call(
        functools.partial(_combine_kernel, tiles_a=tiles_a),
        grid=(n // tm,),
        in_specs=[row, pl.BlockSpec((tm, LANE), lambda i: (i, 0)), rowmajor(0), rowmajor(n // tm),
                  pl.BlockSpec(ln.shape, lambda i: (0, 0))],
        out_specs=[pl.BlockSpec((tm, D_MODEL), lambda i: (jnp.minimum(i, tiles_a - 1), 0)),
                   pl.BlockSpec((tm, D_MODEL), lambda i: (jnp.maximum(i - tiles_a, 0), 0))],
        out_shape=[jax.ShapeDtypeStruct((n_a, D_MODEL), F32), jax.ShapeDtypeStruct((n - n_a, D_MODEL), F32)],
        compiler_params=_cparams("arbitrary"),
        name="moe_combine_ln",
    )(x, route, ye, ye, ln)


def _hier_moe_ln(x, x_rowmajor, wts, blk, n_a):
    n = x.shape[0]
    route, cnt = _router(x, wts["wr"], wts["br"], _tile(n, 512))
    counts = cnt[0, ROUTE_OFF:ROUTE_OFF + N_EXPERTS].astype(jnp.int32)
    pcounts = (counts + blk - 1) // blk * blk
    pend = jnp.cumsum(pcounts)
    pstart = pend - pcounts
    eidx = route[:, 0:2].astype(jnp.int32)
    dest = (pstart[eidx] + route[:, 4:6].astype(jnp.int32)).reshape(-1)
    n_slots = -(-(2 * n + N_EXPERTS * (blk - 1)) // blk) * blk
    n_blk = n_slots // blk
    blk_start = jnp.arange(n_blk, dtype=jnp.int32) * blk
    blk_e = jnp.minimum(jnp.sum(pend[None, :] <= blk_start[:, None], axis=1), N_EXPERTS - 1).astype(jnp.int32)
    n_act = (pend[-1:] // blk).astype(jnp.int32)
    code = _slotmap(dest, pstart + counts, pend, n, n_slots, blk)
    ye = _experts(code, blk_e, n_act, x_rowmajor, wts["w1"], wts["w3"], wts["w2"], wts["l"], blk)
    return _combine(x, route, ye, wts["ln3"], _tile(math.gcd(n_a, n - n_a), 256), n_a)


def _mixer_attn(x, mk, mv, conv_st, shift_st, wkv, wts, n_total, row_off, prev):
    b, t, _ = x.shape
    n = b * t
    x2 = x.reshape(n, D_MODEL)
    p = _proj(x2, wts["w_in"], _tile(n, 512), D_IN_PAD // 3).reshape(b, t, D_IN_PAD)
    shift_pad = jnp.pad(shift_st, ((0, 0), (0, 0), (0, D_SHIFT_PAD - D_SHIFT)))
    C = min(CHUNK, t)
    consts = _rwkv_consts(C)
    yc, yr, nconv, nshift, nwkv = _mixer(p, conv_st, shift_pad, wkv, wts["pvec"], wts["mu"], wts["wl"], wts["hvec"], consts,
                                         _tile(t, 256), C)
    tm = _tile(n, 512)
    x1 = _out_ln(x2, yc.reshape(n, D_CONV), yr.reshape(n, D_RWKV), wts["w_out"], wts["ln1"], tm)
    bufs = _attn(x1.reshape(b, t, D_MODEL), mk, mv, wts["xa_q"], wts["xa_o"], wts["ln2"], _tile(t, 256), n_total, row_off, prev)
    return bufs, nconv, nshift[:, :, :D_SHIFT], nwkv


def _ln_rows(g, b):
    return jnp.zeros((8, D_MODEL), F32).at[0].set(g).at[1].set(b)


def kernel(x_prompt, x_sample, mem_prompt, cache_mem_k, cache_mem_v, state_conv, state_shift, state_wkv, w_in, conv_w, shift_mu, decay_w0, decay_w2, iclr_a0, iclr_a2, gate_g2, k_k, k_a, r_k, lnx_g, lnx_b, w_out, ln1_g, ln1_b, xa_q, xa_k, xa_v, xa_o, ln2_g, ln2_b, router_grp, router_grp_b, router_sub, router_sub_b, moe_w1, moe_w3, moe_w2, ln3_g, ln3_b):
    depth = w_in.shape[0]
    bp = x_prompt.shape[0]
    bs = x_sample.shape[0]
    yp, ys = x_prompt, x_sample
    outs = [[] for _ in range(8)]
    for l in range(depth):
        wl = jnp.zeros((D_LORA_PAD, 3 * D_RWKV), F32)
        wl = wl.at[0:DECAY_LORA, 0:D_RWKV].set(decay_w2[l])
        wl = wl.at[DECAY_LORA:DECAY_LORA + A_LORA, D_RWKV:2 * D_RWKV].set(iclr_a2[l])
        wl = wl.at[DECAY_LORA + A_LORA:D_LORA, 2 * D_RWKV:].set(gate_g2[l])
        wr = jnp.zeros((D_MODEL, LANE), F32)
        wr = wr.at[:, 0:N_GROUPS].set(router_grp[l])
        wr = wr.at[:, ROUTE_OFF:ROUTE_OFF + N_EXPERTS].set(jnp.transpose(router_sub[l], (1, 0, 2)).reshape(D_MODEL, N_EXPERTS))
        br = jnp.zeros((1, LANE), F32)
        br = br.at[0, 0:N_GROUPS].set(router_grp_b[l]).at[0, ROUTE_OFF:ROUTE_OFF + N_EXPERTS].set(router_sub_b[l].reshape(-1))
        wts = dict(
            w_in=jnp.pad(w_in[l], ((0, 0), (0, D_IN_PAD - D_IN))).astype(BF16),
            pvec=jnp.zeros((8, D_RWKV), F32).at[0].set(decay_w0[l]).at[1].set(iclr_a0[l]).at[2].set(k_k[l]).at[3].set(k_a[l]).at[4:7].set(conv_w[l]),
            mu=jnp.pad(shift_mu[l], (0, D_SHIFT_PAD - D_SHIFT)).reshape(1, D_SHIFT_PAD),
            wl=wl.astype(BF16),
            hvec=jnp.zeros((8, D_RWKV), F32).at[0].set(r_k[l].reshape(-1)).at[1].set(lnx_g[l]).at[2].set(lnx_b[l]),
            w_out=w_out[l].astype(BF16), ln1=_ln_rows(ln1_g[l], ln1_b[l]),
            xa_q=xa_q[l].astype(BF16), xa_o=xa_o[l].astype(BF16), ln2=_ln_rows(ln2_g[l], ln2_b[l]),
            wr=wr.astype(BF16), br=br,
            w1=moe_w1, w3=moe_w3, w2=moe_w2, l=l, ln3=_ln_rows(ln3_g[l], ln3_b[l]),
        )
        mem2 = mem_prompt.reshape(bp * N_MEM, D_MODEL)
        mk, mk16 = (a.reshape(bp, N_MEM, D_MODEL) for a in _proj(mem2, xa_k[l].astype(BF16), _tile(bp * N_MEM, 256), 1024, True))
        mv, mv16 = (a.reshape(bp, N_MEM, D_MODEL) for a in _proj(mem2, xa_v[l].astype(BF16), _tile(bp * N_MEM, 256), 1024, True))
        n_p = yp.shape[0] * yp.shape[1]
        n_s = ys.shape[0] * ys.shape[1]
        bufs, c_p, s_p, w_p = _mixer_attn(yp, mk16, mv16, jnp.zeros((bp, 2, D_CONV), F32), jnp.zeros((bp, 1, D_SHIFT), F32),
                                          jnp.zeros((bp, RWKV_HEADS, RWKV_HEAD, RWKV_HEAD), F32), wts, n_p + n_s, 0, None)
        bufs, c_s, s_s, w_s = _mixer_attn(ys, cache_mem_k[l].reshape(bs, N_MEM, D_MODEL), cache_mem_v[l].reshape(bs, N_MEM, D_MODEL),
                                          state_conv[l], state_shift[l], state_wkv[l], wts, n_p + n_s, n_p, bufs)
        y_p, y_s = _hier_moe_ln(bufs[0], bufs[1], wts, MOE_BLOCK, n_p)
        yp = y_p.reshape(yp.shape)
        ys = y_s.reshape(ys.shape)
        for o, val in zip(outs, (mk.reshape(bp, N_MEM, XA_HEADS, XA_HEAD), mv.reshape(bp, N_MEM, XA_HEADS, XA_HEAD),
                                 c_p, s_p, w_p, c_s, s_s, w_s)):
            o.append(val)
    return (yp, ys) + tuple(jnp.stack(o) for o in outs)
```

```python
import functools
import math

import jax
import jax.numpy as jnp
from jax import lax
from jax.experimental import pallas as pl
from jax.experimental.pallas import tpu as pltpu

F32 = jnp.float32
BF16 = jnp.bfloat16

D_MODEL = 2048
D_CONV = 1024
D_RWKV = 1024
RWKV_HEAD = 64
RWKV_HEADS = 16
DECAY_LORA = 64
A_LORA = 64
GATE_LORA = 160
D_LORA = DECAY_LORA + A_LORA + GATE_LORA
D_SHIFT = 3 * D_RWKV + D_LORA
D_IN = 3 * D_CONV + D_SHIFT
LANE = 128
D_LORA_PAD = 384
D_SHIFT_PAD = 3 * D_RWKV + D_LORA_PAD
D_IN_PAD = 3 * D_CONV + D_SHIFT_PAD
N_MEM = 256
XA_HEADS = 4
XA_HEAD = 512
N_GROUPS = 4
EXP_PER_GROUP = 8
N_EXPERTS = 32
EXPERT_FF = 512
LN_EPS = 1e-5
GN_EPS = 64e-5
ALPHA = 2.0 ** 0.25
CHUNK = 64
ROUTE_OFF = N_GROUPS
MOE_BLOCK = 256
VMEM_LIMIT = 56 * 1024 * 1024


def _cparams(*sem):
    return pltpu.CompilerParams(dimension_semantics=sem, vmem_limit_bytes=VMEM_LIMIT)


def _dot(a, b):
    return jnp.dot(a.astype(BF16), b.astype(BF16), preferred_element_type=F32)


def _dot_nt(a, b):
    return lax.dot_general(a.astype(BF16), b.astype(BF16), (((1,), (1,)), ((), ())), preferred_element_type=F32)


def _dot_tn(a, b):
    return lax.dot_general(a.astype(BF16), b.astype(BF16), (((0,), (0,)), ((), ())), preferred_element_type=F32)


def _split3(x):
    h1 = x.astype(BF16)
    r1 = x - h1.astype(F32)
    h2 = r1.astype(BF16)
    h3 = (r1 - h2.astype(F32)).astype(BF16)
    return h1, h2, h3


def _dot_exact_rhs(x, m3):
    return jnp.dot(jnp.concatenate(_split3(x), axis=1), m3, preferred_element_type=F32)


def _layer_norm(z, g, b):
    mu = jnp.mean(z, axis=-1, keepdims=True)
    d = z - mu
    var = jnp.mean(d * d, axis=-1, keepdims=True)
    return d * lax.rsqrt(var + LN_EPS) * g + b


def _sigmoid(x):
    return 1.0 / (1.0 + jnp.exp(-x))


def _tile(n, pref):
    t = min(n, pref)
    while t > 8 and (n % t or t % 8):
        t -= 8
    return t if n % t == 0 else n


def _proj_kernel(x_ref, w_ref, o_ref, *o16_ref):
    y = _dot(x_ref[...], w_ref[...])
    o_ref[...] = y
    for r in o16_ref:
        r[...] = y.astype(BF16)


def _proj(x, w, tm, tn, bf16_twin=False):
    n, k = x.shape
    m = w.shape[1]
    out = pl.BlockSpec((tm, tn), lambda j, i: (i, j))
    res = pl.pallas_call(
        _proj_kernel,
        grid=(m // tn, n // tm),
        in_specs=[pl.BlockSpec((tm, k), lambda j, i: (i, 0)), pl.BlockSpec((k, tn), lambda j, i: (0, j))],
        out_specs=[out, out] if bf16_twin else [out],
        out_shape=[jax.ShapeDtypeStruct((n, m), F32)] + ([jax.ShapeDtypeStruct((n, m), BF16)] if bf16_twin else []),
        compiler_params=_cparams("arbitrary", "arbitrary"),
        name="proj",
    )(x, w)
    return res if bf16_twin else res[0]


def _shift_rows(x, heads):
    s = len(heads)
    out = pltpu.roll(x, s, axis=0)
    row = lax.broadcasted_iota(jnp.int32, x.shape, 0)
    for i, h in enumerate(heads):
        out = jnp.where(row == i, h, out)
    return out


def _segsum64(x, e):
    return jnp.concatenate([_dot_exact_rhs(x[:, j * LANE:(j + 1) * LANE], e) for j in range(x.shape[1] // LANE)], axis=1)


def _mixer_inputs(rows, p_ref, pv_ref, mu_ref, wl_ref, e_ref, yc_ref, nconv_ref, nshift_ref, cu, cp,
                  r_ref, k_ref, v_ref, w_ref, kk_ref, bb_ref, g_ref):
    nr = rows.stop - rows.start
    bg = p_ref[0, rows, 0:D_CONV]
    u = p_ref[0, rows, D_CONV:2 * D_CONV] * p_ref[0, rows, 2 * D_CONV:3 * D_CONV]
    u1 = _shift_rows(u, [cu[1:2, :]])
    u2 = _shift_rows(u, [cu[0:1, :], cu[1:2, :]])
    yc_ref[0, rows] = bg * (pv_ref[4:5, :] * u2 + pv_ref[5:6, :] * u1 + pv_ref[6:7, :] * u)
    tail = u[nr - 2:nr, :]
    cu[0:2, :] = tail
    nconv_ref[0] = tail
    yield

    def shifted(c0, c1):
        prw = p_ref[0, rows, 3 * D_CONV + c0:3 * D_CONV + c1]
        prev = _shift_rows(prw, [cp[0:1, c0:c1]])
        return prw + mu_ref[:, c0:c1] * (prev - prw)

    r = shifted(0, D_RWKV)
    r_ref[rows] = r
    yield
    k = shifted(D_RWKV, 2 * D_RWKV)
    yield
    v_ref[rows] = shifted(2 * D_RWKV, 3 * D_RWKV)
    yield
    ql = shifted(3 * D_RWKV, D_SHIFT_PAD)
    last = p_ref[0, rows.stop - 1:rows.stop, 3 * D_CONV:]
    cp[0:1, :] = last
    nshift_ref[0] = last
    lane = lax.broadcasted_iota(jnp.int32, ql.shape, 1)
    act = jnp.where(lane < DECAY_LORA, jnp.tanh(ql), jnp.where(lane < DECAY_LORA + A_LORA, ql, _sigmoid(ql)))
    lo = _dot(act, wl_ref[...])
    g_ref[rows] = lo[:, 2 * D_RWKV:]
    yield
    z = -(pv_ref[0:1, :] + lo[:, 0:D_RWKV])
    w_log = -(jnp.maximum(z, 0.0) + jnp.log(1.0 + jnp.exp(-jnp.abs(z)))) - 0.5
    w_ref[rows] = -jnp.exp(w_log)
    yield
    a = _sigmoid(pv_ref[1:2, :] + lo[:, D_RWKV:2 * D_RWKV])
    k_ref[rows] = k * (1.0 + (a - 1.0) * pv_ref[3:4, :])
    yield
    kk = k * pv_ref[2:3, :]
    kk = kk * lax.rsqrt(jnp.maximum(_segsum64(kk * kk, e_ref[...]), 1e-24))
    kk_ref[rows] = kk
    bb_ref[rows] = kk * a


N_PAIRS = RWKV_HEADS // 2


def _mixer_kernel(p_ref, cst_ref, sst_ref, s0_ref, pv_ref, mu_ref, wl_ref, hv_ref, ms_ref, mi_ref, lt_ref, e_ref,
                  yc_ref, y_ref, nconv_ref, nshift_ref, sout_ref,
                  cu, cp, S, r_ref, k_ref, v_ref, w_ref, kk_ref, bb_ref, g_ref, *, C):
    t = pl.program_id(1)
    nt = pl.num_programs(1)
    H = RWKV_HEAD
    lane = lax.broadcasted_iota(jnp.int32, (1, LANE), 1)
    m0 = lane < H

    @pl.when(t == 0)
    def _():
        cu[0:2, :] = cst_ref[0]
        cp[0:1, :] = sst_ref[0]
        z = jnp.zeros((H, H), F32)
        for hp in range(N_PAIRS):
            S[hp, 0:H, :] = jnp.concatenate([s0_ref[0, 2 * hp], z], axis=1)
            S[hp, H:2 * H, :] = jnp.concatenate([z, s0_ref[0, 2 * hp + 1]], axis=1)

    pair = lambda x: jnp.concatenate([jnp.where(m0, x, 0.0), jnp.where(m0, 0.0, x)], axis=0)

    def pair_chunk(hp, sl):
        ln = slice(hp * LANE, (hp + 1) * LANE)
        rk = hv_ref[0:1, ln]
        lng = hv_ref[1:2, ln]
        lnb = hv_ref[2:3, ln]
        r = r_ref[sl, ln]
        w = w_ref[sl, ln]
        k = k_ref[sl, ln]
        v = v_ref[sl, ln]
        kk = kk_ref[sl, ln]
        bb = bb_ref[sl, ln]
        cw = jnp.dot(lt_ref[...], jnp.concatenate(_split3(w), axis=0), preferred_element_type=F32)
        yield
        p = jnp.exp(cw)
        ip = jnp.exp(-cw)
        pc = p[C - 1:C, :]
        la = pair(-(jnp.exp(cw - w) * kk))
        lr = pair(p * r)
        rb = pair(bb * ip)
        rkk = pair(k * ip)
        vst = pair(v)
        lhs = jnp.concatenate([la, lr], axis=0)
        rhs = jnp.concatenate([rb, rkk], axis=0)
        gram = _dot_nt(lhs, rhs)
        sm = S[hp]
        ws = _dot_nt(lhs, sm)
        yield
        ms = ms_ref[...] > 0.5
        mi = mi_ref[...] > 0.5
        a_ab = jnp.where(ms, gram[:2 * C, :2 * C], 0.0)
        a_ak = jnp.where(ms, gram[:2 * C, 2 * C:], 0.0)
        a_rb = jnp.where(mi, gram[2 * C:, :2 * C], 0.0)
        a_rk = jnp.where(mi, gram[2 * C:, 2 * C:], 0.0)
        x = ws[:2 * C] + _dot(a_ak, vst)
        tm = (mi_ref[...] - ms_ref[...]) + a_ab
        pw = a_ab
        n = 1
        while 2 * n < C:
            pw = _dot(pw, pw)
            yield
            tm = tm + _dot(pw, tm)
            yield
            n *= 2
        u = _dot(tm, x)
        yield
        uv = jnp.concatenate([u, vst], axis=0)
        ost = ws[2 * C:] + _dot(jnp.concatenate([a_rb, a_rk], axis=1), uv)
        o = ost[:C] + ost[C:]
        S[hp] = sm * pc + _dot_tn(uv, rhs * pc)
        yield
        e3 = e_ref[...]
        sums = _dot_exact_rhs(jnp.concatenate([o, r * k * rk], axis=0), e3)
        yield
        d = o - sums[:C] * (1.0 / H)
        var = _dot_exact_rhs(d * d, e3) * (1.0 / H)
        yield
        on = d * lax.rsqrt(var + GN_EPS) * lng + lnb
        y_ref[0, sl, ln] = (on + sums[C:] * v) * g_ref[sl, ln]

    def inputs(rows):
        return _mixer_inputs(rows, p_ref, pv_ref, mu_ref, wl_ref, e_ref, yc_ref, nconv_ref, nshift_ref, cu, cp,
                             r_ref, k_ref, v_ref, w_ref, kk_ref, bb_ref, g_ref)

    chunks = [slice(c * C, (c + 1) * C) for c in range(r_ref.shape[0] // C)]
    for _ in inputs(chunks[0]):
        pass
    for c, rows in enumerate(chunks):
        live = [pair_chunk(hp, rows) for hp in range(N_PAIRS)]
        if c + 1 < len(chunks):
            live.append(inputs(chunks[c + 1]))
        while live:
            live = [g for g in live if next(g, live) is not live]

    @pl.when(t == nt - 1)
    def _():
        for hp in range(N_PAIRS):
            sout_ref[0, 2 * hp] = S[hp, 0:H, 0:H]
            sout_ref[0, 2 * hp + 1] = S[hp, H:2 * H, H:2 * H]


def _mixer(p, conv_st, shift_st, s0, pvec, mu, wl, hvec, consts, tt, C):
    b, t, _ = p.shape
    ms, mi, lt, e3 = consts
    row = lambda w: pl.BlockSpec((1, tt, w), lambda i, j: (i, j, 0))
    st = lambda a: pl.BlockSpec((1,) + a.shape[1:], lambda i, j: (i,) + (0,) * (a.ndim - 1))
    full = lambda a: pl.BlockSpec(a.shape, lambda i, j: (0,) * a.ndim)
    seq = jax.ShapeDtypeStruct((b, t, D_RWKV), F32)
    return pl.pallas_call(
        functools.partial(_mixer_kernel, C=C),
        grid=(b, t // tt),
        in_specs=[row(D_IN_PAD), st(conv_st), st(shift_st), st(s0), full(pvec), full(mu), full(wl), full(hvec),
                  full(ms), full(mi), full(lt), full(e3)],
        out_specs=[row(D_CONV), row(D_RWKV), st(conv_st), st(shift_st), st(s0)],
        out_shape=[seq, seq, jax.ShapeDtypeStruct(conv_st.shape, F32), jax.ShapeDtypeStruct(shift_st.shape, F32),
                   jax.ShapeDtypeStruct(s0.shape, F32)],
        scratch_shapes=[pltpu.VMEM((8, D_CONV), F32), pltpu.VMEM((8, D_SHIFT_PAD), F32),
                        pltpu.VMEM((N_PAIRS, LANE, LANE), F32)] + [pltpu.VMEM((tt, D_RWKV), F32)] * 7,
        compiler_params=_cparams("arbitrary", "arbitrary"),
        name="mixer",
    )(p, conv_st, shift_st, s0, pvec, mu, wl, hvec, ms, mi, lt, e3)


def _rwkv_consts(C):
    i = jnp.arange(2 * C)
    same = (i[:, None] // C) == (i[None, :] // C)
    ms = (same & ((i[None, :] % C) < (i[:, None] % C))).astype(F32)
    mi = (same & ((i[None, :] % C) <= (i[:, None] % C))).astype(F32)
    j = jnp.arange(C)
    lt = jnp.tile((j[None, :] <= j[:, None]).astype(BF16), (1, 3))
    l = jnp.arange(LANE)
    e3 = jnp.tile(((l[:, None] // RWKV_HEAD) == (l[None, :] // RWKV_HEAD)).astype(BF16), (3, 1))
    return ms, mi, lt, e3


def _out_ln_kernel(x_ref, yc_ref, yr_ref, w_ref, ln_ref, o_ref):
    z = ALPHA * x_ref[...] + _dot(yc_ref[...], w_ref[0:D_CONV, :]) + _dot(yr_ref[...], w_ref[D_CONV:, :])
    o_ref[...] = _layer_norm(z, ln_ref[0:1, :], ln_ref[1:2, :])


def _out_ln(x, yc, yr, w, ln, tm):
    n = x.shape[0]
    row = lambda wd: pl.BlockSpec((tm, wd), lambda i: (i, 0))
    full = lambda a: pl.BlockSpec(a.shape, lambda i: (0,) * a.ndim)
    return pl.pallas_call(
        _out_ln_kernel,
        grid=(n // tm,),
        in_specs=[row(D_MODEL), row(D_CONV), row(D_RWKV), full(w), full(ln)],
        out_specs=row(D_MODEL),
        out_shape=jax.ShapeDtypeStruct((n, D_MODEL), F32),
        compiler_params=_cparams("arbitrary"),
        name="out_proj_ln",
    )(x, yc, yr, w, ln)


ROW_TILES = D_MODEL // LANE


def _store_rowmajor(ref, val):
    for c in range(ROW_TILES):
        ref[pl.ds(c, val.shape[0], stride=ROW_TILES), :] = val[:, c * LANE:(c + 1) * LANE]


def _load_rowmajor(ref, rows):
    return jnp.concatenate([ref[pl.ds(c, rows, stride=ROW_TILES), :] for c in range(ROW_TILES)], axis=1)


def _attn_kernel(x_ref, mk_ref, mv_ref, wq_ref, wo_ref, ln_ref, *rest):
    o_ref, orm_ref = rest[-2:]
    nb, tq, _ = x_ref.shape
    x = x_ref[...].reshape(nb * tq, D_MODEL)
    q = _dot(x, wq_ref[...])
    seqs = []
    for s_i in range(nb):
        heads = []
        for h in range(XA_HEADS):
            sl = slice(h * XA_HEAD, (h + 1) * XA_HEAD)
            s = _dot_nt(q[s_i * tq:(s_i + 1) * tq, sl], mk_ref[s_i, :, sl]) * (XA_HEAD ** -0.5)
            s = s - jnp.max(s, axis=-1, keepdims=True)
            pe = jnp.exp(s)
            pr = pe / jnp.sum(pe, axis=-1, keepdims=True)
            heads.append(_dot(pr, mv_ref[s_i, :, sl]))
        seqs.append(jnp.concatenate(heads, axis=1))
    o = jnp.concatenate(seqs, axis=0)
    z = ALPHA * x + _dot(o, wo_ref[...])
    y = _layer_norm(z, ln_ref[0:1, :], ln_ref[1:2, :])
    o_ref[...] = y
    _store_rowmajor(orm_ref, y)


def _attn(x, mk, mv, wq, wo, ln, tq, n_total, row_off, prev):
    b, t, _ = x.shape
    nt = t // tq
    nb = 2 if (nt == 1 and b % 2 == 0 and tq % 8 == 0) else 1
    rows = nb * tq
    row = pl.BlockSpec((nb, tq, D_MODEL), lambda i, j: (i, j, 0))
    mem = pl.BlockSpec((nb, N_MEM, D_MODEL), lambda i, j: (i, 0, 0))
    full = lambda a: pl.BlockSpec(a.shape, lambda i, j: (0,) * a.ndim)
    out_idx = lambda i, j: (row_off // rows + i * nt + j, 0)
    args = (x, mk, mv, wq, wo, ln) + tuple(prev or ())
    return pl.pallas_call(
        _attn_kernel,
        grid=(b // nb, nt),
        in_specs=[row, mem, mem, full(wq), full(wo), full(ln)] + [pl.BlockSpec(memory_space=pl.ANY)] * (len(args) - 6),
        out_specs=[pl.BlockSpec((rows, D_MODEL), out_idx), pl.BlockSpec((rows * ROW_TILES, LANE), out_idx)],
        out_shape=[jax.ShapeDtypeStruct((n_total, D_MODEL), F32), jax.ShapeDtypeStruct((n_total * ROW_TILES, LANE), F32)],
        input_output_aliases={6: 0, 7: 1} if prev else {},
        compiler_params=_cparams("arbitrary", "arbitrary"),
        name="mem_attn_ln",
    )(*args)


def _router_kernel(x_ref, wr_ref, br_ref, lt_ref, route_ref, cnt_ref, cnt):
    i = pl.program_id(0)

    @pl.when(i == 0)
    def _():
        cnt[...] = jnp.zeros_like(cnt)

    neg = -1e30
    big = 1 << 20
    logits = _dot(x_ref[...], wr_ref[...]) + br_ref[...]
    lane = lax.broadcasted_iota(jnp.int32, logits.shape, 1)
    first_arg = lambda val, m: jnp.min(jnp.where(val == m, lane, big), axis=-1, keepdims=True)
    lg = jnp.where(lane < N_GROUPS, logits, neg)
    mg = jnp.max(lg, axis=-1, keepdims=True)
    grp = first_arg(lg, mg)
    p_grp = 1.0 / jnp.sum(jnp.exp(lg - mg), axis=-1, keepdims=True)
    lo = ROUTE_OFF + EXP_PER_GROUP * grp
    ls = jnp.where((lane >= lo) & (lane < lo + EXP_PER_GROUP), logits, neg)
    m1 = jnp.max(ls, axis=-1, keepdims=True)
    j1 = first_arg(ls, m1)
    ls2 = jnp.where(lane == j1, neg, ls)
    m2 = jnp.max(ls2, axis=-1, keepdims=True)
    j2 = first_arg(ls2, m2)
    t2 = jnp.exp(m2 - m1)
    g1 = p_grp * (1.0 / (1.0 + t2))
    g2 = p_grp * (t2 / (1.0 + t2))
    oh1 = lane == j1
    oh2 = lane == j2
    oh = jnp.where(oh1 | oh2, 1.0, 0.0)
    before = jnp.dot(lt_ref[...], oh.astype(BF16), preferred_element_type=F32) + cnt[0:1, :]
    rank1 = jnp.sum(jnp.where(oh1, before, 0.0), axis=-1, keepdims=True)
    rank2 = jnp.sum(jnp.where(oh2, before, 0.0), axis=-1, keepdims=True)
    total = cnt[0:1, :] + jnp.sum(oh, axis=0, keepdims=True)
    cnt[0:1, :] = total
    cnt_ref[...] = jnp.broadcast_to(total, cnt_ref.shape)
    e1 = (j1 - ROUTE_OFF).astype(F32)
    e2 = (j2 - ROUTE_OFF).astype(F32)
    out = jnp.zeros(logits.shape, F32)
    for col, val in enumerate((e1, e2, g1, g2, rank1, rank2)):
        out = jnp.where(lane == col, val, out)
    route_ref[...] = out


def _router(x, wr, br, tm):
    n = x.shape[0]
    j = jnp.arange(tm)
    lt = (j[None, :] < j[:, None]).astype(BF16)
    full = lambda a: pl.BlockSpec(a.shape, lambda i: (0,) * a.ndim)
    return pl.pallas_call(
        _router_kernel,
        grid=(n // tm,),
        in_specs=[pl.BlockSpec((tm, D_MODEL), lambda i: (i, 0)), full(wr), full(br), full(lt)],
        out_specs=[pl.BlockSpec((tm, LANE), lambda i: (i, 0)), pl.BlockSpec((8, LANE), lambda i: (0, 0))],
        out_shape=[jax.ShapeDtypeStruct((n, LANE), F32), jax.ShapeDtypeStruct((8, LANE), F32)],
        scratch_shapes=[pltpu.VMEM((8, LANE), F32)],
        compiler_params=_cparams("arbitrary"),
        name="router",
    )(x, wr, br, lt)


def _slotmap_kernel(dest_ref, lo_ref, hi_ref, code_ref, *, n_tok, blk):
    n_slots = code_ref.shape[0] - blk

    def spare(lo, hi):
        def body(j, c):
            code_ref[j] = 2 * n_tok + j
            return c

        lax.fori_loop(lo, hi, body, 0)

    spare(0, blk)
    for e in range(N_EXPERTS):
        spare(blk + lo_ref[e], blk + hi_ref[e])
    spare(blk + hi_ref[N_EXPERTS - 1], blk + n_slots)

    def put(a, c):
        code_ref[blk + dest_ref[a]] = a
        return c

    lax.fori_loop(0, 2 * n_tok, put, 0, unroll=16)


def _slotmap(dest, pad_lo, pad_hi, n_tok, n_slots, blk):
    smem = pl.BlockSpec(memory_space=pltpu.SMEM)
    return pl.pallas_call(
        functools.partial(_slotmap_kernel, n_tok=n_tok, blk=blk),
        in_specs=[smem, smem, smem],
        out_specs=smem,
        out_shape=jax.ShapeDtypeStruct((n_slots + blk,), jnp.int32),
        name="moe_slot_map",
    )(dest, pad_lo, pad_hi)


def _expert_kernel(code_ref, be_ref, na_ref, x_hbm, w1_ref, w3_ref, w2_ref, out_hbm,
                   xbuf, ybuf, gsem, ssem, w1s, w3s, w2s, *, blk, n_tok):
    i = pl.program_id(0)
    na = na_ref[0]
    par = i % 2
    oth = 1 - par
    rows = range(blk)

    def g_copy(p, code, r):
        tok = jnp.where(code >= 2 * n_tok, 0, lax.shift_right_logical(code, jnp.int32(1)))
        src = x_hbm.at[pl.ds(pl.multiple_of(tok * ROW_TILES, ROW_TILES), ROW_TILES)]
        return pltpu.make_async_copy(src, xbuf.at[p, pl.ds(r * ROW_TILES, ROW_TILES)], gsem.at[p])

    def s_copy(p, r, code):
        row = jnp.where(code >= 2 * n_tok, code, (code & 1) * n_tok + lax.shift_right_logical(code, jnp.int32(1)))
        dst = out_hbm.at[pl.ds(pl.multiple_of(row * ROW_TILES, ROW_TILES), ROW_TILES)]
        return pltpu.make_async_copy(ybuf.at[p, pl.ds(r * ROW_TILES, ROW_TILES)], dst, ssem.at[p])

    @pl.when(i == 0)
    def _():
        for r in rows:
            g_copy(0, code_ref[blk + r], r).start(priority=r % 2)
        ybuf[1] = jnp.zeros(ybuf.shape[1:], F32)

    @pl.when(i < na)
    def _():
        for r in rows:
            g_copy(par, 0, r).wait()

        @pl.when(i >= 1)
        def _():
            for r in rows:
                s_copy(par, r, 0).wait()

        @pl.when((i == 0) | (be_ref[i] != be_ref[jnp.maximum(i - 1, 0)]))
        def _():
            w1s[...] = w1_ref[0, 0].astype(BF16)
            w3s[...] = w3_ref[0, 0].astype(BF16)
            w2s[...] = w2_ref[0, 0].astype(BF16)

        x = _load_rowmajor(xbuf.at[par], blk).astype(BF16)
        nxt = (jnp.minimum(i + 1, na - 1) + 1) * blk
        for r in rows:
            g_copy(oth, code_ref[nxt + r], r).start(priority=r % 2)
            s_copy(oth, r, code_ref[i * blk + r]).start(priority=1 - r % 2)
        h1 = jnp.dot(x, w1s[...], preferred_element_type=F32)
        h3 = jnp.dot(x, w3s[...], preferred_element_type=F32)
        _store_rowmajor(ybuf.at[par], jnp.dot((h1 * _sigmoid(h1) * h3).astype(BF16), w2s[...], preferred_element_type=F32))

        @pl.when(i == na - 1)
        def _():
            for r in rows:
                s_copy(par, r, code_ref[(i + 1) * blk + r]).start(priority=r % 2)
            for r in rows:
                g_copy(oth, 0, r).wait()
                s_copy(oth, r, 0).wait()
                s_copy(par, r, 0).wait()


def _experts(code, blk_e, n_act, x, w1, w3, w2, l, blk):
    n = x.shape[0] // ROW_TILES
    n_slots = code.shape[0] - blk
    wspec = lambda a: pl.BlockSpec((1, 1) + a.shape[2:], lambda i, c, be, na: (l, be[jnp.minimum(i, na[0] - 1)], 0, 0))
    return pl.pallas_call(
        functools.partial(_expert_kernel, blk=blk, n_tok=n),
        grid_spec=pltpu.PrefetchScalarGridSpec(
            num_scalar_prefetch=3,
            grid=(n_slots // blk,),
            in_specs=[pl.BlockSpec(memory_space=pl.ANY), wspec(w1), wspec(w3), wspec(w2)],
            out_specs=pl.BlockSpec(memory_space=pl.ANY),
            scratch_shapes=[pltpu.VMEM((2, blk * ROW_TILES, LANE), F32), pltpu.VMEM((2, blk * ROW_TILES, LANE), F32),
                            pltpu.SemaphoreType.DMA((2,)), pltpu.SemaphoreType.DMA((2,)),
                            pltpu.VMEM((D_MODEL, EXPERT_FF), BF16), pltpu.VMEM((D_MODEL, EXPERT_FF), BF16),
                            pltpu.VMEM((EXPERT_FF, D_MODEL), BF16)],
        ),
        out_shape=jax.ShapeDtypeStruct(((2 * n + n_slots + blk) * ROW_TILES, LANE), F32),
        compiler_params=_cparams("arbitrary"),
        name="moe_experts",
    )(code, blk_e, n_act, x, w1, w3, w2)


def _combine_kernel(x_ref, route_ref, y0_ref, y1_ref, ln_ref, oa_ref, ob_ref, *, tiles_a):
    tm = x_ref.shape[0]
    y = route_ref[:, 2:3] * _load_rowmajor(y0_ref, tm) + route_ref[:, 3:4] * _load_rowmajor(y1_ref, tm)
    out = _layer_norm(ALPHA * x_ref[...] + y, ln_ref[0:1, :], ln_ref[1:2, :])
    first = pl.program_id(0) < tiles_a

    @pl.when(first)
    def _():
        oa_ref[...] = out

    @pl.when(jnp.logical_not(first))
    def _():
        ob_ref[...] = out


def _combine(x, route, ye, ln, tm, n_a):
    n = x.shape[0]
    tiles_a = n_a // tm
    row = pl.BlockSpec((tm, D_MODEL), lambda i: (i, 0))
    rowmajor = lambda off: pl.BlockSpec((tm * ROW_TILES, LANE), lambda i: (i + off, 0))
    return pl.pallas_call(
        functools.partial(_combine_kernel, tiles_a=tiles_a),
        grid=(n // tm,),
        in_specs=[row, pl.BlockSpec((tm, LANE), lambda i: (i, 0)), rowmajor(0), rowmajor(n // tm),
                  pl.BlockSpec(ln.shape, lambda i: (0, 0))],
        out_specs=[pl.BlockSpec((tm, D_MODEL), lambda i: (jnp.minimum(i, tiles_a - 1), 0)),
                   pl.BlockSpec((tm, D_MODEL), lambda i: (jnp.maximum(i - tiles_a, 0), 0))],
        out_shape=[jax.ShapeDtypeStruct((n_a, D_MODEL), F32), jax.ShapeDtypeStruct((n - n_a, D_MODEL), F32)],
        compiler_params=_cparams("arbitrary"),
        name="moe_combine_ln",
    )(x, route, ye, ye, ln)


def _hier_moe_ln(x, x_rowmajor, wts, blk, n_a):
    n = x.shape[0]
    route, cnt = _router(x, wts["wr"], wts["br"], _tile(n, 512))
    counts = cnt[0, ROUTE_OFF:ROUTE_OFF + N_EXPERTS].astype(jnp.int32)
    pcounts = (counts + blk - 1) // blk * blk
    pend = jnp.cumsum(pcounts)
    pstart = pend - pcounts
    eidx = route[:, 0:2].astype(jnp.int32)
    dest = (pstart[eidx] + route[:, 4:6].astype(jnp.int32)).reshape(-1)
    n_slots = -(-(2 * n + N_EXPERTS * (blk - 1)) // blk) * blk
    n_blk = n_slots // blk
    blk_start = jnp.arange(n_blk, dtype=jnp.int32) * blk
    blk_e = jnp.minimum(jnp.sum(pend[None, :] <= blk_start[:, None], axis=1), N_EXPERTS - 1).astype(jnp.int32)
    n_act = (pend[-1:] // blk).astype(jnp.int32)
    code = _slotmap(dest, pstart + counts, pend, n, n_slots, blk)
    ye = _experts(code, blk_e, n_act, x_rowmajor, wts["w1"], wts["w3"], wts["w2"], wts["l"], blk)
    return _combine(x, route, ye, wts["ln3"], _tile(math.gcd(n_a, n - n_a), 256), n_a)


def _mixer_attn(x, mk, mv, conv_st, shift_st, wkv, wts, n_total, row_off, prev):
    b, t, _ = x.shape
    n = b * t
    x2 = x.reshape(n, D_MODEL)
    p = _proj(x2, wts["w_in"], _tile(n, 512), D_IN_PAD // 3).reshape(b, t, D_IN_PAD)
    shift_pad = jnp.pad(shift_st, ((0, 0), (0, 0), (0, D_SHIFT_PAD - D_SHIFT)))
    C = min(CHUNK, t)
    consts = _rwkv_consts(C)
    yc, yr, nconv, nshift, nwkv = _mixer(p, conv_st, shift_pad, wkv, wts["pvec"], wts["mu"], wts["wl"], wts["hvec"], consts,
                                         _tile(t, 256), C)
    tm = _tile(n, 512)
    x1 = _out_ln(x2, yc.reshape(n, D_CONV), yr.reshape(n, D_RWKV), wts["w_out"], wts["ln1"], tm)
    bufs = _attn(x1.reshape(b, t, D_MODEL), mk, mv, wts["xa_q"], wts["xa_o"], wts["ln2"], _tile(t, 256), n_total, row_off, prev)
    return bufs, nconv, nshift[:, :, :D_SHIFT], nwkv


def _ln_rows(g, b):
    return jnp.zeros((8, D_MODEL), F32).at[0].set(g).at[1].set(b)


def kernel(x_prompt, x_sample, mem_prompt, cache_mem_k, cache_mem_v, state_conv, state_shift, state_wkv, w_in, conv_w, shift_mu, decay_w0, decay_w2, iclr_a0, iclr_a2, gate_g2, k_k, k_a, r_k, lnx_g, lnx_b, w_out, ln1_g, ln1_b, xa_q, xa_k, xa_v, xa_o, ln2_g, ln2_b, router_grp, router_grp_b, router_sub, router_sub_b, moe_w1, moe_w3, moe_w2, ln3_g, ln3_b):
    depth = w_in.shape[0]
    bp = x_prompt.shape[0]
    bs = x_sample.shape[0]
    yp, ys = x_prompt, x_sample
    outs = [[] for _ in range(8)]
    for l in range(depth):
        wl = jnp.zeros((D_LORA_PAD, 3 * D_RWKV), F32)
        wl = wl.at[0:DECAY_LORA, 0:D_RWKV].set(decay_w2[l])
        wl = wl.at[DECAY_LORA:DECAY_LORA + A_LORA, D_RWKV:2 * D_RWKV].set(iclr_a2[l])
        wl = wl.at[DECAY_LORA + A_LORA:D_LORA, 2 * D_RWKV:].set(gate_g2[l])
        wr = jnp.zeros((D_MODEL, LANE), F32)
        wr = wr.at[:, 0:N_GROUPS].set(router_grp[l])
        wr = wr.at[:, ROUTE_OFF:ROUTE_OFF + N_EXPERTS].set(jnp.transpose(router_sub[l], (1, 0, 2)).reshape(D_MODEL, N_EXPERTS))
        br = jnp.zeros((1, LANE), F32)
        br = br.at[0, 0:N_GROUPS].set(router_grp_b[l]).at[0, ROUTE_OFF:ROUTE_OFF + N_EXPERTS].set(router_sub_b[l].reshape(-1))
        wts = dict(
            w_in=jnp.pad(w_in[l], ((0, 0), (0, D_IN_PAD - D_IN))).astype(BF16),
            pvec=jnp.zeros((8, D_RWKV), F32).at[0].set(decay_w0[l]).at[1].set(iclr_a0[l]).at[2].set(k_k[l]).at[3].set(k_a[l]).at[4:7].set(conv_w[l]),
            mu=jnp.pad(shift_mu[l], (0, D_SHIFT_PAD - D_SHIFT)).reshape(1, D_SHIFT_PAD),
            wl=wl.astype(BF16),
            hvec=jnp.zeros((8, D_RWKV), F32).at[0].set(r_k[l].reshape(-1)).at[1].set(lnx_g[l]).at[2].set(lnx_b[l]),
            w_out=w_out[l].astype(BF16), ln1=_ln_rows(ln1_g[l], ln1_b[l]),
            xa_q=xa_q[l].astype(BF16), xa_o=xa_o[l].astype(BF16), ln2=_ln_rows(ln2_g[l], ln2_b[l]),
            wr=wr.astype(BF16), br=br,
            w1=moe_w1, w3=moe_w3, w2=moe_w2, l=l, ln3=_ln_rows(ln3_g[l], ln3_b[l]),
        )
        mem2 = mem_prompt.reshape(bp * N_MEM, D_MODEL)
        mk, mk16 = (a.reshape(bp, N_MEM, D_MODEL) for a in _proj(mem2, xa_k[l].astype(BF16), _tile(bp * N_MEM, 256), 1024, True))
        mv, mv16 = (a.reshape(bp, N_MEM, D_MODEL) for a in _proj(mem2, xa_v[l].astype(BF16), _tile(bp * N_MEM, 256), 1024, True))
        n_p = yp.shape[0] * yp.shape[1]
        n_s = ys.shape[0] * ys.shape[1]
        bufs, c_p, s_p, w_p = _mixer_attn(yp, mk16, mv16, jnp.zeros((bp, 2, D_CONV), F32), jnp.zeros((bp, 1, D_SHIFT), F32),
                                          jnp.zeros((bp, RWKV_HEADS, RWKV_HEAD, RWKV_HEAD), F32), wts, n_p + n_s, 0, None)
        bufs, c_s, s_s, w_s = _mixer_attn(ys, cache_mem_k[l].reshape(bs, N_MEM, D_MODEL), cache_mem_v[l].reshape(bs, N_MEM, D_MODEL),
                                          state_conv[l], state_shift[l], state_wkv[l], wts, n_p + n_s, n_p, bufs)
        y_p, y_s = _hier_moe_ln(bufs[0], bufs[1], wts, MOE_BLOCK, n_p)
        yp = y_p.reshape(yp.shape)
        ys = y_s.reshape(ys.shape)
        for o, val in zip(outs, (mk.reshape(bp, N_MEM, XA_HEADS, XA_HEAD), mv.reshape(bp, N_MEM, XA_HEADS, XA_HEAD),
                                 c_p, s_p, w_p, c_s, s_s, w_s)):
            o.append(val)
    return (yp, ys) + tuple(jnp.stack(o) for o in outs)
```

```python
import functools
import math

import jax
import jax.numpy as jnp
from jax import lax
from jax.experimental import pallas as pl
from jax.experimental.pallas import tpu as pltpu

F32 = jnp.float32
BF16 = jnp.bfloat16

D_MODEL = 2048
D_CONV = 1024
D_RWKV = 1024
RWKV_HEAD = 64
RWKV_HEADS = 16
DECAY_LORA = 64
A_LORA = 64
GATE_LORA = 160
D_LORA = DECAY_LORA + A_LORA + GATE_LORA
D_SHIFT = 3 * D_RWKV + D_LORA
D_IN = 3 * D_CONV + D_SHIFT
LANE = 128
D_LORA_PAD = 384
D_SHIFT_PAD = 3 * D_RWKV + D_LORA_PAD
D_IN_PAD = 3 * D_CONV + D_SHIFT_PAD
N_MEM = 256
XA_HEADS = 4
XA_HEAD = 512
N_GROUPS = 4
EXP_PER_GROUP = 8
N_EXPERTS = 32
EXPERT_FF = 512
LN_EPS = 1e-5
GN_EPS = 64e-5
ALPHA = 2.0 ** 0.25
CHUNK = 64
ROUTE_OFF = N_GROUPS
MOE_BLOCK = 256
VMEM_LIMIT = 56 * 1024 * 1024
ROWS_PROJ = 512
ROWS_MIXER = 256
ROWS_ATTN = 512
ROWS_ROUTER = 512
ROWS_COMBINE = 256


def _cparams(*sem):
    return pltpu.CompilerParams(dimension_semantics=sem, vmem_limit_bytes=VMEM_LIMIT)


def _dot(a, b):
    return jnp.dot(a.astype(BF16), b.astype(BF16), preferred_element_type=F32)


def _dot_nt(a, b):
    return lax.dot_general(a.astype(BF16), b.astype(BF16), (((1,), (1,)), ((), ())), preferred_element_type=F32)


def _dot_tn(a, b):
    return lax.dot_general(a.astype(BF16), b.astype(BF16), (((0,), (0,)), ((), ())), preferred_element_type=F32)


def _split3(x):
    h1 = x.astype(BF16)
    r1 = x - h1.astype(F32)
    h2 = r1.astype(BF16)
    h3 = (r1 - h2.astype(F32)).astype(BF16)
    return h1, h2, h3


def _dot_exact_rhs(x, m3):
    return jnp.dot(jnp.concatenate(_split3(x), axis=1), m3, preferred_element_type=F32)


def _layer_norm(z, g, b):
    mu = jnp.mean(z, axis=-1, keepdims=True)
    d = z - mu
    var = jnp.mean(d * d, axis=-1, keepdims=True)
    return d * lax.rsqrt(var + LN_EPS) * g + b


def _sigmoid(x):
    return 1.0 / (1.0 + jnp.exp(-x))


def _tile(n, pref):
    t = min(n, pref)
    while t > 8 and (n % t or t % 8):
        t -= 8
    return t if n % t == 0 else n


def _proj_kernel(x_ref, w_ref, o_ref, *o16_ref):
    y = _dot(x_ref[...], w_ref[...])
    o_ref[...] = y
    for r in o16_ref:
        r[...] = y.astype(BF16)


def _proj(x, w, tm, tn, bf16_twin=False):
    n, k = x.shape
    m = w.shape[1]
    out = pl.BlockSpec((tm, tn), lambda j, i: (i, j))
    res = pl.pallas_call(
        _proj_kernel,
        grid=(m // tn, n // tm),
        in_specs=[pl.BlockSpec((tm, k), lambda j, i: (i, 0)), pl.BlockSpec((k, tn), lambda j, i: (0, j))],
        out_specs=[out, out] if bf16_twin else [out],
        out_shape=[jax.ShapeDtypeStruct((n, m), F32)] + ([jax.ShapeDtypeStruct((n, m), BF16)] if bf16_twin else []),
        compiler_params=_cparams("arbitrary", "arbitrary"),
        name="proj",
    )(x, w)
    return res if bf16_twin else res[0]


def _shift_rows(x, heads):
    s = len(heads)
    out = pltpu.roll(x, s, axis=0)
    row = lax.broadcasted_iota(jnp.int32, x.shape, 0)
    for i, h in enumerate(heads):
        out = jnp.where(row == i, h, out)
    return out


def _segsum64(x, e):
    return jnp.concatenate([_dot_exact_rhs(x[:, j * LANE:(j + 1) * LANE], e) for j in range(x.shape[1] // LANE)], axis=1)


def _mixer_inputs(rows, p_ref, pv_ref, mu_ref, wl_ref, e_ref, yc_ref, nconv_ref, nshift_ref, cu, cp,
                  r_ref, k_ref, v_ref, w_ref, kk_ref, bb_ref, g_ref):
    nr = rows.stop - rows.start
    bg = p_ref[0, rows, 0:D_CONV]
    u = p_ref[0, rows, D_CONV:2 * D_CONV] * p_ref[0, rows, 2 * D_CONV:3 * D_CONV]
    u1 = _shift_rows(u, [cu[1:2, :]])
    u2 = _shift_rows(u, [cu[0:1, :], cu[1:2, :]])
    yc_ref[0, rows] = bg * (pv_ref[4:5, :] * u2 + pv_ref[5:6, :] * u1 + pv_ref[6:7, :] * u)
    tail = u[nr - 2:nr, :]
    cu[0:2, :] = tail
    nconv_ref[0] = tail
    yield

    def shifted(c0, c1):
        prw = p_ref[0, rows, 3 * D_CONV + c0:3 * D_CONV + c1]
        prev = _shift_rows(prw, [cp[0:1, c0:c1]])
        return prw + mu_ref[:, c0:c1] * (prev - prw)

    r = shifted(0, D_RWKV)
    r_ref[rows] = r
    yield
    k = shifted(D_RWKV, 2 * D_RWKV)
    yield
    v_ref[rows] = shifted(2 * D_RWKV, 3 * D_RWKV)
    yield
    ql = shifted(3 * D_RWKV, D_SHIFT_PAD)
    last = p_ref[0, rows.stop - 1:rows.stop, 3 * D_CONV:]
    cp[0:1, :] = last
    nshift_ref[0] = last
    lane = lax.broadcasted_iota(jnp.int32, ql.shape, 1)
    act = jnp.where(lane < DECAY_LORA, jnp.tanh(ql), jnp.where(lane < DECAY_LORA + A_LORA, ql, _sigmoid(ql)))
    lo = _dot(act, wl_ref[...])
    g_ref[rows] = lo[:, 2 * D_RWKV:]
    yield
    z = -(pv_ref[0:1, :] + lo[:, 0:D_RWKV])
    w_log = -(jnp.maximum(z, 0.0) + jnp.log(1.0 + jnp.exp(-jnp.abs(z)))) - 0.5
    w_ref[rows] = -jnp.exp(w_log)
    yield
    a = _sigmoid(pv_ref[1:2, :] + lo[:, D_RWKV:2 * D_RWKV])
    k_ref[rows] = k * (1.0 + (a - 1.0) * pv_ref[3:4, :])
    yield
    kk = k * pv_ref[2:3, :]
    kk = kk * lax.rsqrt(jnp.maximum(_segsum64(kk * kk, e_ref[...]), 1e-24))
    kk_ref[rows] = kk
    bb_ref[rows] = kk * a


N_PAIRS = RWKV_HEADS // 2


def _mixer_kernel(p_ref, cst_ref, sst_ref, s0_ref, pv_ref, mu_ref, wl_ref, hv_ref, ms_ref, mi_ref, lt_ref, e_ref,
                  yc_ref, y_ref, nconv_ref, nshift_ref, sout_ref,
                  cu, cp, S, r_ref, k_ref, v_ref, w_ref, kk_ref, bb_ref, g_ref, *, C):
    t = pl.program_id(1)
    nt = pl.num_programs(1)
    H = RWKV_HEAD
    lane = lax.broadcasted_iota(jnp.int32, (1, LANE), 1)
    m0 = lane < H

    @pl.when(t == 0)
    def _():
        cu[0:2, :] = cst_ref[0]
        cp[0:1, :] = sst_ref[0]
        z = jnp.zeros((H, H), F32)
        for hp in range(N_PAIRS):
            S[hp, 0:H, :] = jnp.concatenate([s0_ref[0, 2 * hp], z], axis=1)
            S[hp, H:2 * H, :] = jnp.concatenate([z, s0_ref[0, 2 * hp + 1]], axis=1)

    pair = lambda x: jnp.concatenate([jnp.where(m0, x, 0.0), jnp.where(m0, 0.0, x)], axis=0)

    def pair_chunk(hp, sl):
        ln = slice(hp * LANE, (hp + 1) * LANE)
        rk = hv_ref[0:1, ln]
        lng = hv_ref[1:2, ln]
        lnb = hv_ref[2:3, ln]
        r = r_ref[sl, ln]
        w = w_ref[sl, ln]
        k = k_ref[sl, ln]
        v = v_ref[sl, ln]
        kk = kk_ref[sl, ln]
        bb = bb_ref[sl, ln]
        cw = jnp.dot(lt_ref[...], jnp.concatenate(_split3(w), axis=0), preferred_element_type=F32)
        yield
        p = jnp.exp(cw)
        ip = jnp.exp(-cw)
        pc = p[C - 1:C, :]
        la = pair(-(jnp.exp(cw - w) * kk))
        lr = pair(p * r)
        rb = pair(bb * ip)
        rkk = pair(k * ip)
        vst = pair(v)
        lhs = jnp.concatenate([la, lr], axis=0)
        rhs = jnp.concatenate([rb, rkk], axis=0)
        gram = _dot_nt(lhs, rhs)
        sm = S[hp]
        ws = _dot_nt(lhs, sm)
        yield
        ms = ms_ref[...] > 0.5
        mi = mi_ref[...] > 0.5
        a_ab = jnp.where(ms, gram[:2 * C, :2 * C], 0.0)
        a_ak = jnp.where(ms, gram[:2 * C, 2 * C:], 0.0)
        a_rb = jnp.where(mi, gram[2 * C:, :2 * C], 0.0)
        a_rk = jnp.where(mi, gram[2 * C:, 2 * C:], 0.0)
        x = ws[:2 * C] + _dot(a_ak, vst)
        tm = (mi_ref[...] - ms_ref[...]) + a_ab
        pw = a_ab
        n = 1
        while 2 * n < C:
            pw = _dot(pw, pw)
            yield
            tm = tm + _dot(pw, tm)
            yield
            n *= 2
        u = _dot(tm, x)
        yield
        uv = jnp.concatenate([u, vst], axis=0)
        ost = ws[2 * C:] + _dot(jnp.concatenate([a_rb, a_rk], axis=1), uv)
        o = ost[:C] + ost[C:]
        S[hp] = sm * pc + _dot_tn(uv, rhs * pc)
        yield
        e3 = e_ref[...]
        sums = _dot_exact_rhs(jnp.concatenate([o, r * k * rk], axis=0), e3)
        yield
        d = o - sums[:C] * (1.0 / H)
        var = _dot_exact_rhs(d * d, e3) * (1.0 / H)
        yield
        on = d * lax.rsqrt(var + GN_EPS) * lng + lnb
        y_ref[0, sl, ln] = (on + sums[C:] * v) * g_ref[sl, ln]

    def inputs(rows):
        return _mixer_inputs(rows, p_ref, pv_ref, mu_ref, wl_ref, e_ref, yc_ref, nconv_ref, nshift_ref, cu, cp,
                             r_ref, k_ref, v_ref, w_ref, kk_ref, bb_ref, g_ref)

    chunks = [slice(c * C, (c + 1) * C) for c in range(r_ref.shape[0] // C)]
    for _ in inputs(chunks[0]):
        pass
    for c, rows in enumerate(chunks):
        live = [pair_chunk(hp, rows) for hp in range(N_PAIRS)]
        if c + 1 < len(chunks):
            live.append(inputs(chunks[c + 1]))
        while live:
            live = [g for g in live if next(g, live) is not live]

    @pl.when(t == nt - 1)
    def _():
        for hp in range(N_PAIRS):
            sout_ref[0, 2 * hp] = S[hp, 0:H, 0:H]
            sout_ref[0, 2 * hp + 1] = S[hp, H:2 * H, H:2 * H]


def _mixer(p, conv_st, shift_st, s0, pvec, mu, wl, hvec, consts, tt, C):
    b, t, _ = p.shape
    ms, mi, lt, e3 = consts
    row = lambda w: pl.BlockSpec((1, tt, w), lambda i, j: (i, j, 0))
    st = lambda a: pl.BlockSpec((1,) + a.shape[1:], lambda i, j: (i,) + (0,) * (a.ndim - 1))
    full = lambda a: pl.BlockSpec(a.shape, lambda i, j: (0,) * a.ndim)
    seq = jax.ShapeDtypeStruct((b, t, D_RWKV), F32)
    return pl.pallas_call(
        functools.partial(_mixer_kernel, C=C),
        grid=(b, t // tt),
        in_specs=[row(D_IN_PAD), st(conv_st), st(shift_st), st(s0), full(pvec), full(mu), full(wl), full(hvec),
                  full(ms), full(mi), full(lt), full(e3)],
        out_specs=[row(D_CONV), row(D_RWKV), st(conv_st), st(shift_st), st(s0)],
        out_shape=[seq, seq, jax.ShapeDtypeStruct(conv_st.shape, F32), jax.ShapeDtypeStruct(shift_st.shape, F32),
                   jax.ShapeDtypeStruct(s0.shape, F32)],
        scratch_shapes=[pltpu.VMEM((8, D_CONV), F32), pltpu.VMEM((8, D_SHIFT_PAD), F32),
                        pltpu.VMEM((N_PAIRS, LANE, LANE), F32)] + [pltpu.VMEM((tt, D_RWKV), F32)] * 7,
        compiler_params=_cparams("arbitrary", "arbitrary"),
        name="mixer",
    )(p, conv_st, shift_st, s0, pvec, mu, wl, hvec, ms, mi, lt, e3)


def _rwkv_consts(C):
    i = jnp.arange(2 * C)
    same = (i[:, None] // C) == (i[None, :] // C)
    ms = (same & ((i[None, :] % C) < (i[:, None] % C))).astype(F32)
    mi = (same & ((i[None, :] % C) <= (i[:, None] % C))).astype(F32)
    j = jnp.arange(C)
    lt = jnp.tile((j[None, :] <= j[:, None]).astype(BF16), (1, 3))
    l = jnp.arange(LANE)
    e3 = jnp.tile(((l[:, None] // RWKV_HEAD) == (l[None, :] // RWKV_HEAD)).astype(BF16), (3, 1))
    return ms, mi, lt, e3


def _out_ln_kernel(x_ref, yc_ref, yr_ref, w_ref, ln_ref, o_ref):
    z = ALPHA * x_ref[...] + _dot(yc_ref[...], w_ref[0:D_CONV, :]) + _dot(yr_ref[...], w_ref[D_CONV:, :])
    o_ref[...] = _layer_norm(z, ln_ref[0:1, :], ln_ref[1:2, :])


def _out_ln(x, yc, yr, w, ln, tm):
    n = x.shape[0]
    row = lambda wd: pl.BlockSpec((tm, wd), lambda i: (i, 0))
    full = lambda a: pl.BlockSpec(a.shape, lambda i: (0,) * a.ndim)
    return pl.pallas_call(
        _out_ln_kernel,
        grid=(n // tm,),
        in_specs=[row(D_MODEL), row(D_CONV), row(D_RWKV), full(w), full(ln)],
        out_specs=row(D_MODEL),
        out_shape=jax.ShapeDtypeStruct((n, D_MODEL), F32),
        compiler_params=_cparams("arbitrary"),
        name="out_proj_ln",
    )(x, yc, yr, w, ln)


ROW_TILES = D_MODEL // LANE


def _store_rowmajor(ref, val):
    for c in range(ROW_TILES):
        ref[pl.ds(c, val.shape[0], stride=ROW_TILES), :] = val[:, c * LANE:(c + 1) * LANE]


def _load_rowmajor(ref, rows):
    return jnp.concatenate([ref[pl.ds(c, rows, stride=ROW_TILES), :] for c in range(ROW_TILES)], axis=1)


def _attn_kernel(x_ref, mk_ref, mv_ref, wq_ref, wo_ref, ln_ref, *rest):
    o_ref, orm_ref = rest[-2:]
    nb, tq, _ = x_ref.shape
    x = x_ref[...].reshape(nb * tq, D_MODEL)
    q = _dot(x, wq_ref[...])
    seqs = []
    for s_i in range(nb):
        heads = []
        for h in range(XA_HEADS):
            sl = slice(h * XA_HEAD, (h + 1) * XA_HEAD)
            s = _dot_nt(q[s_i * tq:(s_i + 1) * tq, sl], mk_ref[s_i, :, sl]) * (XA_HEAD ** -0.5)
            s = s - jnp.max(s, axis=-1, keepdims=True)
            pe = jnp.exp(s)
            pr = pe / jnp.sum(pe, axis=-1, keepdims=True)
            heads.append(_dot(pr, mv_ref[s_i, :, sl]))
        seqs.append(jnp.concatenate(heads, axis=1))
    o = jnp.concatenate(seqs, axis=0)
    z = ALPHA * x + _dot(o, wo_ref[...])
    y = _layer_norm(z, ln_ref[0:1, :], ln_ref[1:2, :])
    o_ref[...] = y
    _store_rowmajor(orm_ref, y)


def _attn(x, mk, mv, wq, wo, ln, tq, n_total, row_off, prev):
    b, t, _ = x.shape
    nt = t // tq
    nb = 2 if (nt == 1 and b % 2 == 0 and tq % 8 == 0) else 1
    rows = nb * tq
    row = pl.BlockSpec((nb, tq, D_MODEL), lambda i, j: (i, j, 0))
    mem = pl.BlockSpec((nb, N_MEM, D_MODEL), lambda i, j: (i, 0, 0))
    full = lambda a: pl.BlockSpec(a.shape, lambda i, j: (0,) * a.ndim)
    out_idx = lambda i, j: (row_off // rows + i * nt + j, 0)
    args = (x, mk, mv, wq, wo, ln) + tuple(prev or ())
    return pl.pallas_call(
        _attn_kernel,
        grid=(b // nb, nt),
        in_specs=[row, mem, mem, full(wq), full(wo), full(ln)] + [pl.BlockSpec(memory_space=pl.ANY)] * (len(args) - 6),
        out_specs=[pl.BlockSpec((rows, D_MODEL), out_idx), pl.BlockSpec((rows * ROW_TILES, LANE), out_idx)],
        out_shape=[jax.ShapeDtypeStruct((n_total, D_MODEL), F32), jax.ShapeDtypeStruct((n_total * ROW_TILES, LANE), F32)],
        input_output_aliases={6: 0, 7: 1} if prev else {},
        compiler_params=_cparams("arbitrary", "arbitrary"),
        name="mem_attn_ln",
    )(*args)


def _router_kernel(x_ref, wr_ref, br_ref, lt_ref, route_ref, cnt_ref, cnt):
    i = pl.program_id(0)

    @pl.when(i == 0)
    def _():
        cnt[...] = jnp.zeros_like(cnt)

    neg = -1e30
    big = 1 << 20
    logits = _dot(x_ref[...], wr_ref[...]) + br_ref[...]
    lane = lax.broadcasted_iota(jnp.int32, logits.shape, 1)
    first_arg = lambda val, m: jnp.min(jnp.where(val == m, lane, big), axis=-1, keepdims=True)
    lg = jnp.where(lane < N_GROUPS, logits, neg)
    mg = jnp.max(lg, axis=-1, keepdims=True)
    grp = first_arg(lg, mg)
    p_grp = 1.0 / jnp.sum(jnp.exp(lg - mg), axis=-1, keepdims=True)
    lo = ROUTE_OFF + EXP_PER_GROUP * grp
    ls = jnp.where((lane >= lo) & (lane < lo + EXP_PER_GROUP), logits, neg)
    m1 = jnp.max(ls, axis=-1, keepdims=True)
    j1 = first_arg(ls, m1)
    ls2 = jnp.where(lane == j1, neg, ls)
    m2 = jnp.max(ls2, axis=-1, keepdims=True)
    j2 = first_arg(ls2, m2)
    t2 = jnp.exp(m2 - m1)
    g1 = p_grp * (1.0 / (1.0 + t2))
    g2 = p_grp * (t2 / (1.0 + t2))
    oh1 = lane == j1
    oh2 = lane == j2
    oh = jnp.where(oh1 | oh2, 1.0, 0.0)
    before = jnp.dot(lt_ref[...], oh.astype(BF16), preferred_element_type=F32) + cnt[0:1, :]
    rank1 = jnp.sum(jnp.where(oh1, before, 0.0), axis=-1, keepdims=True)
    rank2 = jnp.sum(jnp.where(oh2, before, 0.0), axis=-1, keepdims=True)
    total = cnt[0:1, :] + jnp.sum(oh, axis=0, keepdims=True)
    cnt[0:1, :] = total
    cnt_ref[...] = jnp.broadcast_to(total, cnt_ref.shape)
    e1 = (j1 - ROUTE_OFF).astype(F32)
    e2 = (j2 - ROUTE_OFF).astype(F32)
    out = jnp.zeros(logits.shape, F32)
    for col, val in enumerate((e1, e2, g1, g2, rank1, rank2)):
        out = jnp.where(lane == col, val, out)
    route_ref[...] = out


def _router(x, wr, br, tm):
    n = x.shape[0]
    j = jnp.arange(tm)
    lt = (j[None, :] < j[:, None]).astype(BF16)
    full = lambda a: pl.BlockSpec(a.shape, lambda i: (0,) * a.ndim)
    return pl.pallas_call(
        _router_kernel,
        grid=(n // tm,),
        in_specs=[pl.BlockSpec((tm, D_MODEL), lambda i: (i, 0)), full(wr), full(br), full(lt)],
        out_specs=[pl.BlockSpec((tm, LANE), lambda i: (i, 0)), pl.BlockSpec((8, LANE), lambda i: (0, 0))],
        out_shape=[jax.ShapeDtypeStruct((n, LANE), F32), jax.ShapeDtypeStruct((8, LANE), F32)],
        scratch_shapes=[pltpu.VMEM((8, LANE), F32)],
        compiler_params=_cparams("arbitrary"),
        name="router",
    )(x, wr, br, lt)


def _slotmap_kernel(dest_ref, lo_ref, hi_ref, code_ref, *, n_tok, blk):
    n_slots = code_ref.shape[0] - blk

    def spare(lo, hi):
        def body(j, c):
            code_ref[j] = 2 * n_tok + j
            return c

        lax.fori_loop(lo, hi, body, 0)

    spare(0, blk)
    for e in range(N_EXPERTS):
        spare(blk + lo_ref[e], blk + hi_ref[e])
    spare(blk + hi_ref[N_EXPERTS - 1], blk + n_slots)

    def put(a, c):
        code_ref[blk + dest_ref[a]] = a
        return c

    lax.fori_loop(0, 2 * n_tok, put, 0, unroll=16)


def _slotmap(dest, pad_lo, pad_hi, n_tok, n_slots, blk):
    smem = pl.BlockSpec(memory_space=pltpu.SMEM)
    return pl.pallas_call(
        functools.partial(_slotmap_kernel, n_tok=n_tok, blk=blk),
        in_specs=[smem, smem, smem],
        out_specs=smem,
        out_shape=jax.ShapeDtypeStruct((n_slots + blk,), jnp.int32),
        name="moe_slot_map",
    )(dest, pad_lo, pad_hi)


def _expert_kernel(code_ref, be_ref, na_ref, x_hbm, w1_ref, w3_ref, w2_ref, out_hbm,
                   xbuf, ybuf, gsem, ssem, w1s, w3s, w2s, *, blk, n_tok):
    i = pl.program_id(0)
    na = na_ref[0]
    par = i % 2
    oth = 1 - par
    rows = range(blk)

    def g_copy(p, code, r):
        tok = jnp.where(code >= 2 * n_tok, 0, lax.shift_right_logical(code, jnp.int32(1)))
        src = x_hbm.at[pl.ds(pl.multiple_of(tok * ROW_TILES, ROW_TILES), ROW_TILES)]
        return pltpu.make_async_copy(src, xbuf.at[p, pl.ds(r * ROW_TILES, ROW_TILES)], gsem.at[p])

    def s_copy(p, r, code):
        row = jnp.where(code >= 2 * n_tok, code, (code & 1) * n_tok + lax.shift_right_logical(code, jnp.int32(1)))
        dst = out_hbm.at[pl.ds(pl.multiple_of(row * ROW_TILES, ROW_TILES), ROW_TILES)]
        return pltpu.make_async_copy(ybuf.at[p, pl.ds(r * ROW_TILES, ROW_TILES)], dst, ssem.at[p])

    @pl.when(i == 0)
    def _():
        for r in rows:
            g_copy(0, code_ref[blk + r], r).start(priority=r % 2)
        ybuf[1] = jnp.zeros(ybuf.shape[1:], F32)

    @pl.when(i < na)
    def _():
        for r in rows:
            g_copy(par, 0, r).wait()

        @pl.when(i >= 1)
        def _():
            for r in rows:
                s_copy(par, r, 0).wait()

        @pl.when((i == 0) | (be_ref[i] != be_ref[jnp.maximum(i - 1, 0)]))
        def _():
            w1s[...] = w1_ref[0, 0].astype(BF16)
            w3s[...] = w3_ref[0, 0].astype(BF16)
            w2s[...] = w2_ref[0, 0].astype(BF16)

        x = _load_rowmajor(xbuf.at[par], blk).astype(BF16)
        nxt = (jnp.minimum(i + 1, na - 1) + 1) * blk
        for r in rows:
            g_copy(oth, code_ref[nxt + r], r).start(priority=r % 2)
            s_copy(oth, r, code_ref[i * blk + r]).start(priority=1 - r % 2)
        h1 = jnp.dot(x, w1s[...], preferred_element_type=F32)
        h3 = jnp.dot(x, w3s[...], preferred_element_type=F32)
        _store_rowmajor(ybuf.at[par], jnp.dot((h1 * _sigmoid(h1) * h3).astype(BF16), w2s[...], preferred_element_type=F32))

        @pl.when(i == na - 1)
        def _():
            for r in rows:
                s_copy(par, r, code_ref[(i + 1) * blk + r]).start(priority=r % 2)
            for r in rows:
                g_copy(oth, 0, r).wait()
                s_copy(oth, r, 0).wait()
                s_copy(par, r, 0).wait()


def _experts(code, blk_e, n_act, x, w1, w3, w2, l, blk):
    n = x.shape[0] // ROW_TILES
    n_slots = code.shape[0] - blk
    wspec = lambda a: pl.BlockSpec((1, 1) + a.shape[2:], lambda i, c, be, na: (l, be[jnp.minimum(i, na[0] - 1)], 0, 0))
    return pl.pallas_call(
        functools.partial(_expert_kernel, blk=blk, n_tok=n),
        grid_spec=pltpu.PrefetchScalarGridSpec(
            num_scalar_prefetch=3,
            grid=(n_slots // blk,),
            in_specs=[pl.BlockSpec(memory_space=pl.ANY), wspec(w1), wspec(w3), wspec(w2)],
            out_specs=pl.BlockSpec(memory_space=pl.ANY),
            scratch_shapes=[pltpu.VMEM((2, blk * ROW_TILES, LANE), F32), pltpu.VMEM((2, blk * ROW_TILES, LANE), F32),
                            pltpu.SemaphoreType.DMA((2,)), pltpu.SemaphoreType.DMA((2,)),
                            pltpu.VMEM((D_MODEL, EXPERT_FF), BF16), pltpu.VMEM((D_MODEL, EXPERT_FF), BF16),
                            pltpu.VMEM((EXPERT_FF, D_MODEL), BF16)],
        ),
        out_shape=jax.ShapeDtypeStruct(((2 * n + n_slots + blk) * ROW_TILES, LANE), F32),
        compiler_params=_cparams("arbitrary"),
        name="moe_experts",
    )(code, blk_e, n_act, x, w1, w3, w2)


def _combine_kernel(x_ref, route_ref, y0_ref, y1_ref, ln_ref, oa_ref, ob_ref, *, tiles_a):
    tm = x_ref.shape[0]
    y = route_ref[:, 2:3] * _load_rowmajor(y0_ref, tm) + route_ref[:, 3:4] * _load_rowmajor(y1_ref, tm)
    out = _layer_norm(ALPHA * x_ref[...] + y, ln_ref[0:1, :], ln_ref[1:2, :])
    first = pl.program_id(0) < tiles_a

    @pl.when(first)
    def _():
        oa_ref[...] = out

    @pl.when(jnp.logical_not(first))
    def _():
        ob_ref[...] = out


def _combine(x, route, ye, ln, tm, n_a):
    n = x.shape[0]
    tiles_a = n_a // tm
    row = pl.BlockSpec((tm, D_MODEL), lambda i: (i, 0))
    rowmajor = lambda off: pl.BlockSpec((tm * ROW_TILES, LANE), lambda i: (i + off, 0))
    return pl.pallas_call(
        functools.partial(_combine_kernel, tiles_a=tiles_a),
        grid=(n // tm,),
        in_specs=[row, pl.BlockSpec((tm, LANE), lambda i: (i, 0)), rowmajor(0), rowmajor(n // tm),
                  pl.BlockSpec(ln.shape, lambda i: (0, 0))],
        out_specs=[pl.BlockSpec((tm, D_MODEL), lambda i: (jnp.minimum(i, tiles_a - 1), 0)),
                   pl.BlockSpec((tm, D_MODEL), lambda i: (jnp.maximum(i - tiles_a, 0), 0))],
        out_shape=[jax.ShapeDtypeStruct((n_a, D_MODEL), F32), jax.ShapeDtypeStruct((n - n_a, D_MODEL), F32)],
        compiler_params=_cparams("arbitrary"),
        name="moe_combine_ln",
    )(x, route, ye, ye, ln)


def _hier_moe_ln(x, x_rowmajor, wts, blk, n_a):
    n = x.shape[0]
    route, cnt = _router(x, wts["wr"], wts["br"], _tile(n, ROWS_ROUTER))
    counts = cnt[0, ROUTE_OFF:ROUTE_OFF + N_EXPERTS].astype(jnp.int32)
    pcounts = (counts + blk - 1) // blk * blk
    pend = jnp.cumsum(pcounts)
    pstart = pend - pcounts
    eidx = route[:, 0:2].astype(jnp.int32)
    dest = (pstart[eidx] + route[:, 4:6].astype(jnp.int32)).reshape(-1)
    n_slots = -(-(2 * n + N_EXPERTS * (blk - 1)) // blk) * blk
    n_blk = n_slots // blk
    blk_start = jnp.arange(n_blk, dtype=jnp.int32) * blk
    blk_e = jnp.minimum(jnp.sum(pend[None, :] <= blk_start[:, None], axis=1), N_EXPERTS - 1).astype(jnp.int32)
    n_act = (pend[-1:] // blk).astype(jnp.int32)
    code = _slotmap(dest, pstart + counts, pend, n, n_slots, blk)
    ye = _experts(code, blk_e, n_act, x_rowmajor, wts["w1"], wts["w3"], wts["w2"], wts["l"], blk)
    return _combine(x, route, ye, wts["ln3"], _tile(math.gcd(n_a, n - n_a), ROWS_COMBINE), n_a)


def _mixer_attn(x, mk, mv, conv_st, shift_st, wkv, wts, n_total, row_off, prev):
    b, t, _ = x.shape
    n = b * t
    x2 = x.reshape(n, D_MODEL)
    p = _proj(x2, wts["w_in"], _tile(n, ROWS_PROJ), D_IN_PAD // 3).reshape(b, t, D_IN_PAD)
    shift_pad = jnp.pad(shift_st, ((0, 0), (0, 0), (0, D_SHIFT_PAD - D_SHIFT)))
    C = min(CHUNK, t)
    consts = _rwkv_consts(C)
    yc, yr, nconv, nshift, nwkv = _mixer(p, conv_st, shift_pad, wkv, wts["pvec"], wts["mu"], wts["wl"], wts["hvec"], consts,
                                         _tile(t, ROWS_MIXER), C)
    tm = _tile(n, ROWS_PROJ)
    x1 = _out_ln(x2, yc.reshape(n, D_CONV), yr.reshape(n, D_RWKV), wts["w_out"], wts["ln1"], tm)
    bufs = _attn(x1.reshape(b, t, D_MODEL), mk, mv, wts["xa_q"], wts["xa_o"], wts["ln2"], _tile(t, ROWS_ATTN), n_total, row_off, prev)
    return bufs, nconv, nshift[:, :, :D_SHIFT], nwkv


def _ln_rows(g, b):
    return jnp.zeros((8, D_MODEL), F32).at[0].set(g).at[1].set(b)


def kernel(x_prompt, x_sample, mem_prompt, cache_mem_k, cache_mem_v, state_conv, state_shift, state_wkv, w_in, conv_w, shift_mu, decay_w0, decay_w2, iclr_a0, iclr_a2, gate_g2, k_k, k_a, r_k, lnx_g, lnx_b, w_out, ln1_g, ln1_b, xa_q, xa_k, xa_v, xa_o, ln2_g, ln2_b, router_grp, router_grp_b, router_sub, router_sub_b, moe_w1, moe_w3, moe_w2, ln3_g, ln3_b):
    depth = w_in.shape[0]
    bp = x_prompt.shape[0]
    bs = x_sample.shape[0]
    yp, ys = x_prompt, x_sample
    outs = [[] for _ in range(8)]
    for l in range(depth):
        wl = jnp.zeros((D_LORA_PAD, 3 * D_RWKV), F32)
        wl = wl.at[0:DECAY_LORA, 0:D_RWKV].set(decay_w2[l])
        wl = wl.at[DECAY_LORA:DECAY_LORA + A_LORA, D_RWKV:2 * D_RWKV].set(iclr_a2[l])
        wl = wl.at[DECAY_LORA + A_LORA:D_LORA, 2 * D_RWKV:].set(gate_g2[l])
        wr = jnp.zeros((D_MODEL, LANE), F32)
        wr = wr.at[:, 0:N_GROUPS].set(router_grp[l])
        wr = wr.at[:, ROUTE_OFF:ROUTE_OFF + N_EXPERTS].set(jnp.transpose(router_sub[l], (1, 0, 2)).reshape(D_MODEL, N_EXPERTS))
        br = jnp.zeros((1, LANE), F32)
        br = br.at[0, 0:N_GROUPS].set(router_grp_b[l]).at[0, ROUTE_OFF:ROUTE_OFF + N_EXPERTS].set(router_sub_b[l].reshape(-1))
        wts = dict(
            w_in=jnp.pad(w_in[l], ((0, 0), (0, D_IN_PAD - D_IN))).astype(BF16),
            pvec=jnp.zeros((8, D_RWKV), F32).at[0].set(decay_w0[l]).at[1].set(iclr_a0[l]).at[2].set(k_k[l]).at[3].set(k_a[l]).at[4:7].set(conv_w[l]),
            mu=jnp.pad(shift_mu[l], (0, D_SHIFT_PAD - D_SHIFT)).reshape(1, D_SHIFT_PAD),
            wl=wl.astype(BF16),
            hvec=jnp.zeros((8, D_RWKV), F32).at[0].set(r_k[l].reshape(-1)).at[1].set(lnx_g[l]).at[2].set(lnx_b[l]),
            w_out=w_out[l].astype(BF16), ln1=_ln_rows(ln1_g[l], ln1_b[l]),
            xa_q=xa_q[l].astype(BF16), xa_o=xa_o[l].astype(BF16), ln2=_ln_rows(ln2_g[l], ln2_b[l]),
            wr=wr.astype(BF16), br=br,
            w1=moe_w1, w3=moe_w3, w2=moe_w2, l=l, ln3=_ln_rows(ln3_g[l], ln3_b[l]),
        )
        mem2 = mem_prompt.reshape(bp * N_MEM, D_MODEL)
        mk, mk16 = (a.reshape(bp, N_MEM, D_MODEL) for a in _proj(mem2, xa_k[l].astype(BF16), _tile(bp * N_MEM, ROWS_PROJ), D_MODEL // 2, True))
        mv, mv16 = (a.reshape(bp, N_MEM, D_MODEL) for a in _proj(mem2, xa_v[l].astype(BF16), _tile(bp * N_MEM, ROWS_PROJ), D_MODEL // 2, True))
        n_p = yp.shape[0] * yp.shape[1]
        n_s = ys.shape[0] * ys.shape[1]
        bufs, c_p, s_p, w_p = _mixer_attn(yp, mk16, mv16, jnp.zeros((bp, 2, D_CONV), F32), jnp.zeros((bp, 1, D_SHIFT), F32),
                                          jnp.zeros((bp, RWKV_HEADS, RWKV_HEAD, RWKV_HEAD), F32), wts, n_p + n_s, 0, None)
        bufs, c_s, s_s, w_s = _mixer_attn(ys, cache_mem_k[l].reshape(bs, N_MEM, D_MODEL), cache_mem_v[l].reshape(bs, N_MEM, D_MODEL),
                                          state_conv[l], state_shift[l], state_wkv[l], wts, n_p + n_s, n_p, bufs)
        y_p, y_s = _hier_moe_ln(bufs[0], bufs[1], wts, MOE_BLOCK, n_p)
        yp = y_p.reshape(yp.shape)
        ys = y_s.reshape(ys.shape)
        for o, val in zip(outs, (mk.reshape(bp, N_MEM, XA_HEADS, XA_HEAD), mv.reshape(bp, N_MEM, XA_HEADS, XA_HEAD),
                                 c_p, s_p, w_p, c_s, s_s, w_s)):
            o.append(val)
    return (yp, ys) + tuple(jnp.stack(o) for o in outs)
```

```python
import functools
import math

import jax
import jax.numpy as jnp
from jax import lax
from jax.experimental import pallas as pl
from jax.experimental.pallas import tpu as pltpu

F32 = jnp.float32
BF16 = jnp.bfloat16

D_MODEL = 2048
D_CONV = 1024
D_RWKV = 1024
RWKV_HEAD = 64
RWKV_HEADS = 16
DECAY_LORA = 64
A_LORA = 64
GATE_LORA = 160
D_LORA = DECAY_LORA + A_LORA + GATE_LORA
D_SHIFT = 3 * D_RWKV + D_LORA
D_IN = 3 * D_CONV + D_SHIFT
LANE = 128
D_LORA_PAD = 384
D_SHIFT_PAD = 3 * D_RWKV + D_LORA_PAD
D_IN_PAD = 3 * D_CONV + D_SHIFT_PAD
N_MEM = 256
XA_HEADS = 4
XA_HEAD = 512
N_GROUPS = 4
EXP_PER_GROUP = 8
N_EXPERTS = 32
EXPERT_FF = 512
LN_EPS = 1e-5
GN_EPS = 64e-5
ALPHA = 2.0 ** 0.25
CHUNK = 64
ROUTE_OFF = N_GROUPS
MOE_BLOCK = 256
VMEM_LIMIT = 56 * 1024 * 1024
ROWS_INPROJ = 1024
ROWS_PROJ = 512
ROWS_MIXER = 256
ROWS_ATTN = 512
ROWS_ROUTER = 1024
ROWS_COMBINE = 512


def _cparams(*sem):
    return pltpu.CompilerParams(dimension_semantics=sem, vmem_limit_bytes=VMEM_LIMIT)


def _dot(a, b):
    return jnp.dot(a.astype(BF16), b.astype(BF16), preferred_element_type=F32)


def _dot_nt(a, b):
    return lax.dot_general(a.astype(BF16), b.astype(BF16), (((1,), (1,)), ((), ())), preferred_element_type=F32)


def _dot_tn(a, b):
    return lax.dot_general(a.astype(BF16), b.astype(BF16), (((0,), (0,)), ((), ())), preferred_element_type=F32)


def _split3(x):
    h1 = x.astype(BF16)
    r1 = x - h1.astype(F32)
    h2 = r1.astype(BF16)
    h3 = (r1 - h2.astype(F32)).astype(BF16)
    return h1, h2, h3


def _dot_exact_rhs(x, m3):
    return jnp.dot(jnp.concatenate(_split3(x), axis=1), m3, preferred_element_type=F32)


def _layer_norm(z, g, b):
    mu = jnp.mean(z, axis=-1, keepdims=True)
    d = z - mu
    var = jnp.mean(d * d, axis=-1, keepdims=True)
    return d * lax.rsqrt(var + LN_EPS) * g + b


def _sigmoid(x):
    return 1.0 / (1.0 + jnp.exp(-x))


def _tile(n, pref):
    t = min(n, pref)
    while t > 8 and (n % t or t % 8):
        t -= 8
    return t if n % t == 0 else n


def _proj_kernel(x_ref, w_ref, o_ref, *o16_ref):
    y = _dot(x_ref[...], w_ref[...])
    o_ref[...] = y
    for r in o16_ref:
        r[...] = y.astype(BF16)


def _proj(x, w, tm, tn, bf16_twin=False):
    n, k = x.shape
    m = w.shape[1]
    out = pl.BlockSpec((tm, tn), lambda j, i: (i, j))
    res = pl.pallas_call(
        _proj_kernel,
        grid=(m // tn, n // tm),
        in_specs=[pl.BlockSpec((tm, k), lambda j, i: (i, 0)), pl.BlockSpec((k, tn), lambda j, i: (0, j))],
        out_specs=[out, out] if bf16_twin else [out],
        out_shape=[jax.ShapeDtypeStruct((n, m), F32)] + ([jax.ShapeDtypeStruct((n, m), BF16)] if bf16_twin else []),
        compiler_params=_cparams("arbitrary", "arbitrary"),
        name="proj",
    )(x, w)
    return res if bf16_twin else res[0]


def _shift_rows(x, heads):
    s = len(heads)
    out = pltpu.roll(x, s, axis=0)
    row = lax.broadcasted_iota(jnp.int32, x.shape, 0)
    for i, h in enumerate(heads):
        out = jnp.where(row == i, h, out)
    return out


def _segsum64(x, e):
    return jnp.concatenate([_dot_exact_rhs(x[:, j * LANE:(j + 1) * LANE], e) for j in range(x.shape[1] // LANE)], axis=1)


def _mixer_inputs(rows, p_ref, pv_ref, mu_ref, wl_ref, e_ref, yc_ref, nconv_ref, nshift_ref, cu, cp,
                  r_ref, k_ref, v_ref, w_ref, kk_ref, bb_ref, g_ref):
    nr = rows.stop - rows.start
    bg = p_ref[0, rows, 0:D_CONV]
    u = p_ref[0, rows, D_CONV:2 * D_CONV] * p_ref[0, rows, 2 * D_CONV:3 * D_CONV]
    u1 = _shift_rows(u, [cu[1:2, :]])
    u2 = _shift_rows(u, [cu[0:1, :], cu[1:2, :]])
    yc_ref[0, rows] = bg * (pv_ref[4:5, :] * u2 + pv_ref[5:6, :] * u1 + pv_ref[6:7, :] * u)
    tail = u[nr - 2:nr, :]
    cu[0:2, :] = tail
    nconv_ref[0] = tail
    yield

    def shifted(c0, c1):
        prw = p_ref[0, rows, 3 * D_CONV + c0:3 * D_CONV + c1]
        prev = _shift_rows(prw, [cp[0:1, c0:c1]])
        return prw + mu_ref[:, c0:c1] * (prev - prw)

    r = shifted(0, D_RWKV)
    r_ref[rows] = r
    yield
    k = shifted(D_RWKV, 2 * D_RWKV)
    yield
    v_ref[rows] = shifted(2 * D_RWKV, 3 * D_RWKV)
    yield
    ql = shifted(3 * D_RWKV, D_SHIFT_PAD)
    last = p_ref[0, rows.stop - 1:rows.stop, 3 * D_CONV:]
    cp[0:1, :] = last
    nshift_ref[0] = last
    lane = lax.broadcasted_iota(jnp.int32, ql.shape, 1)
    act = jnp.where(lane < DECAY_LORA, jnp.tanh(ql), jnp.where(lane < DECAY_LORA + A_LORA, ql, _sigmoid(ql)))
    lo = _dot(act, wl_ref[...])
    g_ref[rows] = lo[:, 2 * D_RWKV:]
    yield
    z = -(pv_ref[0:1, :] + lo[:, 0:D_RWKV])
    w_log = -(jnp.maximum(z, 0.0) + jnp.log(1.0 + jnp.exp(-jnp.abs(z)))) - 0.5
    w_ref[rows] = -jnp.exp(w_log)
    yield
    a = _sigmoid(pv_ref[1:2, :] + lo[:, D_RWKV:2 * D_RWKV])
    k_ref[rows] = k * (1.0 + (a - 1.0) * pv_ref[3:4, :])
    yield
    kk = k * pv_ref[2:3, :]
    kk = kk * lax.rsqrt(jnp.maximum(_segsum64(kk * kk, e_ref[...]), 1e-24))
    kk_ref[rows] = kk
    bb_ref[rows] = kk * a


N_PAIRS = RWKV_HEADS // 2


def _mixer_kernel(p_ref, cst_ref, sst_ref, s0_ref, pv_ref, mu_ref, wl_ref, hv_ref, ms_ref, mi_ref, lt_ref, e_ref,
                  yc_ref, y_ref, nconv_ref, nshift_ref, sout_ref,
                  cu, cp, S, r_ref, k_ref, v_ref, w_ref, kk_ref, bb_ref, g_ref, *, C):
    t = pl.program_id(1)
    nt = pl.num_programs(1)
    H = RWKV_HEAD
    lane = lax.broadcasted_iota(jnp.int32, (1, LANE), 1)
    m0 = lane < H

    @pl.when(t == 0)
    def _():
        cu[0:2, :] = cst_ref[0]
        cp[0:1, :] = sst_ref[0]
        z = jnp.zeros((H, H), F32)
        for hp in range(N_PAIRS):
            S[hp, 0:H, :] = jnp.concatenate([s0_ref[0, 2 * hp], z], axis=1)
            S[hp, H:2 * H, :] = jnp.concatenate([z, s0_ref[0, 2 * hp + 1]], axis=1)

    pair = lambda x: jnp.concatenate([jnp.where(m0, x, 0.0), jnp.where(m0, 0.0, x)], axis=0)

    def pair_chunk(hp, sl):
        ln = slice(hp * LANE, (hp + 1) * LANE)
        rk = hv_ref[0:1, ln]
        lng = hv_ref[1:2, ln]
        lnb = hv_ref[2:3, ln]
        r = r_ref[sl, ln]
        w = w_ref[sl, ln]
        k = k_ref[sl, ln]
        v = v_ref[sl, ln]
        kk = kk_ref[sl, ln]
        bb = bb_ref[sl, ln]
        cw = jnp.dot(lt_ref[...], jnp.concatenate(_split3(w), axis=0), preferred_element_type=F32)
        yield
        p = jnp.exp(cw)
        ip = jnp.exp(-cw)
        pc = p[C - 1:C, :]
        la = pair(-(jnp.exp(cw - w) * kk))
        lr = pair(p * r)
        rb = pair(bb * ip)
        rkk = pair(k * ip)
        vst = pair(v)
        lhs = jnp.concatenate([la, lr], axis=0)
        rhs = jnp.concatenate([rb, rkk], axis=0)
        gram = _dot_nt(lhs, rhs)
        sm = S[hp]
        ws = _dot_nt(lhs, sm)
        yield
        ms = ms_ref[...] > 0.5
        mi = mi_ref[...] > 0.5
        a_ab = jnp.where(ms, gram[:2 * C, :2 * C], 0.0)
        a_ak = jnp.where(ms, gram[:2 * C, 2 * C:], 0.0)
        a_rb = jnp.where(mi, gram[2 * C:, :2 * C], 0.0)
        a_rk = jnp.where(mi, gram[2 * C:, 2 * C:], 0.0)
        x = ws[:2 * C] + _dot(a_ak, vst)
        tm = (mi_ref[...] - ms_ref[...]) + a_ab
        pw = a_ab
        n = 1
        while 2 * n < C:
            pw = _dot(pw, pw)
            yield
            tm = tm + _dot(pw, tm)
            yield
            n *= 2
        u = _dot(tm, x)
        yield
        uv = jnp.concatenate([u, vst], axis=0)
        ost = ws[2 * C:] + _dot(jnp.concatenate([a_rb, a_rk], axis=1), uv)
        o = ost[:C] + ost[C:]
        S[hp] = sm * pc + _dot_tn(uv, rhs * pc)
        yield
        e3 = e_ref[...]
        sums = _dot_exact_rhs(jnp.concatenate([o, r * k * rk], axis=0), e3)
        yield
        d = o - sums[:C] * (1.0 / H)
        var = _dot_exact_rhs(d * d, e3) * (1.0 / H)
        yield
        on = d * lax.rsqrt(var + GN_EPS) * lng + lnb
        y_ref[0, sl, ln] = (on + sums[C:] * v) * g_ref[sl, ln]

    def inputs(rows):
        return _mixer_inputs(rows, p_ref, pv_ref, mu_ref, wl_ref, e_ref, yc_ref, nconv_ref, nshift_ref, cu, cp,
                             r_ref, k_ref, v_ref, w_ref, kk_ref, bb_ref, g_ref)

    chunks = [slice(c * C, (c + 1) * C) for c in range(r_ref.shape[0] // C)]
    for _ in inputs(chunks[0]):
        pass
    for c, rows in enumerate(chunks):
        live = [pair_chunk(hp, rows) for hp in range(N_PAIRS)]
        if c + 1 < len(chunks):
            live.append(inputs(chunks[c + 1]))
        while live:
            live = [g for g in live if next(g, live) is not live]

    @pl.when(t == nt - 1)
    def _():
        for hp in range(N_PAIRS):
            sout_ref[0, 2 * hp] = S[hp, 0:H, 0:H]
            sout_ref[0, 2 * hp + 1] = S[hp, H:2 * H, H:2 * H]


def _mixer(p, conv_st, shift_st, s0, pvec, mu, wl, hvec, consts, tt, C):
    b, t, _ = p.shape
    ms, mi, lt, e3 = consts
    row = lambda w: pl.BlockSpec((1, tt, w), lambda i, j: (i, j, 0))
    st = lambda a: pl.BlockSpec((1,) + a.shape[1:], lambda i, j: (i,) + (0,) * (a.ndim - 1))
    full = lambda a: pl.BlockSpec(a.shape, lambda i, j: (0,) * a.ndim)
    seq = jax.ShapeDtypeStruct((b, t, D_RWKV), F32)
    return pl.pallas_call(
        functools.partial(_mixer_kernel, C=C),
        grid=(b, t // tt),
        in_specs=[row(D_IN_PAD), st(conv_st), st(shift_st), st(s0), full(pvec), full(mu), full(wl), full(hvec),
                  full(ms), full(mi), full(lt), full(e3)],
        out_specs=[row(D_CONV), row(D_RWKV), st(conv_st), st(shift_st), st(s0)],
        out_shape=[seq, seq, jax.ShapeDtypeStruct(conv_st.shape, F32), jax.ShapeDtypeStruct(shift_st.shape, F32),
                   jax.ShapeDtypeStruct(s0.shape, F32)],
        scratch_shapes=[pltpu.VMEM((8, D_CONV), F32), pltpu.VMEM((8, D_SHIFT_PAD), F32),
                        pltpu.VMEM((N_PAIRS, LANE, LANE), F32)] + [pltpu.VMEM((tt, D_RWKV), F32)] * 7,
        compiler_params=_cparams("arbitrary", "arbitrary"),
        name="mixer",
    )(p, conv_st, shift_st, s0, pvec, mu, wl, hvec, ms, mi, lt, e3)


def _rwkv_consts(C):
    i = jnp.arange(2 * C)
    same = (i[:, None] // C) == (i[None, :] // C)
    ms = (same & ((i[None, :] % C) < (i[:, None] % C))).astype(F32)
    mi = (same & ((i[None, :] % C) <= (i[:, None] % C))).astype(F32)
    j = jnp.arange(C)
    lt = jnp.tile((j[None, :] <= j[:, None]).astype(BF16), (1, 3))
    l = jnp.arange(LANE)
    e3 = jnp.tile(((l[:, None] // RWKV_HEAD) == (l[None, :] // RWKV_HEAD)).astype(BF16), (3, 1))
    return ms, mi, lt, e3


def _out_ln_kernel(x_ref, yc_ref, yr_ref, w_ref, ln_ref, o_ref):
    z = ALPHA * x_ref[...] + _dot(yc_ref[...], w_ref[0:D_CONV, :]) + _dot(yr_ref[...], w_ref[D_CONV:, :])
    o_ref[...] = _layer_norm(z, ln_ref[0:1, :], ln_ref[1:2, :])


def _out_ln(x, yc, yr, w, ln, tm):
    n = x.shape[0]
    row = lambda wd: pl.BlockSpec((tm, wd), lambda i: (i, 0))
    full = lambda a: pl.BlockSpec(a.shape, lambda i: (0,) * a.ndim)
    return pl.pallas_call(
        _out_ln_kernel,
        grid=(n // tm,),
        in_specs=[row(D_MODEL), row(D_CONV), row(D_RWKV), full(w), full(ln)],
        out_specs=row(D_MODEL),
        out_shape=jax.ShapeDtypeStruct((n, D_MODEL), F32),
        compiler_params=_cparams("arbitrary"),
        name="out_proj_ln",
    )(x, yc, yr, w, ln)


ROW_TILES = D_MODEL // LANE


def _store_rowmajor(ref, val):
    for c in range(ROW_TILES):
        ref[pl.ds(c, val.shape[0], stride=ROW_TILES), :] = val[:, c * LANE:(c + 1) * LANE]


def _load_rowmajor(ref, rows):
    return jnp.concatenate([ref[pl.ds(c, rows, stride=ROW_TILES), :] for c in range(ROW_TILES)], axis=1)


def _attn_kernel(x_ref, mk_ref, mv_ref, wq_ref, wo_ref, ln_ref, *rest):
    o_ref, orm_ref = rest[-2:]
    nb, tq, _ = x_ref.shape
    x = x_ref[...].reshape(nb * tq, D_MODEL)
    q = _dot(x, wq_ref[...])
    seqs = []
    for s_i in range(nb):
        heads = []
        for h in range(XA_HEADS):
            sl = slice(h * XA_HEAD, (h + 1) * XA_HEAD)
            s = _dot_nt(q[s_i * tq:(s_i + 1) * tq, sl], mk_ref[s_i, :, sl]) * (XA_HEAD ** -0.5)
            s = s - jnp.max(s, axis=-1, keepdims=True)
            pe = jnp.exp(s)
            pr = pe / jnp.sum(pe, axis=-1, keepdims=True)
            heads.append(_dot(pr, mv_ref[s_i, :, sl]))
        seqs.append(jnp.concatenate(heads, axis=1))
    o = jnp.concatenate(seqs, axis=0)
    z = ALPHA * x + _dot(o, wo_ref[...])
    y = _layer_norm(z, ln_ref[0:1, :], ln_ref[1:2, :])
    o_ref[...] = y
    _store_rowmajor(orm_ref, y)


def _attn(x, mk, mv, wq, wo, ln, tq, n_total, row_off, prev):
    b, t, _ = x.shape
    nt = t // tq
    nb = 2 if (nt == 1 and b % 2 == 0 and tq % 8 == 0) else 1
    rows = nb * tq
    row = pl.BlockSpec((nb, tq, D_MODEL), lambda i, j: (i, j, 0))
    mem = pl.BlockSpec((nb, N_MEM, D_MODEL), lambda i, j: (i, 0, 0))
    full = lambda a: pl.BlockSpec(a.shape, lambda i, j: (0,) * a.ndim)
    out_idx = lambda i, j: (row_off // rows + i * nt + j, 0)
    args = (x, mk, mv, wq, wo, ln) + tuple(prev or ())
    return pl.pallas_call(
        _attn_kernel,
        grid=(b // nb, nt),
        in_specs=[row, mem, mem, full(wq), full(wo), full(ln)] + [pl.BlockSpec(memory_space=pl.ANY)] * (len(args) - 6),
        out_specs=[pl.BlockSpec((rows, D_MODEL), out_idx), pl.BlockSpec((rows * ROW_TILES, LANE), out_idx)],
        out_shape=[jax.ShapeDtypeStruct((n_total, D_MODEL), F32), jax.ShapeDtypeStruct((n_total * ROW_TILES, LANE), F32)],
        input_output_aliases={6: 0, 7: 1} if prev else {},
        compiler_params=_cparams("arbitrary", "arbitrary"),
        name="mem_attn_ln",
    )(*args)


def _router_kernel(x_ref, wr_ref, br_ref, lt_ref, route_ref, cnt_ref, cnt):
    i = pl.program_id(0)

    @pl.when(i == 0)
    def _():
        cnt[...] = jnp.zeros_like(cnt)

    neg = -1e30
    big = 1 << 20
    logits = _dot(x_ref[...], wr_ref[...]) + br_ref[...]
    lane = lax.broadcasted_iota(jnp.int32, logits.shape, 1)
    first_arg = lambda val, m: jnp.min(jnp.where(val == m, lane, big), axis=-1, keepdims=True)
    lg = jnp.where(lane < N_GROUPS, logits, neg)
    mg = jnp.max(lg, axis=-1, keepdims=True)
    grp = first_arg(lg, mg)
    p_grp = 1.0 / jnp.sum(jnp.exp(lg - mg), axis=-1, keepdims=True)
    lo = ROUTE_OFF + EXP_PER_GROUP * grp
    ls = jnp.where((lane >= lo) & (lane < lo + EXP_PER_GROUP), logits, neg)
    m1 = jnp.max(ls, axis=-1, keepdims=True)
    j1 = first_arg(ls, m1)
    ls2 = jnp.where(lane == j1, neg, ls)
    m2 = jnp.max(ls2, axis=-1, keepdims=True)
    j2 = first_arg(ls2, m2)
    t2 = jnp.exp(m2 - m1)
    g1 = p_grp * (1.0 / (1.0 + t2))
    g2 = p_grp * (t2 / (1.0 + t2))
    oh1 = lane == j1
    oh2 = lane == j2
    oh = jnp.where(oh1 | oh2, 1.0, 0.0)
    before = jnp.dot(lt_ref[...], oh.astype(BF16), preferred_element_type=F32) + cnt[0:1, :]
    rank1 = jnp.sum(jnp.where(oh1, before, 0.0), axis=-1, keepdims=True)
    rank2 = jnp.sum(jnp.where(oh2, before, 0.0), axis=-1, keepdims=True)
    total = cnt[0:1, :] + jnp.sum(oh, axis=0, keepdims=True)
    cnt[0:1, :] = total
    cnt_ref[...] = jnp.broadcast_to(total, cnt_ref.shape)
    e1 = (j1 - ROUTE_OFF).astype(F32)
    e2 = (j2 - ROUTE_OFF).astype(F32)
    out = jnp.zeros(logits.shape, F32)
    for col, val in enumerate((e1, e2, g1, g2, rank1, rank2)):
        out = jnp.where(lane == col, val, out)
    route_ref[...] = out


def _router(x, wr, br, tm):
    n = x.shape[0]
    j = jnp.arange(tm)
    lt = (j[None, :] < j[:, None]).astype(BF16)
    full = lambda a: pl.BlockSpec(a.shape, lambda i: (0,) * a.ndim)
    return pl.pallas_call(
        _router_kernel,
        grid=(n // tm,),
        in_specs=[pl.BlockSpec((tm, D_MODEL), lambda i: (i, 0)), full(wr), full(br), full(lt)],
        out_specs=[pl.BlockSpec((tm, LANE), lambda i: (i, 0)), pl.BlockSpec((8, LANE), lambda i: (0, 0))],
        out_shape=[jax.ShapeDtypeStruct((n, LANE), F32), jax.ShapeDtypeStruct((8, LANE), F32)],
        scratch_shapes=[pltpu.VMEM((8, LANE), F32)],
        compiler_params=_cparams("arbitrary"),
        name="router",
    )(x, wr, br, lt)


def _slotmap_kernel(dest_ref, lo_ref, hi_ref, code_ref, *, n_tok, blk):
    n_slots = code_ref.shape[0] - blk

    def spare(lo, hi):
        def body(j, c):
            code_ref[j] = 2 * n_tok + j
            return c

        lax.fori_loop(lo, hi, body, 0)

    spare(0, blk)
    for e in range(N_EXPERTS):
        spare(blk + lo_ref[e], blk + hi_ref[e])
    spare(blk + hi_ref[N_EXPERTS - 1], blk + n_slots)

    def put(a, c):
        code_ref[blk + dest_ref[a]] = a
        return c

    lax.fori_loop(0, 2 * n_tok, put, 0, unroll=16)


def _slotmap(dest, pad_lo, pad_hi, n_tok, n_slots, blk):
    smem = pl.BlockSpec(memory_space=pltpu.SMEM)
    return pl.pallas_call(
        functools.partial(_slotmap_kernel, n_tok=n_tok, blk=blk),
        in_specs=[smem, smem, smem],
        out_specs=smem,
        out_shape=jax.ShapeDtypeStruct((n_slots + blk,), jnp.int32),
        name="moe_slot_map",
    )(dest, pad_lo, pad_hi)


def _expert_kernel(code_ref, be_ref, na_ref, x_hbm, w1_ref, w3_ref, w2_ref, out_hbm,
                   xbuf, ybuf, gsem, ssem, w1s, w3s, w2s, *, blk, n_tok):
    i = pl.program_id(0)
    na = na_ref[0]
    par = i % 2
    oth = 1 - par
    rows = range(blk)

    def g_copy(p, code, r):
        tok = jnp.where(code >= 2 * n_tok, 0, lax.shift_right_logical(code, jnp.int32(1)))
        src = x_hbm.at[pl.ds(pl.multiple_of(tok * ROW_TILES, ROW_TILES), ROW_TILES)]
        return pltpu.make_async_copy(src, xbuf.at[p, pl.ds(r * ROW_TILES, ROW_TILES)], gsem.at[p])

    def s_copy(p, r, code):
        row = jnp.where(code >= 2 * n_tok, code, (code & 1) * n_tok + lax.shift_right_logical(code, jnp.int32(1)))
        dst = out_hbm.at[pl.ds(pl.multiple_of(row * ROW_TILES, ROW_TILES), ROW_TILES)]
        return pltpu.make_async_copy(ybuf.at[p, pl.ds(r * ROW_TILES, ROW_TILES)], dst, ssem.at[p])

    @pl.when(i == 0)
    def _():
        for r in rows:
            g_copy(0, code_ref[blk + r], r).start(priority=r % 2)
        ybuf[1] = jnp.zeros(ybuf.shape[1:], F32)

    @pl.when(i < na)
    def _():
        for r in rows:
            g_copy(par, 0, r).wait()

        @pl.when(i >= 1)
        def _():
            for r in rows:
                s_copy(par, r, 0).wait()

        @pl.when((i == 0) | (be_ref[i] != be_ref[jnp.maximum(i - 1, 0)]))
        def _():
            w1s[...] = w1_ref[0, 0].astype(BF16)
            w3s[...] = w3_ref[0, 0].astype(BF16)
            w2s[...] = w2_ref[0, 0].astype(BF16)

        x = _load_rowmajor(xbuf.at[par], blk).astype(BF16)
        nxt = (jnp.minimum(i + 1, na - 1) + 1) * blk
        for r in rows:
            g_copy(oth, code_ref[nxt + r], r).start(priority=r % 2)
            s_copy(oth, r, code_ref[i * blk + r]).start(priority=1 - r % 2)
        h1 = jnp.dot(x, w1s[...], preferred_element_type=F32)
        h3 = jnp.dot(x, w3s[...], preferred_element_type=F32)
        _store_rowmajor(ybuf.at[par], jnp.dot((h1 * _sigmoid(h1) * h3).astype(BF16), w2s[...], preferred_element_type=F32))

        @pl.when(i == na - 1)
        def _():
            for r in rows:
                s_copy(par, r, code_ref[(i + 1) * blk + r]).start(priority=r % 2)
            for r in rows:
                g_copy(oth, 0, r).wait()
                s_copy(oth, r, 0).wait()
                s_copy(par, r, 0).wait()


def _experts(code, blk_e, n_act, x, w1, w3, w2, l, blk):
    n = x.shape[0] // ROW_TILES
    n_slots = code.shape[0] - blk
    wspec = lambda a: pl.BlockSpec((1, 1) + a.shape[2:], lambda i, c, be, na: (l, be[jnp.minimum(i, na[0] - 1)], 0, 0))
    return pl.pallas_call(
        functools.partial(_expert_kernel, blk=blk, n_tok=n),
        grid_spec=pltpu.PrefetchScalarGridSpec(
            num_scalar_prefetch=3,
            grid=(n_slots // blk,),
            in_specs=[pl.BlockSpec(memory_space=pl.ANY), wspec(w1), wspec(w3), wspec(w2)],
            out_specs=pl.BlockSpec(memory_space=pl.ANY),
            scratch_shapes=[pltpu.VMEM((2, blk * ROW_TILES, LANE), F32), pltpu.VMEM((2, blk * ROW_TILES, LANE), F32),
                            pltpu.SemaphoreType.DMA((2,)), pltpu.SemaphoreType.DMA((2,)),
                            pltpu.VMEM((D_MODEL, EXPERT_FF), BF16), pltpu.VMEM((D_MODEL, EXPERT_FF), BF16),
                            pltpu.VMEM((EXPERT_FF, D_MODEL), BF16)],
        ),
        out_shape=jax.ShapeDtypeStruct(((2 * n + n_slots + blk) * ROW_TILES, LANE), F32),
        compiler_params=_cparams("arbitrary"),
        name="moe_experts",
    )(code, blk_e, n_act, x, w1, w3, w2)


def _combine_kernel(x_ref, route_ref, y0_ref, y1_ref, ln_ref, oa_ref, ob_ref, *, tiles_a):
    tm = x_ref.shape[0]
    y = route_ref[:, 2:3] * _load_rowmajor(y0_ref, tm) + route_ref[:, 3:4] * _load_rowmajor(y1_ref, tm)
    out = _layer_norm(ALPHA * x_ref[...] + y, ln_ref[0:1, :], ln_ref[1:2, :])
    first = pl.program_id(0) < tiles_a

    @pl.when(first)
    def _():
        oa_ref[...] = out

    @pl.when(jnp.logical_not(first))
    def _():
        ob_ref[...] = out


def _combine(x, route, ye, ln, tm, n_a):
    n = x.shape[0]
    tiles_a = n_a // tm
    row = pl.BlockSpec((tm, D_MODEL), lambda i: (i, 0))
    rowmajor = lambda off: pl.BlockSpec((tm * ROW_TILES, LANE), lambda i: (i + off, 0))
    return pl.pallas_call(
        functools.partial(_combine_kernel, tiles_a=tiles_a),
        grid=(n // tm,),
        in_specs=[row, pl.BlockSpec((tm, LANE), lambda i: (i, 0)), rowmajor(0), rowmajor(n // tm),
                  pl.BlockSpec(ln.shape, lambda i: (0, 0))],
        out_specs=[pl.BlockSpec((tm, D_MODEL), lambda i: (jnp.minimum(i, tiles_a - 1), 0)),
                   pl.BlockSpec((tm, D_MODEL), lambda i: (jnp.maximum(i - tiles_a, 0), 0))],
        out_shape=[jax.ShapeDtypeStruct((n_a, D_MODEL), F32), jax.ShapeDtypeStruct((n - n_a, D_MODEL), F32)],
        compiler_params=_cparams("arbitrary"),
        name="moe_combine_ln",
    )(x, route, ye, ye, ln)


def _hier_moe_ln(x, x_rowmajor, wts, blk, n_a):
    n = x.shape[0]
    route, cnt = _router(x, wts["wr"], wts["br"], _tile(n, ROWS_ROUTER))
    counts = cnt[0, ROUTE_OFF:ROUTE_OFF + N_EXPERTS].astype(jnp.int32)
    pcounts = (counts + blk - 1) // blk * blk
    pend = jnp.cumsum(pcounts)
    pstart = pend - pcounts
    eidx = route[:, 0:2].astype(jnp.int32)
    dest = (pstart[eidx] + route[:, 4:6].astype(jnp.int32)).reshape(-1)
    n_slots = -(-(2 * n + N_EXPERTS * (blk - 1)) // blk) * blk
    n_blk = n_slots // blk
    blk_start = jnp.arange(n_blk, dtype=jnp.int32) * blk
    blk_e = jnp.minimum(jnp.sum(pend[None, :] <= blk_start[:, None], axis=1), N_EXPERTS - 1).astype(jnp.int32)
    n_act = (pend[-1:] // blk).astype(jnp.int32)
    code = _slotmap(dest, pstart + counts, pend, n, n_slots, blk)
    ye = _experts(code, blk_e, n_act, x_rowmajor, wts["w1"], wts["w3"], wts["w2"], wts["l"], blk)
    return _combine(x, route, ye, wts["ln3"], _tile(math.gcd(n_a, n - n_a), ROWS_COMBINE), n_a)


def _mixer_attn(x, mk, mv, conv_st, shift_st, wkv, wts, n_total, row_off, prev):
    b, t, _ = x.shape
    n = b * t
    x2 = x.reshape(n, D_MODEL)
    p = _proj(x2, wts["w_in"], _tile(n, ROWS_INPROJ), D_IN_PAD // 3).reshape(b, t, D_IN_PAD)
    shift_pad = jnp.pad(shift_st, ((0, 0), (0, 0), (0, D_SHIFT_PAD - D_SHIFT)))
    C = min(CHUNK, t)
    consts = _rwkv_consts(C)
    yc, yr, nconv, nshift, nwkv = _mixer(p, conv_st, shift_pad, wkv, wts["pvec"], wts["mu"], wts["wl"], wts["hvec"], consts,
                                         _tile(t, ROWS_MIXER), C)
    tm = _tile(n, ROWS_PROJ)
    x1 = _out_ln(x2, yc.reshape(n, D_CONV), yr.reshape(n, D_RWKV), wts["w_out"], wts["ln1"], tm)
    bufs = _attn(x1.reshape(b, t, D_MODEL), mk, mv, wts["xa_q"], wts["xa_o"], wts["ln2"], _tile(t, ROWS_ATTN), n_total, row_off, prev)
    return bufs, nconv, nshift[:, :, :D_SHIFT], nwkv


def _ln_rows(g, b):
    return jnp.zeros((8, D_MODEL), F32).at[0].set(g).at[1].set(b)


def kernel(x_prompt, x_sample, mem_prompt, cache_mem_k, cache_mem_v, state_conv, state_shift, state_wkv, w_in, conv_w, shift_mu, decay_w0, decay_w2, iclr_a0, iclr_a2, gate_g2, k_k, k_a, r_k, lnx_g, lnx_b, w_out, ln1_g, ln1_b, xa_q, xa_k, xa_v, xa_o, ln2_g, ln2_b, router_grp, router_grp_b, router_sub, router_sub_b, moe_w1, moe_w3, moe_w2, ln3_g, ln3_b):
    depth = w_in.shape[0]
    bp = x_prompt.shape[0]
    bs = x_sample.shape[0]
    yp, ys = x_prompt, x_sample
    outs = [[] for _ in range(8)]
    for l in range(depth):
        wl = jnp.zeros((D_LORA_PAD, 3 * D_RWKV), F32)
        wl = wl.at[0:DECAY_LORA, 0:D_RWKV].set(decay_w2[l])
        wl = wl.at[DECAY_LORA:DECAY_LORA + A_LORA, D_RWKV:2 * D_RWKV].set(iclr_a2[l])
        wl = wl.at[DECAY_LORA + A_LORA:D_LORA, 2 * D_RWKV:].set(gate_g2[l])
        wr = jnp.zeros((D_MODEL, LANE), F32)
        wr = wr.at[:, 0:N_GROUPS].set(router_grp[l])
        wr = wr.at[:, ROUTE_OFF:ROUTE_OFF + N_EXPERTS].set(jnp.transpose(router_sub[l], (1, 0, 2)).reshape(D_MODEL, N_EXPERTS))
        br = jnp.zeros((1, LANE), F32)
        br = br.at[0, 0:N_GROUPS].set(router_grp_b[l]).at[0, ROUTE_OFF:ROUTE_OFF + N_EXPERTS].set(router_sub_b[l].reshape(-1))
        wts = dict(
            w_in=jnp.pad(w_in[l], ((0, 0), (0, D_IN_PAD - D_IN))).astype(BF16),
            pvec=jnp.zeros((8, D_RWKV), F32).at[0].set(decay_w0[l]).at[1].set(iclr_a0[l]).at[2].set(k_k[l]).at[3].set(k_a[l]).at[4:7].set(conv_w[l]),
            mu=jnp.pad(shift_mu[l], (0, D_SHIFT_PAD - D_SHIFT)).reshape(1, D_SHIFT_PAD),
            wl=wl.astype(BF16),
            hvec=jnp.zeros((8, D_RWKV), F32).at[0].set(r_k[l].reshape(-1)).at[1].set(lnx_g[l]).at[2].set(lnx_b[l]),
            w_out=w_out[l].astype(BF16), ln1=_ln_rows(ln1_g[l], ln1_b[l]),
            xa_q=xa_q[l].astype(BF16), xa_o=xa_o[l].astype(BF16), ln2=_ln_rows(ln2_g[l], ln2_b[l]),
            wr=wr.astype(BF16), br=br,
            w1=moe_w1, w3=moe_w3, w2=moe_w2, l=l, ln3=_ln_rows(ln3_g[l], ln3_b[l]),
        )
        mem2 = mem_prompt.reshape(bp * N_MEM, D_MODEL)
        mk, mk16 = (a.reshape(bp, N_MEM, D_MODEL) for a in _proj(mem2, xa_k[l].astype(BF16), _tile(bp * N_MEM, ROWS_PROJ), D_MODEL // 2, True))
        mv, mv16 = (a.reshape(bp, N_MEM, D_MODEL) for a in _proj(mem2, xa_v[l].astype(BF16), _tile(bp * N_MEM, ROWS_PROJ), D_MODEL // 2, True))
        n_p = yp.shape[0] * yp.shape[1]
        n_s = ys.shape[0] * ys.shape[1]
        bufs, c_p, s_p, w_p = _mixer_attn(yp, mk16, mv16, jnp.zeros((bp, 2, D_CONV), F32), jnp.zeros((bp, 1, D_SHIFT), F32),
                                          jnp.zeros((bp, RWKV_HEADS, RWKV_HEAD, RWKV_HEAD), F32), wts, n_p + n_s, 0, None)
        bufs, c_s, s_s, w_s = _mixer_attn(ys, cache_mem_k[l].reshape(bs, N_MEM, D_MODEL), cache_mem_v[l].reshape(bs, N_MEM, D_MODEL),
                                          state_conv[l], state_shift[l], state_wkv[l], wts, n_p + n_s, n_p, bufs)
        y_p, y_s = _hier_moe_ln(bufs[0], bufs[1], wts, MOE_BLOCK, n_p)
        yp = y_p.reshape(yp.shape)
        ys = y_s.reshape(ys.shape)
        for o, val in zip(outs, (mk.reshape(bp, N_MEM, XA_HEADS, XA_HEAD), mv.reshape(bp, N_MEM, XA_HEADS, XA_HEAD),
                                 c_p, s_p, w_p, c_s, s_s, w_s)):
            o.append(val)
    return (yp, ys) + tuple(jnp.stack(o) for o in outs)
```
